```python
import jax
import jax.numpy as jnp
from jax import lax
import numpy as np


D_MODEL = 1024
BATCH = 8
SEQ = 4096
DEPTH = 2

CHUNK = 64
Q_BLOCK = 128
N_MEM = 256
MEM_HEADS = 4
MEM_W = D_MODEL // 4
MEM_HD = MEM_W // MEM_HEADS
SELF_W = D_MODEL - MEM_W
GLA_HEADS = 4
GLA_KDIM = SELF_W // 2
GLA_DK = GLA_KDIM // GLA_HEADS
GLA_DV = SELF_W // GLA_HEADS
GLA_GATE_RANK = 16
GLA_TAU = 16.0
FOX_HD = 64
FOX_HEADS = SELF_W // FOX_HD
N_EXPERTS = 16
N_GROUPS = 4
EXPERTS_PER_GROUP = N_EXPERTS // N_GROUPS
TOP_K = 2
D_EXPERT = D_MODEL // 2
N_A_LAYERS = DEPTH // 2
N_B_LAYERS = DEPTH - N_A_LAYERS
DN_ALPHA = (2.0 * DEPTH) ** 0.25
DN_BETA = (8.0 * DEPTH) ** -0.25
LN_EPS = 1e-5
RMS_EPS = 1e-6
A_IN = 2 * GLA_KDIM + 2 * SELF_W + GLA_GATE_RANK + MEM_W
A_SPLITS = (GLA_KDIM, 2 * GLA_KDIM, 2 * GLA_KDIM + SELF_W, 2 * GLA_KDIM + SELF_W + GLA_GATE_RANK, 2 * GLA_KDIM + 2 * SELF_W + GLA_GATE_RANK)
B_IN = 2 * SELF_W + MEM_W
B_SPLITS = (SELF_W, 2 * SELF_W)
KV_SHARED = 2 * SELF_W + FOX_HEADS
KV_SPLITS = (SELF_W, 2 * SELF_W)

kernel_name = 'yoco_gla_fox_grouped_moe_block'


def layer_norm(x, g, b):
    xf = x.astype(jnp.float32)
    mu = jnp.mean(xf, axis=-1, keepdims=True)
    var = jnp.mean(jnp.square(xf - mu), axis=-1, keepdims=True)
    return ((xf - mu) * lax.rsqrt(var + LN_EPS) * g + b).astype(x.dtype)


def rms_norm(x, g):
    xf = x.astype(jnp.float32)
    return xf * lax.rsqrt(jnp.mean(xf * xf, axis=-1, keepdims=True) + RMS_EPS) * g


def split_heads(t, n):
    return t.reshape(t.shape[:-1] + (n, t.shape[-1] // n))


def gla_chunked(q, k, v, log_a):
    B, S, H, dk = q.shape
    dv = v.shape[-1]
    nc = S // CHUNK
    qc = q.reshape(B, nc, CHUNK, H, dk)
    kc = k.reshape(B, nc, CHUNK, H, dk)
    vc = v.reshape(B, nc, CHUNK, H, dv)
    b = jnp.cumsum(log_a.reshape(B, nc, CHUNK, H, dk), axis=2)
    b_last = b[:, :, -1]
    q_dec = qc * jnp.exp(b)
    k_inv = kc * jnp.exp(-b)
    k_end = kc * jnp.exp(b_last[:, :, None] - b)
    causal = jnp.tril(jnp.ones((CHUNK, CHUNK), dtype=bool))
    att = jnp.einsum('bnthk,bnshk->bnhts', q_dec, k_inv)
    att = jnp.where(causal, att, 0.0)
    o_intra = jnp.einsum('bnhts,bnshv->bnthv', att, vc)
    chunk_kv = jnp.einsum('bnshk,bnshv->bnhkv', k_end, vc)

    def step(state, inp):
        decay, kv = inp
        return decay[..., None] * state + kv, state

    state0 = jnp.zeros((B, H, dk, dv), dtype=chunk_kv.dtype)
    _, states = lax.scan(step, state0, (jnp.moveaxis(jnp.exp(b_last), 1, 0), jnp.moveaxis(chunk_kv, 1, 0)))
    states = jnp.moveaxis(states, 0, 1)
    o_inter = jnp.einsum('bnthk,bnhkv->bnthv', q_dec, states)
    return (o_intra + o_inter).reshape(B, S, H, dv)


def fox_attention(q, k, v, cum_logf):
    S = q.shape[2]
    scale = FOX_HD ** -0.5
    outs = []
    for blk in range(S // Q_BLOCK):
        lo = blk * Q_BLOCK
        hi = lo + Q_BLOCK
        logits = jnp.einsum('bhqd,bhkd->bhqk', q[:, :, lo:hi], k[:, :, :hi]).astype(jnp.float32) * scale
        logits = logits + cum_logf[:, :, lo:hi, None] - cum_logf[:, :, None, :hi]
        mask = (lo + jnp.arange(Q_BLOCK))[:, None] >= jnp.arange(hi)[None, :]
        p = jax.nn.softmax(jnp.where(mask, logits, -jnp.inf), axis=-1)
        outs.append(jnp.einsum('bhqk,bhkd->bhqd', p.astype(v.dtype), v[:, :, :hi]))
    return jnp.concatenate(outs, axis=2)


def mem_attention(q, mk, mv):
    logits = jnp.einsum('bshd,bmhd->bhsm', q, mk).astype(jnp.float32) * (MEM_HD ** -0.5)
    p = jax.nn.softmax(logits, axis=-1)
    return jnp.einsum('bhsm,bmhd->bshd', p.astype(mv.dtype), mv)


def grouped_moe(x, w_router, b_router, w_gate, w_up, w_down):
    B, S, D = x.shape
    xt = x.reshape(B * S, D)
    n = xt.shape[0]
    affinity = jax.nn.sigmoid((xt @ w_router).astype(jnp.float32))
    sel = (affinity + b_router.astype(jnp.float32)).reshape(n, N_GROUPS, EXPERTS_PER_GROUP)
    group_score = jnp.sum(lax.top_k(sel, TOP_K)[0], axis=-1)
    g_idx = jnp.argmax(group_score, axis=-1)
    in_group = jnp.take_along_axis(sel, jnp.broadcast_to(g_idx[:, None, None], (n, 1, EXPERTS_PER_GROUP)), axis=1)[:, 0]
    _, local = lax.top_k(in_group, TOP_K)
    e_idx = g_idx[:, None] * EXPERTS_PER_GROUP + local
    w = jnp.take_along_axis(affinity, e_idx, axis=-1)
    w = w / jnp.sum(w, axis=-1, keepdims=True)
    gates = jnp.sum(jax.nn.one_hot(e_idx, N_EXPERTS, dtype=jnp.float32) * w[..., None], axis=1).astype(x.dtype)
    out = jnp.zeros_like(xt)
    for e in range(N_EXPERTS):
        h = jax.nn.silu(xt @ w_gate[e]) * (xt @ w_up[e])
        out = out + gates[:, e:e + 1] * (h @ w_down[e])
    return out.reshape(B, S, D)


def setup_inputs(seed: int = 0) -> dict:
    key = jax.random.key(seed)
    ks = iter(jax.random.split(key, 32))

    def nrm(shape, scale):
        return jax.random.normal(next(ks), shape, jnp.float32) * scale

    d = D_MODEL
    x = nrm((BATCH, SEQ, d), 1.0)
    mem = nrm((BATCH, N_MEM, d), 1.0)
    col_a = jnp.concatenate([jnp.ones((2 * GLA_KDIM,)), jnp.full((SELF_W,), DN_BETA), jnp.ones((GLA_GATE_RANK + SELF_W + MEM_W,))])
    w_in_a = nrm((N_A_LAYERS, d, A_IN), d ** -0.5) * col_a
    w_gate_up_a = nrm((N_A_LAYERS, GLA_GATE_RANK, GLA_KDIM), GLA_GATE_RANK ** -0.5)
    b_gate_a = nrm((N_A_LAYERS, GLA_KDIM), 0.1)
    gla_norm_g = 1.0 + nrm((N_A_LAYERS, GLA_DV), 0.02)
    w_in_b = nrm((N_B_LAYERS, d, B_IN), d ** -0.5)
    q_norm_g = 1.0 + nrm((N_B_LAYERS, FOX_HD), 0.02)
    col_kv = jnp.concatenate([jnp.ones((SELF_W,)), jnp.full((SELF_W,), DN_BETA), jnp.ones((FOX_HEADS,))])
    w_kv_shared = nrm((d, KV_SHARED), d ** -0.5) * col_kv
    b_forget = 3.0 + nrm((FOX_HEADS,), 0.5)
    k_norm_g = 1.0 + nrm((FOX_HD,), 0.02)
    col_m = jnp.concatenate([jnp.ones((MEM_W,)), jnp.full((MEM_W,), DN_BETA)])
    w_mem_kv = nrm((DEPTH, d, 2 * MEM_W), d ** -0.5) * col_m
    w_out = nrm((DEPTH, d, d), d ** -0.5 * DN_BETA)
    ln_mix_g = 1.0 + nrm((DEPTH, d), 0.02)
    ln_mix_b = nrm((DEPTH, d), 0.02)
    ln_ffn_g = 1.0 + nrm((DEPTH, d), 0.02)
    ln_ffn_b = nrm((DEPTH, d), 0.02)
    w_router = nrm((d, N_EXPERTS), d ** -0.5)
    b_router = nrm((N_EXPERTS,), 0.01)
    w_exp_gate = nrm((DEPTH, N_EXPERTS, d, D_EXPERT), d ** -0.5)
    w_exp_up = nrm((DEPTH, N_EXPERTS, d, D_EXPERT), d ** -0.5 * DN_BETA)
    w_exp_down = nrm((DEPTH, N_EXPERTS, D_EXPERT, d), D_EXPERT ** -0.5 * DN_BETA)
    return {'x': x, 'mem': mem, 'w_in_a': w_in_a, 'w_gate_up_a': w_gate_up_a, 'b_gate_a': b_gate_a,
            'gla_norm_g': gla_norm_g, 'w_in_b': w_in_b, 'q_norm_g': q_norm_g, 'w_kv_shared': w_kv_shared,
            'b_forget': b_forget, 'k_norm_g': k_norm_g, 'w_mem_kv': w_mem_kv, 'w_out': w_out,
            'ln_mix_g': ln_mix_g, 'ln_mix_b': ln_mix_b, 'ln_ffn_g': ln_ffn_g, 'ln_ffn_b': ln_ffn_b,
            'w_router': w_router, 'b_router': b_router, 'w_exp_gate': w_exp_gate, 'w_exp_up': w_exp_up,
            'w_exp_down': w_exp_down}


def reference(x, mem, w_in_a, w_gate_up_a, b_gate_a, gla_norm_g, w_in_b, q_norm_g, w_kv_shared,
              b_forget, k_norm_g, w_mem_kv, w_out, ln_mix_g, ln_mix_b, ln_ffn_g, ln_ffn_b,
              w_router, b_router, w_exp_gate, w_exp_up, w_exp_down):
    B, S, D = x.shape
    shared_k = shared_v = shared_cum = None
    for layer in range(DEPTH):
        if layer < N_A_LAYERS:
            i = layer
            h = x @ w_in_a[i]
            q, k, v, g_lr, r, mq = jnp.split(h, A_SPLITS, axis=-1)
            log_a = jax.nn.log_sigmoid((g_lr @ w_gate_up_a[i] + b_gate_a[i]).astype(jnp.float32)) / GLA_TAU
            o = gla_chunked(split_heads(q, GLA_HEADS) * (GLA_DK ** -0.5), split_heads(k, GLA_HEADS),
                            split_heads(v, GLA_HEADS), split_heads(log_a, GLA_HEADS))
            o = rms_norm(o, gla_norm_g[i]).astype(x.dtype).reshape(B, S, SELF_W) * jax.nn.silu(r)
        else:
            if shared_k is None:
                kvf = x @ w_kv_shared
                ks_, vs_, f_logit = jnp.split(kvf, KV_SPLITS, axis=-1)
                shared_k = jnp.transpose(rms_norm(split_heads(ks_, FOX_HEADS), k_norm_g).astype(x.dtype), (0, 2, 1, 3))
                shared_v = jnp.transpose(split_heads(vs_, FOX_HEADS), (0, 2, 1, 3))
                log_f = jax.nn.log_sigmoid(f_logit.astype(jnp.float32) + b_forget)
                shared_cum = jnp.transpose(jnp.cumsum(log_f, axis=1), (0, 2, 1))
            j = layer - N_A_LAYERS
            h = x @ w_in_b[j]
            q, og, mq = jnp.split(h, B_SPLITS, axis=-1)
            q = jnp.transpose(rms_norm(split_heads(q, FOX_HEADS), q_norm_g[j]).astype(x.dtype), (0, 2, 1, 3))
            o = fox_attention(q, shared_k, shared_v, shared_cum)
            o = jnp.transpose(o, (0, 2, 1, 3)).reshape(B, S, SELF_W) * jax.nn.sigmoid(og)
        mkv = mem @ w_mem_kv[layer]
        mk, mv = jnp.split(mkv, 2, axis=-1)
        m_out = mem_attention(split_heads(mq, MEM_HEADS), split_heads(mk, MEM_HEADS), split_heads(mv, MEM_HEADS))
        y = jnp.concatenate([o, m_out.reshape(B, S, MEM_W)], axis=-1) @ w_out[layer]
        x = layer_norm(DN_ALPHA * x + y, ln_mix_g[layer], ln_mix_b[layer])
        f = grouped_moe(x, w_router, b_router, w_exp_gate[layer], w_exp_up[layer], w_exp_down[layer])
        x = layer_norm(DN_ALPHA * x + f, ln_ffn_g[layer], ln_ffn_b[layer])
    return x
```

```python
import functools
import math

import jax
import jax.numpy as jnp
from jax import lax
from jax.experimental import pallas as pl
from jax.experimental.pallas import tpu as pltpu

F32 = jnp.float32
BF16 = jnp.bfloat16

D_MODEL = 1024
DEPTH = 2
CHUNK = 64
N_MEM = 256
MEM_HEADS = 4
MEM_W = D_MODEL // 4
MEM_HD = MEM_W // MEM_HEADS
SELF_W = D_MODEL - MEM_W
GLA_HEADS = 4
GLA_KDIM = SELF_W // 2
GLA_DK = GLA_KDIM // GLA_HEADS
GLA_DV = SELF_W // GLA_HEADS
GLA_GATE_RANK = 16
GLA_TAU = 16.0
FOX_HD = 64
FOX_HEADS = SELF_W // FOX_HD
N_EXPERTS = 16
N_GROUPS = 4
EXPERTS_PER_GROUP = N_EXPERTS // N_GROUPS
D_EXPERT = D_MODEL // 2
DN_ALPHA = (2.0 * DEPTH) ** 0.25
LN_EPS = 1e-5
RMS_EPS = 1e-6

LANE = 128
GLA_DK_PAD = LANE
GLA_KPAD = GLA_HEADS * GLA_DK_PAD
GLA_VBLOCKS = SELF_W // LANE
GLA_HEAD_BLOCKS = ((0, 1), (1, 2), (3, 4), (4, 5))
FOX_PAIRS = FOX_HEADS // 2
LOG2E = math.log2(math.e)
NEG_BIG = -1e30
VMEM_LIMIT_BYTES = 48 * 1024 * 1024

NT_DIMS = (((1,), (1,)), ((), ()))
TN_DIMS = (((0,), (0,)), ((), ()))


def _cparams(*sem):
    return pltpu.CompilerParams(dimension_semantics=sem, vmem_limit_bytes=VMEM_LIMIT_BYTES)


def _dot(a, b):
    return jnp.dot(a, b, preferred_element_type=F32)


def _dot_nt(a, b):
    return lax.dot_general(a, b, NT_DIMS, preferred_element_type=F32)


def _dot_tn(a, b):
    return lax.dot_general(a, b, TN_DIMS, preferred_element_type=F32)


def _log_sigmoid(z):
    return -(jnp.maximum(-z, 0.0) + jnp.log1p(jnp.exp(-jnp.abs(z))))


def _sigmoid(z):
    return 1.0 / (1.0 + jnp.exp(-z))


def _split2(x):
    hi = x.astype(BF16)
    lo = (x - hi.astype(F32)).astype(BF16)
    return hi, lo


def _split3(x):
    hi = x.astype(BF16)
    r1 = x - hi.astype(F32)
    mid = r1.astype(BF16)
    lo = (r1 - mid.astype(F32)).astype(BF16)
    return hi, mid, lo


def _layer_norm(x, g, b):
    mu = jnp.mean(x, axis=-1, keepdims=True)
    xc = x - mu
    var = jnp.mean(xc * xc, axis=-1, keepdims=True)
    return xc * lax.rsqrt(var + LN_EPS) * g + b


def _proj_a_kernel(x_ref, w_ref, q_ref, k_ref, v_ref, r_ref, mq_ref, g_ref):
    xb = x_ref[...].astype(BF16)
    c = 0
    for ref in (q_ref, k_ref, v_ref, r_ref, mq_ref, g_ref):
        n = ref.shape[1]
        ref[...] = _dot(xb, w_ref[:, c:c + n]).astype(ref.dtype)
        c += n


def _proj_a(x2d, w, tm):
    m = x2d.shape[0]
    widths = (GLA_KPAD, GLA_KPAD, SELF_W, SELF_W, MEM_W, LANE)
    assert w.shape == (D_MODEL, sum(widths))
    return pl.pallas_call(
        _proj_a_kernel,
        grid=(m // tm,),
        in_specs=[pl.BlockSpec((tm, D_MODEL), lambda i: (i, 0)),
                  pl.BlockSpec(w.shape, lambda i: (0, 0))],
        out_specs=[pl.BlockSpec((tm, n), lambda i: (i, 0)) for n in widths],
        out_shape=[jax.ShapeDtypeStruct((m, n), BF16) for n in widths],
        compiler_params=_cparams("parallel"),
        name="proj_a",
    )(x2d, w)


def _gla_kernel(q_ref, k_ref, v_ref, r_ref, g_ref, wgu_ref, bg_ref, gn_ref, o_ref, st_ref, la_ref):
    tg = q_ref.shape[0]

    @pl.when(pl.program_id(1) == 0)
    def _():
        st_ref[...] = jnp.zeros_like(st_ref)

    z = _dot(g_ref[...], wgu_ref[...]) + bg_ref[...]
    la_ref[...] = _log_sigmoid(z) * (1.0 / GLA_TAU)

    row = lax.broadcasted_iota(jnp.int32, (CHUNK, CHUNK), 0)
    col = lax.broadcasted_iota(jnp.int32, (CHUNK, CHUNK), 1)
    causal = col <= row
    tri = jnp.where(causal, 1.0, 0.0).astype(BF16)
    lo_half = lax.broadcasted_iota(jnp.int32, (CHUNK, LANE), 1) < (LANE // 2)
    scale = GLA_DK ** -0.5

    def chunk(c, carry):
        r0 = pl.multiple_of(c * CHUNK, CHUNK)
        rows = pl.ds(r0, CHUNK)
        la_hi, la_lo = _split2(la_ref[rows, :])
        b = _dot(tri, la_hi) + _dot(tri, la_lo)
        bl = b[CHUNK - 1:CHUNK, :]
        qc = q_ref[rows, :].astype(F32) * scale
        kc = k_ref[rows, :].astype(F32)
        qd = (qc * jnp.exp(b)).astype(BF16)
        ki = (kc * jnp.exp(-b)).astype(BF16)
        ke = (kc * jnp.exp(bl - b)).astype(BF16)
        dec = jnp.exp(bl)
        unit = []
        for h in range(GLA_HEADS):
            sl = slice(GLA_DK_PAD * h, GLA_DK_PAD * (h + 1))
            qh, kih, keh = qd[:, sl], ki[:, sl], ke[:, sl]
            att = jnp.where(causal, _dot_nt(qh, kih), 0.0).astype(BF16)
            for t, blk in enumerate(GLA_HEAD_BLOCKS[h]):
                u = 2 * h + t
                vb = v_ref[rows, LANE * blk:LANE * (blk + 1)]
                st = st_ref[u]
                unit.append(_dot(att, vb) + _dot_nt(qh, st.astype(BF16)))
                st_ref[u] = st * dec[:, sl] + _dot_tn(vb, keh)
        o_blk = [unit[0], jnp.where(lo_half, unit[1], unit[2]), unit[3],
                 unit[4], jnp.where(lo_half, unit[5], unit[6]), unit[7]]
        sq = [o * o for o in o_blk]
        full = [jnp.sum(s, axis=-1, keepdims=True) for s in sq]
        lo1 = jnp.sum(jnp.where(lo_half, sq[1], 0.0), axis=-1, keepdims=True)
        hi1 = jnp.sum(jnp.where(lo_half, 0.0, sq[1]), axis=-1, keepdims=True)
        lo4 = jnp.sum(jnp.where(lo_half, sq[4], 0.0), axis=-1, keepdims=True)
        hi4 = jnp.sum(jnp.where(lo_half, 0.0, sq[4]), axis=-1, keepdims=True)
        ss = [full[0] + lo1, hi1 + full[2], full[3] + lo4, hi4 + full[5]]
        inv = [lax.rsqrt(s * (1.0 / GLA_DV) + RMS_EPS) for s in ss]
        inv_blk = [inv[0], jnp.where(lo_half, inv[0], inv[1]), inv[1],
                   inv[2], jnp.where(lo_half, inv[2], inv[3]), inv[3]]
        for blk in range(GLA_VBLOCKS):
            cs = slice(LANE * blk, LANE * (blk + 1))
            rg = r_ref[rows, cs].astype(F32)
            y = o_blk[blk] * inv_blk[blk] * gn_ref[:, cs]
            o_ref[rows, cs] = (y * (rg * _sigmoid(rg))).astype(o_ref.dtype)
        return carry

    lax.fori_loop(0, tg // CHUNK, chunk, 0)


def _gla(q, k, v, r, g, wgu, bg, gn, batch, seq, tg):
    m = batch * seq
    nt = seq // tg
    row_map = lambda b, i: (b * nt + i, 0)
    const = lambda b, i: (0, 0)
    return pl.pallas_call(
        _gla_kernel,
        grid=(batch, nt),
        in_specs=[pl.BlockSpec((tg, GLA_KPAD), row_map),
                  pl.BlockSpec((tg, GLA_KPAD), row_map),
                  pl.BlockSpec((tg, SELF_W), row_map),
                  pl.BlockSpec((tg, SELF_W), row_map),
                  pl.BlockSpec((tg, LANE), row_map),
                  pl.BlockSpec(wgu.shape, const),
                  pl.BlockSpec(bg.shape, const),
                  pl.BlockSpec(gn.shape, const)],
        out_specs=pl.BlockSpec((tg, SELF_W), row_map),
        out_shape=jax.ShapeDtypeStruct((m, SELF_W), BF16),
        scratch_shapes=[pltpu.VMEM((2 * GLA_HEADS, LANE, GLA_DK_PAD), F32),
                        pltpu.VMEM((tg, GLA_KPAD), F32)],
        compiler_params=_cparams("parallel", "arbitrary"),
        name="gla",
    )(q, k, v, r, g, wgu, bg, gn)


def _mem_kv_kernel(m_ref, w_ref, o_ref):
    o_ref[...] = _dot(m_ref[...].astype(BF16), w_ref[...]).astype(o_ref.dtype)


def _mem_kv(mem2d, w):
    m, n = mem2d.shape[0], w.shape[1]
    tm = N_MEM
    return pl.pallas_call(
        _mem_kv_kernel,
        grid=(m // tm,),
        in_specs=[pl.BlockSpec((tm, D_MODEL), lambda i: (i, 0)),
                  pl.BlockSpec(w.shape, lambda i: (0, 0))],
        out_specs=pl.BlockSpec((tm, n), lambda i: (i, 0)),
        out_shape=jax.ShapeDtypeStruct((m, n), BF16),
        compiler_params=_cparams("parallel"),
        name="mem_kv",
    )(mem2d, w)


def _top2_sum(a, b, c, d):
    p, q = jnp.maximum(a, b), jnp.minimum(a, b)
    r, s = jnp.maximum(c, d), jnp.minimum(c, d)
    return jnp.maximum(p, r) + jnp.maximum(jnp.minimum(p, r), jnp.maximum(q, s))


def _router_gates(aff, sel):
    score = [_top2_sum(*sel[EXPERTS_PER_GROUP * g:EXPERTS_PER_GROUP * (g + 1)]) for g in range(N_GROUPS)]
    picked = []
    for g in range(N_GROUPS):
        ok = None
        for i in range(N_GROUPS):
            if i == g:
                continue
            c = (score[g] > score[i]) if i < g else (score[g] >= score[i])
            ok = c if ok is None else (ok & c)
        picked.append(ok)
    w = []
    for g in range(N_GROUPS):
        for j in range(EXPERTS_PER_GROUP):
            ej = EXPERTS_PER_GROUP * g + j
            rank = None
            for i in range(EXPERTS_PER_GROUP):
                if i == j:
                    continue
                ei = EXPERTS_PER_GROUP * g + i
                beats = (sel[ei] >= sel[ej]) if i < j else (sel[ei] > sel[ej])
                beats = jnp.where(beats, 1.0, 0.0)
                rank = beats if rank is None else rank + beats
            chosen = picked[g] & (rank < 1.5)
            w.append(jnp.where(chosen, aff[ej], 0.0))
    denom = functools.reduce(lambda a, b: a + b, w)
    return [wi / denom for wi in w]


def _out_kernel(o_ref, mq_ref, mk_ref, mv_ref, x_ref, wo_ref, lng_ref, lnb_ref,
                wrh_ref, wrl_ref, br_ref, x1_ref, gt_ref):
    tm = o_ref.shape[0]
    mq = mq_ref[...]
    mk = mk_ref[...]
    mv = mv_ref[...]
    lane = lax.broadcasted_iota(jnp.int32, (N_MEM, MEM_W), 1)
    m_out = jnp.zeros((tm, MEM_W), F32)
    for h in range(MEM_HEADS):
        in_head = (lane >= MEM_HD * h) & (lane < MEM_HD * (h + 1))
        s = _dot_nt(mq, jnp.where(in_head, mk, jnp.zeros_like(mk))) * (MEM_HD ** -0.5)
        p = jnp.exp(s - jnp.max(s, axis=-1, keepdims=True))
        l = jnp.sum(p, axis=-1, keepdims=True)
        m_out = m_out + _dot(p.astype(BF16), jnp.where(in_head, mv, jnp.zeros_like(mv))) / l
    y = _dot(o_ref[...], wo_ref[:SELF_W, :]) + _dot(m_out.astype(BF16), wo_ref[SELF_W:, :])
    x1 = _layer_norm(DN_ALPHA * x_ref[...] + y, lng_ref[...], lnb_ref[...])
    x1_ref[...] = x1

    xh, xl = _split2(x1)
    wrh = wrh_ref[...]
    logits = _dot_nt(wrh, xh) + _dot_nt(wrh, xl) + _dot_nt(wrl_ref[...], xh)
    aff = _sigmoid(logits)
    sel = aff + br_ref[...]
    gates = _router_gates([aff[e:e + 1, :] for e in range(N_EXPERTS)],
                          [sel[e:e + 1, :] for e in range(N_EXPERTS)])
    sub = lax.broadcasted_iota(jnp.int32, (N_EXPERTS, tm), 0)
    gt = jnp.zeros((N_EXPERTS, tm), F32)
    for e in range(N_EXPERTS):
        gt = jnp.where(sub == e, gates[e], gt)
    gt_ref[...] = gt


def _out_block(o, mq, mkv, layer, x2d, wo, lng, lnb, wrh, wrl, br, seq, tm):
    m = x2d.shape[0]
    nt = seq // tm
    row_map = lambda i: (i, 0)
    const = lambda i: (0, 0)
    return pl.pallas_call(
        _out_kernel,
        grid=(m // tm,),
        in_specs=[pl.BlockSpec((tm, SELF_W), row_map),
                  pl.BlockSpec((tm, MEM_W), row_map),
                  pl.BlockSpec((N_MEM, MEM_W), lambda i: (i // nt, 2 * layer)),
                  pl.BlockSpec((N_MEM, MEM_W), lambda i: (i // nt, 2 * layer + 1)),
                  pl.BlockSpec((tm, D_MODEL), row_map),
                  pl.BlockSpec(wo.shape, const),
                  pl.BlockSpec(lng.shape, const),
                  pl.BlockSpec(lnb.shape, const),
                  pl.BlockSpec(wrh.shape, const),
                  pl.BlockSpec(wrl.shape, const),
                  pl.BlockSpec(br.shape, const)],
        out_specs=[pl.BlockSpec((tm, D_MODEL), row_map),
                   pl.BlockSpec((N_EXPERTS, tm), lambda i: (0, i))],
        out_shape=[jax.ShapeDtypeStruct((m, D_MODEL), F32),
                   jax.ShapeDtypeStruct((N_EXPERTS, m), F32)],
        compiler_params=_cparams("parallel"),
        name="out_ln_router",
    )(o, mq, mkv, mkv, x2d, wo, lng, lnb, wrh, wrl, br)


def _moe_kernel(x_ref, gt_ref, wg_ref, wu_ref, wd_ref, lng_ref, lnb_ref, o_ref, acc_ref, xb_ref, gc_ref):
    e = pl.program_id(1)

    @pl.when(e == 0)
    def _():
        acc_ref[...] = jnp.zeros_like(acc_ref)
        xb_ref[...] = x_ref[...].astype(BF16)
        gc_ref[...] = gt_ref[...].T

    xb = xb_ref[...]
    hg = _dot(xb, wg_ref[0])
    hu = _dot(xb, wu_ref[0])
    h = (hg * _sigmoid(hg) * hu).astype(BF16)
    lane = lax.broadcasted_iota(jnp.int32, gc_ref.shape, 1)
    ge = jnp.sum(jnp.where(lane == e, gc_ref[...], 0.0), axis=1, keepdims=True)
    acc_ref[...] += ge * _dot(h, wd_ref[0])

    @pl.when(e == N_EXPERTS - 1)
    def _():
        o_ref[...] = _layer_norm(DN_ALPHA * x_ref[...] + acc_ref[...], lng_ref[...], lnb_ref[...])


def _moe(x1, gt, wg, wu, wd, lng, lnb, tm):
    m = x1.shape[0]
    return pl.pallas_call(
        _moe_kernel,
        grid=(m // tm, N_EXPERTS),
        in_specs=[pl.BlockSpec((tm, D_MODEL), lambda i, e: (i, 0)),
                  pl.BlockSpec((N_EXPERTS, tm), lambda i, e: (0, i)),
                  pl.BlockSpec((1, D_MODEL, D_EXPERT), lambda i, e: (e, 0, 0)),
                  pl.BlockSpec((1, D_MODEL, D_EXPERT), lambda i, e: (e, 0, 0)),
                  pl.BlockSpec((1, D_EXPERT, D_MODEL), lambda i, e: (e, 0, 0)),
                  pl.BlockSpec(lng.shape, lambda i, e: (0, 0)),
                  pl.BlockSpec(lnb.shape, lambda i, e: (0, 0))],
        out_specs=pl.BlockSpec((tm, D_MODEL), lambda i, e: (i, 0)),
        out_shape=jax.ShapeDtypeStruct((m, D_MODEL), F32),
        scratch_shapes=[pltpu.VMEM((tm, D_MODEL), F32),
                        pltpu.VMEM((tm, D_MODEL), BF16),
                        pltpu.VMEM((tm, N_EXPERTS), F32)],
        compiler_params=_cparams("parallel", "arbitrary"),
        name="moe",
    )(x1, gt, wg, wu, wd, lng, lnb)


def _head_tile(res, h):
    blk = res[:, LANE * (h // 2):LANE * (h // 2 + 1)]
    if h % 2:
        blk = pltpu.roll(blk, FOX_HD, 1)
    return blk


def _proj_b_kernel(x_ref, wq_ref, wog_ref, wmq_ref, wk_ref, wv_ref, wf_ref, bf_ref, qg_ref, kg_ref,
                   qa_ref, ka_ref, v_ref, og_ref, mq_ref, carry_ref):
    tm = x_ref.shape[0]

    @pl.when(pl.program_id(1) == 0)
    def _():
        carry_ref[...] = jnp.zeros_like(carry_ref)

    xb = x_ref[...].astype(BF16)
    og_ref[...] = _dot(xb, wog_ref[...]).astype(og_ref.dtype)
    mq_ref[...] = _dot(xb, wmq_ref[...]).astype(mq_ref.dtype)
    v_ref[...] = _dot(xb, wv_ref[...]).astype(v_ref.dtype)
    rq = _dot(xb, wq_ref[...])
    rk = _dot(xb, wk_ref[...])

    log_f = _log_sigmoid(_dot(xb, wf_ref[...]) + bf_ref[...])
    row = lax.broadcasted_iota(jnp.int32, (tm, tm), 0)
    col = lax.broadcasted_iota(jnp.int32, (tm, tm), 1)
    tri = jnp.where(col <= row, 1.0, 0.0).astype(BF16)
    f_hi, f_mid, f_lo = _split3(log_f)
    cum = carry_ref[...] + (_dot(tri, f_hi) + _dot(tri, f_mid) + _dot(tri, f_lo))
    carry_ref[...] = cum[tm - 1:tm, :]
    cum = cum * LOG2E

    lane = lax.broadcasted_iota(jnp.int32, (tm, LANE), 1)
    is_q = lane < FOX_HD
    ones_q = (lane >= FOX_HD + 3) & (lane < FOX_HD + 6)
    ones_k = (lane >= FOX_HD) & (lane < FOX_HD + 3)
    q_scale = (FOX_HD ** -0.5) * LOG2E
    for h in range(FOX_HEADS):
        c_hi, c_mid, c_lo = _split3(cum[:, h:h + 1])
        c_hi, c_mid, c_lo = c_hi.astype(F32), c_mid.astype(F32), c_lo.astype(F32)

        qt = _head_tile(rq, h)
        ssq = jnp.sum(jnp.where(is_q, qt * qt, 0.0), axis=-1, keepdims=True)
        qn = qt * lax.rsqrt(ssq * (1.0 / FOX_HD) + RMS_EPS) * qg_ref[...] * q_scale
        aug = jnp.where(lane == FOX_HD, c_hi, jnp.where(lane == FOX_HD + 1, c_mid,
              jnp.where(lane == FOX_HD + 2, c_lo, jnp.where(ones_q, 1.0, 0.0))))
        qa_ref[0, h] = jnp.where(is_q, qn, aug).astype(qa_ref.dtype)

        kt = _head_tile(rk, h)
        ssk = jnp.sum(jnp.where(is_q, kt * kt, 0.0), axis=-1, keepdims=True)
        kn = kt * lax.rsqrt(ssk * (1.0 / FOX_HD) + RMS_EPS) * kg_ref[...]
        aug = jnp.where(lane == FOX_HD + 3, -c_hi, jnp.where(lane == FOX_HD + 4, -c_mid,
              jnp.where(lane == FOX_HD + 5, -c_lo, jnp.where(ones_k, 1.0, 0.0))))
        ka_ref[0, h] = jnp.where(is_q, kn, aug).astype(ka_ref.dtype)


def _proj_b(x2d, wq, wog, wmq, wk, wv, wf, bf, qg, kg, batch, seq, tm):
    m = batch * seq
    nt = seq // tm
    row_map = lambda b, i: (b * nt + i, 0)
    const = lambda b, i: (0, 0)
    head_map = lambda b, i: (b, 0, i, 0)
    return pl.pallas_call(
        _proj_b_kernel,
        grid=(batch, nt),
        in_specs=[pl.BlockSpec((tm, D_MODEL), row_map)]
                 + [pl.BlockSpec(a.shape, const) for a in (wq, wog, wmq, wk, wv, wf, bf, qg, kg)],
        out_specs=[pl.BlockSpec((1, FOX_HEADS, tm, LANE), head_map),
                   pl.BlockSpec((1, FOX_HEADS, tm, LANE), head_map),
                   pl.BlockSpec((tm, SELF_W), row_map),
                   pl.BlockSpec((tm, SELF_W), row_map),
                   pl.BlockSpec((tm, MEM_W), row_map)],
        out_shape=[jax.ShapeDtypeStruct((batch, FOX_HEADS, seq, LANE), BF16),
                   jax.ShapeDtypeStruct((batch, FOX_HEADS, seq, LANE), BF16),
                   jax.ShapeDtypeStruct((m, SELF_W), BF16),
                   jax.ShapeDtypeStruct((m, SELF_W), BF16),
                   jax.ShapeDtypeStruct((m, MEM_W), BF16)],
        scratch_shapes=[pltpu.VMEM((1, LANE), F32)],
        compiler_params=_cparams("parallel", "arbitrary"),
        name="proj_b",
    )(x2d, wq, wog, wmq, wk, wv, wf, bf, qg, kg)


def _fox_kernel(qa_ref, ka_ref, v_ref, og_ref, o_ref, m_ref, l_ref, acc_ref):
    tq = qa_ref.shape[2]
    qi = pl.program_id(2)
    row = lax.broadcasted_iota(jnp.int32, (tq, tq), 0)
    col = lax.broadcasted_iota(jnp.int32, (tq, tq), 1)
    causal = col <= row

    for hh in range(2):
        q = qa_ref[0, hh]
        m_ref[hh] = jnp.full((tq, 1), NEG_BIG, F32)
        l_ref[hh] = jnp.zeros((tq, 1), F32)
        acc_ref[hh] = jnp.zeros((tq, LANE), F32)

        def step(j, masked, hh=hh, q=q):
            rows = pl.ds(pl.multiple_of(j * tq, tq), tq)
            s = _dot_nt(q, ka_ref[0, hh, rows, :])
            if masked:
                s = jnp.where(causal, s, NEG_BIG)
            m_old = m_ref[hh]
            m_new = jnp.maximum(m_old, jnp.max(s, axis=-1, keepdims=True))
            alpha = jnp.exp2(m_old - m_new)
            p = jnp.exp2(s - m_new)
            l_ref[hh] = alpha * l_ref[hh] + jnp.sum(p, axis=-1, keepdims=True)
            acc_ref[hh] = alpha * acc_ref[hh] + _dot(p.astype(BF16), v_ref[0, rows, :])
            m_ref[hh] = m_new

        def body(j, carry):
            step(j, False)
            return carry

        lax.fori_loop(0, qi, body, 0)
        step(qi, True)

    lo_half = lax.broadcasted_iota(jnp.int32, (tq, LANE), 1) < FOX_HD
    o = jnp.where(lo_half, acc_ref[0] / l_ref[0], acc_ref[1] / l_ref[1])
    og = og_ref[...].astype(F32)
    o_ref[...] = (o * _sigmoid(og)).astype(o_ref.dtype)


def _fox(qa, ka, v3, og, batch, seq, tq):
    nq = seq // tq
    m = batch * seq
    return pl.pallas_call(
        _fox_kernel,
        grid=(batch, FOX_PAIRS, nq),
        in_specs=[pl.BlockSpec((1, 2, tq, LANE), lambda b, p, i: (b, p, i, 0)),
                  pl.BlockSpec((1, 2, seq, LANE), lambda b, p, i: (b, p, 0, 0)),
                  pl.BlockSpec((1, seq, LANE), lambda b, p, i: (b, 0, p)),
                  pl.BlockSpec((tq, LANE), lambda b, p, i: (b * nq + i, p))],
        out_specs=pl.BlockSpec((tq, LANE), lambda b, p, i: (b * nq + i, p)),
        out_shape=jax.ShapeDtypeStruct((m, SELF_W), BF16),
        scratch_shapes=[pltpu.VMEM((2, tq, 1), F32),
                        pltpu.VMEM((2, tq, 1), F32),
                        pltpu.VMEM((2, tq, LANE), F32)],
        compiler_params=_cparams("parallel", "parallel", "arbitrary"),
        name="fox",
    )(qa, ka, v3, og)


def _pad_heads(w, n_heads, d, d_pad):
    lead = w.shape[:-1]
    w = w.reshape(lead + (n_heads, d))
    w = jnp.pad(w, [(0, 0)] * len(lead) + [(0, 0), (0, d_pad - d)])
    return w.reshape(lead + (n_heads * d_pad,))


def _pad_last(w, n):
    return jnp.pad(w, [(0, 0)] * (w.ndim - 1) + [(0, n - w.shape[-1])])


def kernel(x, mem, w_in_a, w_gate_up_a, b_gate_a, gla_norm_g, w_in_b, q_norm_g, w_kv_shared, b_forget, k_norm_g, w_mem_kv, w_out, ln_mix_g, ln_mix_b, ln_ffn_g, ln_ffn_b, w_router, b_router, w_exp_gate, w_exp_up, w_exp_down):
    batch, seq, d = x.shape
    m = batch * seq
    tm = 512
    x2d = x.reshape(m, d)

    wa = w_in_a[0]
    s0, s1, s2, s3, s4 = (GLA_KDIM, 2 * GLA_KDIM, 2 * GLA_KDIM + SELF_W,
                          2 * GLA_KDIM + SELF_W + GLA_GATE_RANK, 2 * GLA_KDIM + 2 * SELF_W + GLA_GATE_RANK)
    w_a = jnp.concatenate([
        _pad_heads(wa[:, :s0], GLA_HEADS, GLA_DK, GLA_DK_PAD),
        _pad_heads(wa[:, s0:s1], GLA_HEADS, GLA_DK, GLA_DK_PAD),
        wa[:, s1:s2], wa[:, s3:s4], wa[:, s4:], _pad_last(wa[:, s2:s3], LANE)], axis=1).astype(BF16)
    wgu = _pad_heads(w_gate_up_a[0], GLA_HEADS, GLA_DK, GLA_DK_PAD)
    wgu = jnp.pad(wgu, ((0, LANE - GLA_GATE_RANK), (0, 0))).astype(BF16)
    bg = _pad_heads(b_gate_a[0], GLA_HEADS, GLA_DK, GLA_DK_PAD).reshape(1, GLA_KPAD)
    gn = jnp.tile(gla_norm_g[0], GLA_HEADS).reshape(1, SELF_W)

    wb = w_in_b[0]
    wq, wog, wmq = (wb[:, :SELF_W].astype(BF16), wb[:, SELF_W:2 * SELF_W].astype(BF16),
                    wb[:, 2 * SELF_W:].astype(BF16))
    wk, wv = w_kv_shared[:, :SELF_W].astype(BF16), w_kv_shared[:, SELF_W:2 * SELF_W].astype(BF16)
    wf = _pad_last(w_kv_shared[:, 2 * SELF_W:], LANE).astype(BF16)
    bf = _pad_last(b_forget, LANE).reshape(1, LANE)
    qg = _pad_last(q_norm_g[0], LANE).reshape(1, LANE)
    kg = _pad_last(k_norm_g, LANE).reshape(1, LANE)

    w_mkv = jnp.concatenate([w_mem_kv[l] for l in range(DEPTH)], axis=1).astype(BF16)
    wo = w_out.astype(BF16)
    wr_t = w_router.T
    wrh = wr_t.astype(BF16)
    wrl = (wr_t - wrh.astype(F32)).astype(BF16)
    br = b_router.reshape(N_EXPERTS, 1)
    weg, weu, wed = w_exp_gate.astype(BF16), w_exp_up.astype(BF16), w_exp_down.astype(BF16)
    row = lambda a: a.reshape(1, d)

    mkv = _mem_kv(mem.reshape(batch * N_MEM, d), w_mkv)

    def tail(o, mq, xin, layer):
        x1, gt = _out_block(o, mq, mkv, layer, xin, wo[layer], row(ln_mix_g[layer]), row(ln_mix_b[layer]),
                            wrh, wrl, br, seq, tm)
        return _moe(x1, gt, weg[layer], weu[layer], wed[layer], row(ln_ffn_g[layer]), row(ln_ffn_b[layer]), 1024)

    q, k, v, r, mq, g = _proj_a(x2d, w_a, tm)
    o = _gla(q, k, v, r, g, wgu, bg, gn, batch, seq, tm)
    xa = tail(o, mq, x2d, 0)

    qa, ka, vb, og, mqb = _proj_b(xa, wq, wog, wmq, wk, wv, wf, bf, qg, kg, batch, seq, tm)
    ob = _fox(qa, ka, vb.reshape(batch, seq, SELF_W), og, batch, seq, tm)
    xb = tail(ob, mqb, xa, 1)
    return xb.reshape(batch, seq, d)
```

```python
import functools
import math

import jax
import jax.numpy as jnp
import numpy as np
from jax import lax
from jax.experimental import pallas as pl
from jax.experimental.pallas import tpu as pltpu

F32 = jnp.float32
BF16 = jnp.bfloat16

D_MODEL = 1024
DEPTH = 2
CHUNK = 64
N_MEM = 256
MEM_HEADS = 4
MEM_W = D_MODEL // 4
MEM_HD = MEM_W // MEM_HEADS
SELF_W = D_MODEL - MEM_W
GLA_HEADS = 4
GLA_KDIM = SELF_W // 2
GLA_DK = GLA_KDIM // GLA_HEADS
GLA_DV = SELF_W // GLA_HEADS
GLA_GATE_RANK = 16
GLA_TAU = 16.0
FOX_HD = 64
FOX_HEADS = SELF_W // FOX_HD
N_EXPERTS = 16
N_GROUPS = 4
EXPERTS_PER_GROUP = N_EXPERTS // N_GROUPS
D_EXPERT = D_MODEL // 2
DN_ALPHA = (2.0 * DEPTH) ** 0.25
LN_EPS = 1e-5
RMS_EPS = 1e-6

LANE = 128
GLA_DK_PAD = LANE
GLA_KPAD = GLA_HEADS * GLA_DK_PAD
GLA_VBLOCKS = SELF_W // LANE
GLA_HEAD_BLOCKS = ((0, 1), (1, 2), (3, 4), (4, 5))
FOX_PAIRS = FOX_HEADS // 2
LOG2E = math.log2(math.e)
NEG_BIG = -1e30
VMEM_LIMIT_BYTES = 48 * 1024 * 1024

NT_DIMS = (((1,), (1,)), ((), ()))
TN_DIMS = (((0,), (0,)), ((), ()))


def _cparams(*sem):
    return pltpu.CompilerParams(dimension_semantics=sem, vmem_limit_bytes=VMEM_LIMIT_BYTES)


def _dot(a, b):
    return jnp.dot(a, b, preferred_element_type=F32)


def _dot_nt(a, b):
    return lax.dot_general(a, b, NT_DIMS, preferred_element_type=F32)


def _dot_tn(a, b):
    return lax.dot_general(a, b, TN_DIMS, preferred_element_type=F32)


def _log_sigmoid(z):
    return -(jnp.maximum(-z, 0.0) + jnp.log1p(jnp.exp(-jnp.abs(z))))


def _sigmoid(z):
    return 1.0 / (1.0 + jnp.exp(-z))


def _split2(x):
    hi = x.astype(BF16)
    lo = (x - hi.astype(F32)).astype(BF16)
    return hi, lo


def _split3(x):
    hi = x.astype(BF16)
    r1 = x - hi.astype(F32)
    mid = r1.astype(BF16)
    lo = (r1 - mid.astype(F32)).astype(BF16)
    return hi, mid, lo


def _layer_norm(x, g, b):
    mu = jnp.mean(x, axis=-1, keepdims=True)
    xc = x - mu
    var = jnp.mean(xc * xc, axis=-1, keepdims=True)
    return xc * lax.rsqrt(var + LN_EPS) * g + b


def _proj_a_kernel(x_ref, w_ref, q_ref, k_ref, v_ref, r_ref, mq_ref, g_ref):
    xb = x_ref[...].astype(BF16)
    c = 0
    for ref in (q_ref, k_ref, v_ref, r_ref, mq_ref, g_ref):
        n = ref.shape[1]
        ref[...] = _dot(xb, w_ref[:, c:c + n]).astype(ref.dtype)
        c += n


def _proj_a(x2d, w, tm):
    m = x2d.shape[0]
    widths = (GLA_KPAD, GLA_KPAD, SELF_W, SELF_W, MEM_W, LANE)
    assert w.shape == (D_MODEL, sum(widths))
    return pl.pallas_call(
        _proj_a_kernel,
        grid=(m // tm,),
        in_specs=[pl.BlockSpec((tm, D_MODEL), lambda i: (i, 0)),
                  pl.BlockSpec(w.shape, lambda i: (0, 0))],
        out_specs=[pl.BlockSpec((tm, n), lambda i: (i, 0)) for n in widths],
        out_shape=[jax.ShapeDtypeStruct((m, n), BF16) for n in widths],
        compiler_params=_cparams("parallel"),
        name="proj_a",
    )(x2d, w)


def _gla_kernel(q_ref, k_ref, v_ref, r_ref, g_ref, wgu_ref, bg_ref, gn_ref, o_ref, st_ref, la_ref):
    tg = q_ref.shape[0]

    @pl.when(pl.program_id(1) == 0)
    def _():
        st_ref[...] = jnp.zeros_like(st_ref)

    z = _dot(g_ref[...], wgu_ref[...]) + bg_ref[...]
    la_ref[...] = _log_sigmoid(z) * (1.0 / GLA_TAU)

    row = lax.broadcasted_iota(jnp.int32, (CHUNK, CHUNK), 0)
    col = lax.broadcasted_iota(jnp.int32, (CHUNK, CHUNK), 1)
    causal = col <= row
    tri = jnp.where(causal, 1.0, 0.0).astype(BF16)
    lo_half = lax.broadcasted_iota(jnp.int32, (CHUNK, LANE), 1) < (LANE // 2)
    scale = GLA_DK ** -0.5

    def chunk(c, carry):
        r0 = pl.multiple_of(c * CHUNK, CHUNK)
        rows = pl.ds(r0, CHUNK)
        la_hi, la_lo = _split2(la_ref[rows, :])
        b = _dot(tri, la_hi) + _dot(tri, la_lo)
        bl = b[CHUNK - 1:CHUNK, :]
        qc = q_ref[rows, :].astype(F32) * scale
        kc = k_ref[rows, :].astype(F32)
        qd = (qc * jnp.exp(b)).astype(BF16)
        ki = (kc * jnp.exp(-b)).astype(BF16)
        ke = (kc * jnp.exp(bl - b)).astype(BF16)
        dec = jnp.exp(bl)
        unit = []
        for h in range(GLA_HEADS):
            sl = slice(GLA_DK_PAD * h, GLA_DK_PAD * (h + 1))
            qh, kih, keh = qd[:, sl], ki[:, sl], ke[:, sl]
            att = jnp.where(causal, _dot_nt(qh, kih), 0.0).astype(BF16)
            for t, blk in enumerate(GLA_HEAD_BLOCKS[h]):
                u = 2 * h + t
                vb = v_ref[rows, LANE * blk:LANE * (blk + 1)]
                st = st_ref[u]
                unit.append(_dot(att, vb) + _dot_nt(qh, st.astype(BF16)))
                st_ref[u] = st * dec[:, sl] + _dot_tn(vb, keh)
        o_blk = [unit[0], jnp.where(lo_half, unit[1], unit[2]), unit[3],
                 unit[4], jnp.where(lo_half, unit[5], unit[6]), unit[7]]
        sq = [o * o for o in o_blk]
        full = [jnp.sum(s, axis=-1, keepdims=True) for s in sq]
        lo1 = jnp.sum(jnp.where(lo_half, sq[1], 0.0), axis=-1, keepdims=True)
        hi1 = jnp.sum(jnp.where(lo_half, 0.0, sq[1]), axis=-1, keepdims=True)
        lo4 = jnp.sum(jnp.where(lo_half, sq[4], 0.0), axis=-1, keepdims=True)
        hi4 = jnp.sum(jnp.where(lo_half, 0.0, sq[4]), axis=-1, keepdims=True)
        ss = [full[0] + lo1, hi1 + full[2], full[3] + lo4, hi4 + full[5]]
        inv = [lax.rsqrt(s * (1.0 / GLA_DV) + RMS_EPS) for s in ss]
        inv_blk = [inv[0], jnp.where(lo_half, inv[0], inv[1]), inv[1],
                   inv[2], jnp.where(lo_half, inv[2], inv[3]), inv[3]]
        for blk in range(GLA_VBLOCKS):
            cs = slice(LANE * blk, LANE * (blk + 1))
            rg = r_ref[rows, cs].astype(F32)
            y = o_blk[blk] * inv_blk[blk] * gn_ref[:, cs]
            o_ref[rows, cs] = (y * (rg * _sigmoid(rg))).astype(o_ref.dtype)
        return carry

    lax.fori_loop(0, tg // CHUNK, chunk, 0)


def _gla(q, k, v, r, g, wgu, bg, gn, batch, seq, tg):
    m = batch * seq
    nt = seq // tg
    row_map = lambda b, i: (b * nt + i, 0)
    const = lambda b, i: (0, 0)
    return pl.pallas_call(
        _gla_kernel,
        grid=(batch, nt),
        in_specs=[pl.BlockSpec((tg, GLA_KPAD), row_map),
                  pl.BlockSpec((tg, GLA_KPAD), row_map),
                  pl.BlockSpec((tg, SELF_W), row_map),
                  pl.BlockSpec((tg, SELF_W), row_map),
                  pl.BlockSpec((tg, LANE), row_map),
                  pl.BlockSpec(wgu.shape, const),
                  pl.BlockSpec(bg.shape, const),
                  pl.BlockSpec(gn.shape, const)],
        out_specs=pl.BlockSpec((tg, SELF_W), row_map),
        out_shape=jax.ShapeDtypeStruct((m, SELF_W), BF16),
        scratch_shapes=[pltpu.VMEM((2 * GLA_HEADS, LANE, GLA_DK_PAD), F32),
                        pltpu.VMEM((tg, GLA_KPAD), F32)],
        compiler_params=_cparams("parallel", "arbitrary"),
        name="gla",
    )(q, k, v, r, g, wgu, bg, gn)


def _mem_kv_kernel(m_ref, w_ref, o_ref):
    o_ref[...] = _dot(m_ref[...].astype(BF16), w_ref[...]).astype(o_ref.dtype)


def _mem_kv(mem2d, w):
    m, n = mem2d.shape[0], w.shape[1]
    tm = N_MEM
    return pl.pallas_call(
        _mem_kv_kernel,
        grid=(m // tm,),
        in_specs=[pl.BlockSpec((tm, D_MODEL), lambda i: (i, 0)),
                  pl.BlockSpec(w.shape, lambda i: (0, 0))],
        out_specs=pl.BlockSpec((tm, n), lambda i: (i, 0)),
        out_shape=jax.ShapeDtypeStruct((m, n), BF16),
        compiler_params=_cparams("parallel"),
        name="mem_kv",
    )(mem2d, w)


def _top2_sum(a, b, c, d):
    p, q = jnp.maximum(a, b), jnp.minimum(a, b)
    r, s = jnp.maximum(c, d), jnp.minimum(c, d)
    return jnp.maximum(p, r) + jnp.maximum(jnp.minimum(p, r), jnp.maximum(q, s))


def _router_gates(aff, sel):
    score = [_top2_sum(*sel[EXPERTS_PER_GROUP * g:EXPERTS_PER_GROUP * (g + 1)]) for g in range(N_GROUPS)]
    picked = []
    for g in range(N_GROUPS):
        ok = None
        for i in range(N_GROUPS):
            if i == g:
                continue
            c = (score[g] > score[i]) if i < g else (score[g] >= score[i])
            ok = c if ok is None else (ok & c)
        picked.append(ok)
    w = []
    for g in range(N_GROUPS):
        for j in range(EXPERTS_PER_GROUP):
            ej = EXPERTS_PER_GROUP * g + j
            rank = None
            for i in range(EXPERTS_PER_GROUP):
                if i == j:
                    continue
                ei = EXPERTS_PER_GROUP * g + i
                beats = (sel[ei] >= sel[ej]) if i < j else (sel[ei] > sel[ej])
                beats = jnp.where(beats, 1.0, 0.0)
                rank = beats if rank is None else rank + beats
            chosen = picked[g] & (rank < 1.5)
            w.append(jnp.where(chosen, aff[ej], 0.0))
    denom = functools.reduce(lambda a, b: a + b, w)
    return [wi / denom for wi in w]


def _out_kernel(o_ref, mq_ref, mk_ref, mv_ref, x_ref, wo_ref, lng_ref, lnb_ref,
                wrh_ref, wrl_ref, br_ref, x1_ref, gt_ref):
    tm = o_ref.shape[0]
    mq = mq_ref[...]
    mk = mk_ref[...]
    mv = mv_ref[...]
    lane = lax.broadcasted_iota(jnp.int32, (N_MEM, MEM_W), 1)
    m_out = jnp.zeros((tm, MEM_W), F32)
    for h in range(MEM_HEADS):
        in_head = (lane >= MEM_HD * h) & (lane < MEM_HD * (h + 1))
        s = _dot_nt(mq, jnp.where(in_head, mk, jnp.zeros_like(mk))) * (MEM_HD ** -0.5)
        p = jnp.exp(s - jnp.max(s, axis=-1, keepdims=True))
        l = jnp.sum(p, axis=-1, keepdims=True)
        m_out = m_out + _dot(p.astype(BF16), jnp.where(in_head, mv, jnp.zeros_like(mv))) / l
    y = _dot(o_ref[...], wo_ref[:SELF_W, :]) + _dot(m_out.astype(BF16), wo_ref[SELF_W:, :])
    x1 = _layer_norm(DN_ALPHA * x_ref[...] + y, lng_ref[...], lnb_ref[...])
    x1_ref[...] = x1

    xh, xl = _split2(x1)
    wrh = wrh_ref[...]
    logits = _dot_nt(wrh, xh) + _dot_nt(wrh, xl) + _dot_nt(wrl_ref[...], xh)
    aff = _sigmoid(logits)
    sel = aff + br_ref[...]
    gates = _router_gates([aff[e:e + 1, :] for e in range(N_EXPERTS)],
                          [sel[e:e + 1, :] for e in range(N_EXPERTS)])
    sub = lax.broadcasted_iota(jnp.int32, (N_EXPERTS, tm), 0)
    gt = jnp.zeros((N_EXPERTS, tm), F32)
    for e in range(N_EXPERTS):
        gt = jnp.where(sub == e, gates[e], gt)
    gt_ref[...] = gt


def _out_block(o, mq, mkv, layer, x2d, wo, lng, lnb, wrh, wrl, br, seq, tm):
    m = x2d.shape[0]
    nt = seq // tm
    row_map = lambda i: (i, 0)
    const = lambda i: (0, 0)
    return pl.pallas_call(
        _out_kernel,
        grid=(m // tm,),
        in_specs=[pl.BlockSpec((tm, SELF_W), row_map),
                  pl.BlockSpec((tm, MEM_W), row_map),
                  pl.BlockSpec((N_MEM, MEM_W), lambda i: (i // nt, 2 * layer)),
                  pl.BlockSpec((N_MEM, MEM_W), lambda i: (i // nt, 2 * layer + 1)),
                  pl.BlockSpec((tm, D_MODEL), row_map),
                  pl.BlockSpec(wo.shape, const),
                  pl.BlockSpec(lng.shape, const),
                  pl.BlockSpec(lnb.shape, const),
                  pl.BlockSpec(wrh.shape, const),
                  pl.BlockSpec(wrl.shape, const),
                  pl.BlockSpec(br.shape, const)],
        out_specs=[pl.BlockSpec((tm, D_MODEL), row_map),
                   pl.BlockSpec((N_EXPERTS, tm), lambda i: (0, i))],
        out_shape=[jax.ShapeDtypeStruct((m, D_MODEL), F32),
                   jax.ShapeDtypeStruct((N_EXPERTS, m), F32)],
        compiler_params=_cparams("parallel"),
        name="out_ln_router",
    )(o, mq, mkv, mkv, x2d, wo, lng, lnb, wrh, wrl, br)


def _moe_kernel(x_ref, gt_ref, wg_ref, wu_ref, wd_ref, lng_ref, lnb_ref, o_ref, acc_ref, xb_ref, gc_ref):
    e = pl.program_id(1)

    @pl.when(e == 0)
    def _():
        acc_ref[...] = jnp.zeros_like(acc_ref)
        xb_ref[...] = x_ref[...].astype(BF16)
        gc_ref[...] = gt_ref[...].T

    xb = xb_ref[...]
    hg = _dot(xb, wg_ref[0])
    hu = _dot(xb, wu_ref[0])
    h = (hg * _sigmoid(hg) * hu).astype(BF16)
    lane = lax.broadcasted_iota(jnp.int32, gc_ref.shape, 1)
    ge = jnp.sum(jnp.where(lane == e, gc_ref[...], 0.0), axis=1, keepdims=True)
    acc_ref[...] += ge * _dot(h, wd_ref[0])

    @pl.when(e == N_EXPERTS - 1)
    def _():
        o_ref[...] = _layer_norm(DN_ALPHA * x_ref[...] + acc_ref[...], lng_ref[...], lnb_ref[...])


def _moe(x1, gt, wg, wu, wd, lng, lnb, tm):
    m = x1.shape[0]
    return pl.pallas_call(
        _moe_kernel,
        grid=(m // tm, N_EXPERTS),
        in_specs=[pl.BlockSpec((tm, D_MODEL), lambda i, e: (i, 0)),
                  pl.BlockSpec((N_EXPERTS, tm), lambda i, e: (0, i)),
                  pl.BlockSpec((1, D_MODEL, D_EXPERT), lambda i, e: (e, 0, 0)),
                  pl.BlockSpec((1, D_MODEL, D_EXPERT), lambda i, e: (e, 0, 0)),
                  pl.BlockSpec((1, D_EXPERT, D_MODEL), lambda i, e: (e, 0, 0)),
                  pl.BlockSpec(lng.shape, lambda i, e: (0, 0)),
                  pl.BlockSpec(lnb.shape, lambda i, e: (0, 0))],
        out_specs=pl.BlockSpec((tm, D_MODEL), lambda i, e: (i, 0)),
        out_shape=jax.ShapeDtypeStruct((m, D_MODEL), F32),
        scratch_shapes=[pltpu.VMEM((tm, D_MODEL), F32),
                        pltpu.VMEM((tm, D_MODEL), BF16),
                        pltpu.VMEM((tm, N_EXPERTS), F32)],
        compiler_params=_cparams("parallel", "arbitrary"),
        name="moe",
    )(x1, gt, wg, wu, wd, lng, lnb)


FOX_AUG = 6


def _fox_placement():
    pl_q = np.zeros((3 * LANE, SELF_W), np.float32)
    pl_k = np.zeros((3 * LANE, SELF_W), np.float32)
    ones_q = np.zeros((1, SELF_W), np.float32)
    ones_k = np.zeros((1, SELF_W), np.float32)
    for h in range(FOX_HEADS):
        base = LANE * (h // 2) + FOX_AUG * (h % 2)
        for piece in range(3):
            pl_q[piece * LANE + h, base + piece] = 1.0
            pl_k[piece * LANE + h, base + 3 + piece] = -1.0
            ones_q[0, base + 3 + piece] = 1.0
            ones_k[0, base + piece] = 1.0
    return (jnp.asarray(pl_q, BF16), jnp.asarray(pl_k, BF16), jnp.asarray(ones_q), jnp.asarray(ones_k))


def _proj_b_kernel(x_ref, wq_ref, wog_ref, wmq_ref, wk_ref, wv_ref, wf_ref, bf_ref, plq_ref, plk_ref,
                   oq_ref, ok_ref, q_ref, k_ref, v_ref, og_ref, mq_ref, aq_ref, ak_ref, carry_ref):
    tm = x_ref.shape[0]

    @pl.when(pl.program_id(1) == 0)
    def _():
        carry_ref[...] = jnp.zeros_like(carry_ref)

    xb = x_ref[...].astype(BF16)
    for w_ref, o_ref in ((wq_ref, q_ref), (wk_ref, k_ref), (wv_ref, v_ref), (wog_ref, og_ref), (wmq_ref, mq_ref)):
        o_ref[...] = _dot(xb, w_ref[...]).astype(o_ref.dtype)

    log_f = _log_sigmoid(_dot(xb, wf_ref[...]) + bf_ref[...])
    row = lax.broadcasted_iota(jnp.int32, (tm, tm), 0)
    col = lax.broadcasted_iota(jnp.int32, (tm, tm), 1)
    tri = jnp.where(col <= row, 1.0, 0.0).astype(BF16)
    f_hi, f_mid, f_lo = _split3(log_f)
    cum = carry_ref[...] + (_dot(tri, f_hi) + _dot(tri, f_mid) + _dot(tri, f_lo))
    carry_ref[...] = cum[tm - 1:tm, :]
    c3 = jnp.concatenate(_split3(cum * LOG2E), axis=1)
    aq_ref[...] = (_dot(c3, plq_ref[...]) + oq_ref[...]).astype(aq_ref.dtype)
    ak_ref[...] = (_dot(c3, plk_ref[...]) + ok_ref[...]).astype(ak_ref.dtype)


def _proj_b(x2d, wq, wog, wmq, wk, wv, wf, bf, batch, seq, tm):
    m = batch * seq
    nt = seq // tm
    row_map = lambda b, i: (b * nt + i, 0)
    const = lambda b, i: (0, 0)
    consts = (wq, wog, wmq, wk, wv, wf, bf) + _fox_placement()
    widths = (SELF_W, SELF_W, SELF_W, SELF_W, MEM_W, SELF_W, SELF_W)
    return pl.pallas_call(
        _proj_b_kernel,
        grid=(batch, nt),
        in_specs=[pl.BlockSpec((tm, D_MODEL), row_map)] + [pl.BlockSpec(a.shape, const) for a in consts],
        out_specs=[pl.BlockSpec((tm, n), row_map) for n in widths],
        out_shape=[jax.ShapeDtypeStruct((m, n), BF16) for n in widths],
        scratch_shapes=[pltpu.VMEM((1, LANE), F32)],
        compiler_params=_cparams("parallel", "arbitrary"),
        name="proj_b",
    )(x2d, *consts)


def _pair_rms_norm(t, gain):
    lo_half = lax.broadcasted_iota(jnp.int32, t.shape, 1) < FOX_HD
    sq = t * t
    ss_lo = jnp.sum(jnp.where(lo_half, sq, 0.0), axis=-1, keepdims=True)
    ss_hi = jnp.sum(jnp.where(lo_half, 0.0, sq), axis=-1, keepdims=True)
    ss = jnp.where(lo_half, ss_lo, ss_hi)
    return t * lax.rsqrt(ss * (1.0 / FOX_HD) + RMS_EPS) * gain


def _fox_kernel(q_ref, aq_ref, k_ref, ak_ref, v_ref, og_ref, qg_ref, kg_ref, o_ref,
                ka_ref, qa_ref, m_ref, l_ref, acc_ref):
    tq = q_ref.shape[0]
    seq = k_ref.shape[1]
    qi = pl.program_id(2)

    @pl.when(qi == 0)
    def _():
        def fill(c, carry):
            rows = pl.ds(pl.multiple_of(c * tq, tq), tq)
            kn = _pair_rms_norm(k_ref[0, rows, :].astype(F32), kg_ref[...])
            ka_ref[rows, :] = jnp.concatenate([kn.astype(BF16), ak_ref[0, rows, :]], axis=1)
            return carry
        lax.fori_loop(0, seq // tq, fill, 0)

    lane = lax.broadcasted_iota(jnp.int32, (tq, LANE), 1)
    qn = _pair_rms_norm(q_ref[...].astype(F32), qg_ref[...]) * ((FOX_HD ** -0.5) * LOG2E)
    aq = aq_ref[...]
    for hh in range(2):
        feat = (lane >= FOX_HD * hh) & (lane < FOX_HD * (hh + 1))
        bias = (lane >= FOX_AUG * hh) & (lane < FOX_AUG * (hh + 1))
        qa_ref[hh] = jnp.concatenate([jnp.where(feat, qn, 0.0).astype(BF16),
                                      jnp.where(bias, aq, jnp.zeros_like(aq))], axis=1)

    row = lax.broadcasted_iota(jnp.int32, (tq, tq), 0)
    col = lax.broadcasted_iota(jnp.int32, (tq, tq), 1)
    causal = col <= row

    m_ref[...] = jnp.full(m_ref.shape, NEG_BIG, F32)
    l_ref[...] = jnp.zeros(l_ref.shape, F32)
    acc_ref[...] = jnp.zeros(acc_ref.shape, F32)
    n_cb = tq // LANE

    def step(j, masked):
        rows = pl.ds(pl.multiple_of(j * tq, tq), tq)
        vt = v_ref[0, rows, :]
        kt = ka_ref[rows, :]
        for hh in range(2):
            s = _dot_nt(qa_ref[hh], kt)
            if masked:
                s = jnp.where(causal, s, NEG_BIG)
            s_cb = [s[:, LANE * c:LANE * (c + 1)] for c in range(n_cb)]
            m_old = m_ref[hh]
            row_max = jnp.max(functools.reduce(jnp.maximum, s_cb), axis=-1, keepdims=True)
            m_new = jnp.maximum(m_old, jnp.broadcast_to(row_max, (tq, LANE)))
            alpha = jnp.exp2(m_old - m_new)
            p_cb = [jnp.exp2(sc - m_new) for sc in s_cb]
            l_ref[hh] = alpha * l_ref[hh] + functools.reduce(lambda a, b: a + b, p_cb)
            p = jnp.concatenate([pc.astype(BF16) for pc in p_cb], axis=1)
            acc_ref[hh] = alpha * acc_ref[hh] + _dot(p, vt)
            m_ref[hh] = m_new

    def body(j, carry):
        step(j, False)
        return carry

    lax.fori_loop(0, qi, body, 0)
    step(qi, True)

    l0 = jnp.sum(l_ref[0], axis=-1, keepdims=True)
    l1 = jnp.sum(l_ref[1], axis=-1, keepdims=True)
    o = jnp.where(lane < FOX_HD, acc_ref[0] / l0, acc_ref[1] / l1)
    og = og_ref[...].astype(F32)
    o_ref[...] = (o * _sigmoid(og)).astype(o_ref.dtype)


def _fox(q, aq, k, ak, v, og, qg2, kg2, batch, seq, tq):
    nq = seq // tq
    m = batch * seq
    k3, ak3, v3 = (a.reshape(batch, seq, SELF_W) for a in (k, ak, v))
    tile_map = lambda b, p, i: (b * nq + i, p)
    seq_map = lambda b, p, i: (b, 0, p)
    const = lambda b, p, i: (0, 0)
    return pl.pallas_call(
        _fox_kernel,
        grid=(batch, FOX_PAIRS, nq),
        in_specs=[pl.BlockSpec((tq, LANE), tile_map),
                  pl.BlockSpec((tq, LANE), tile_map),
                  pl.BlockSpec((1, seq, LANE), seq_map),
                  pl.BlockSpec((1, seq, LANE), seq_map),
                  pl.BlockSpec((1, seq, LANE), seq_map),
                  pl.BlockSpec((tq, LANE), tile_map),
                  pl.BlockSpec((1, LANE), const),
                  pl.BlockSpec((1, LANE), const)],
        out_specs=pl.BlockSpec((tq, LANE), tile_map),
        out_shape=jax.ShapeDtypeStruct((m, SELF_W), BF16),
        scratch_shapes=[pltpu.VMEM((seq, 2 * LANE), BF16),
                        pltpu.VMEM((2, tq, 2 * LANE), BF16),
                        pltpu.VMEM((2, tq, LANE), F32),
                        pltpu.VMEM((2, tq, LANE), F32),
                        pltpu.VMEM((2, tq, LANE), F32)],
        compiler_params=_cparams("parallel", "parallel", "arbitrary"),
        name="fox",
    )(q, aq, k3, ak3, v3, og, qg2, kg2)


def _pad_heads(w, n_heads, d, d_pad):
    lead = w.shape[:-1]
    w = w.reshape(lead + (n_heads, d))
    w = jnp.pad(w, [(0, 0)] * len(lead) + [(0, 0), (0, d_pad - d)])
    return w.reshape(lead + (n_heads * d_pad,))


def _pad_last(w, n):
    return jnp.pad(w, [(0, 0)] * (w.ndim - 1) + [(0, n - w.shape[-1])])


def kernel(x, mem, w_in_a, w_gate_up_a, b_gate_a, gla_norm_g, w_in_b, q_norm_g, w_kv_shared, b_forget, k_norm_g, w_mem_kv, w_out, ln_mix_g, ln_mix_b, ln_ffn_g, ln_ffn_b, w_router, b_router, w_exp_gate, w_exp_up, w_exp_down):
    batch, seq, d = x.shape
    m = batch * seq
    tm = 512
    x2d = x.reshape(m, d)

    wa = w_in_a[0]
    s0, s1, s2, s3, s4 = (GLA_KDIM, 2 * GLA_KDIM, 2 * GLA_KDIM + SELF_W,
                          2 * GLA_KDIM + SELF_W + GLA_GATE_RANK, 2 * GLA_KDIM + 2 * SELF_W + GLA_GATE_RANK)
    w_a = jnp.concatenate([
        _pad_heads(wa[:, :s0], GLA_HEADS, GLA_DK, GLA_DK_PAD),
        _pad_heads(wa[:, s0:s1], GLA_HEADS, GLA_DK, GLA_DK_PAD),
        wa[:, s1:s2], wa[:, s3:s4], wa[:, s4:], _pad_last(wa[:, s2:s3], LANE)], axis=1).astype(BF16)
    wgu = _pad_heads(w_gate_up_a[0], GLA_HEADS, GLA_DK, GLA_DK_PAD)
    wgu = jnp.pad(wgu, ((0, LANE - GLA_GATE_RANK), (0, 0))).astype(BF16)
    bg = _pad_heads(b_gate_a[0], GLA_HEADS, GLA_DK, GLA_DK_PAD).reshape(1, GLA_KPAD)
    gn = jnp.tile(gla_norm_g[0], GLA_HEADS).reshape(1, SELF_W)

    wb = w_in_b[0]
    wq, wog, wmq = (wb[:, :SELF_W].astype(BF16), wb[:, SELF_W:2 * SELF_W].astype(BF16),
                    wb[:, 2 * SELF_W:].astype(BF16))
    wk, wv = w_kv_shared[:, :SELF_W].astype(BF16), w_kv_shared[:, SELF_W:2 * SELF_W].astype(BF16)
    wf = _pad_last(w_kv_shared[:, 2 * SELF_W:], LANE).astype(BF16)
    bf = _pad_last(b_forget, LANE).reshape(1, LANE)
    qg2 = jnp.tile(q_norm_g[0], 2).reshape(1, LANE)
    kg2 = jnp.tile(k_norm_g, 2).reshape(1, LANE)

    w_mkv = jnp.concatenate([w_mem_kv[l] for l in range(DEPTH)], axis=1).astype(BF16)
    wo = w_out.astype(BF16)
    wr_t = w_router.T
    wrh = wr_t.astype(BF16)
    wrl = (wr_t - wrh.astype(F32)).astype(BF16)
    br = b_router.reshape(N_EXPERTS, 1)
    weg, weu, wed = w_exp_gate.astype(BF16), w_exp_up.astype(BF16), w_exp_down.astype(BF16)
    row = lambda a: a.reshape(1, d)

    mkv = _mem_kv(mem.reshape(batch * N_MEM, d), w_mkv)

    def tail(o, mq, xin, layer):
        x1, gt = _out_block(o, mq, mkv, layer, xin, wo[layer], row(ln_mix_g[layer]), row(ln_mix_b[layer]),
                            wrh, wrl, br, seq, tm)
        return _moe(x1, gt, weg[layer], weu[layer], wed[layer], row(ln_ffn_g[layer]), row(ln_ffn_b[layer]), 1024)

    q, k, v, r, mq, g = _proj_a(x2d, w_a, tm)
    o = _gla(q, k, v, r, g, wgu, bg, gn, batch, seq, tm)
    xa = tail(o, mq, x2d, 0)

    qb, kb, vb, og, mqb, aq, ak = _proj_b(xa, wq, wog, wmq, wk, wv, wf, bf, batch, seq, tm)
    ob = _fox(qb, aq, kb, ak, vb, og, qg2, kg2, batch, seq, tm)
    xb = tail(ob, mqb, xa, 1)
    return xb.reshape(batch, seq, d)
```

```python
import functools
import math

import jax
import jax.numpy as jnp
import numpy as np
from jax import lax
from jax.experimental import pallas as pl
from jax.experimental.pallas import tpu as pltpu

F32 = jnp.float32
BF16 = jnp.bfloat16

D_MODEL = 1024
DEPTH = 2
CHUNK = 64
N_MEM = 256
MEM_HEADS = 4
MEM_W = D_MODEL // 4
MEM_HD = MEM_W // MEM_HEADS
SELF_W = D_MODEL - MEM_W
GLA_HEADS = 4
GLA_KDIM = SELF_W // 2
GLA_DK = GLA_KDIM // GLA_HEADS
GLA_DV = SELF_W // GLA_HEADS
GLA_GATE_RANK = 16
GLA_TAU = 16.0
FOX_HD = 64
FOX_HEADS = SELF_W // FOX_HD
N_EXPERTS = 16
N_GROUPS = 4
EXPERTS_PER_GROUP = N_EXPERTS // N_GROUPS
D_EXPERT = D_MODEL // 2
DN_ALPHA = (2.0 * DEPTH) ** 0.25
LN_EPS = 1e-5
RMS_EPS = 1e-6

LANE = 128
GLA_DK_PAD = LANE
GLA_KPAD = GLA_HEADS * GLA_DK_PAD
GLA_VBLOCKS = SELF_W // LANE
GLA_HEAD_BLOCKS = ((0, 1), (1, 2), (3, 4), (4, 5))
FOX_PAIRS = FOX_HEADS // 2
LOG2E = math.log2(math.e)
NEG_BIG = -1e30
VMEM_LIMIT_BYTES = 48 * 1024 * 1024

NT_DIMS = (((1,), (1,)), ((), ()))
TN_DIMS = (((0,), (0,)), ((), ()))


def _cparams(*sem):
    return pltpu.CompilerParams(dimension_semantics=sem, vmem_limit_bytes=VMEM_LIMIT_BYTES)


def _dot(a, b):
    return jnp.dot(a, b, preferred_element_type=F32)


def _dot_nt(a, b):
    return lax.dot_general(a, b, NT_DIMS, preferred_element_type=F32)


def _dot_tn(a, b):
    return lax.dot_general(a, b, TN_DIMS, preferred_element_type=F32)


def _log_sigmoid(z):
    return -(jnp.maximum(-z, 0.0) + jnp.log1p(jnp.exp(-jnp.abs(z))))


def _sigmoid(z):
    return 1.0 / (1.0 + jnp.exp(-z))


def _split2(x):
    hi = x.astype(BF16)
    lo = (x - hi.astype(F32)).astype(BF16)
    return hi, lo


def _split3(x):
    hi = x.astype(BF16)
    r1 = x - hi.astype(F32)
    mid = r1.astype(BF16)
    lo = (r1 - mid.astype(F32)).astype(BF16)
    return hi, mid, lo


def _layer_norm(x, g, b):
    mu = jnp.mean(x, axis=-1, keepdims=True)
    xc = x - mu
    var = jnp.mean(xc * xc, axis=-1, keepdims=True)
    return xc * lax.rsqrt(var + LN_EPS) * g + b


def _proj_a_kernel(x_ref, w_ref, q_ref, k_ref, v_ref, r_ref, mq_ref, g_ref):
    xb = x_ref[...].astype(BF16)
    c = 0
    for ref in (q_ref, k_ref, v_ref, r_ref, mq_ref, g_ref):
        n = ref.shape[1]
        ref[...] = _dot(xb, w_ref[:, c:c + n]).astype(ref.dtype)
        c += n


def _proj_a(x2d, w, tm):
    m = x2d.shape[0]
    widths = (GLA_KPAD, GLA_KPAD, SELF_W, SELF_W, MEM_W, LANE)
    assert w.shape == (D_MODEL, sum(widths))
    return pl.pallas_call(
        _proj_a_kernel,
        grid=(m // tm,),
        in_specs=[pl.BlockSpec((tm, D_MODEL), lambda i: (i, 0)),
                  pl.BlockSpec(w.shape, lambda i: (0, 0))],
        out_specs=[pl.BlockSpec((tm, n), lambda i: (i, 0)) for n in widths],
        out_shape=[jax.ShapeDtypeStruct((m, n), BF16) for n in widths],
        compiler_params=_cparams("parallel"),
        name="proj_a",
    )(x2d, w)


def _gla_kernel(q_ref, k_ref, v_ref, r_ref, g_ref, wgu_ref, bg_ref, gn_ref, o_ref, st_ref, la_ref):
    tg = q_ref.shape[0]

    @pl.when(pl.program_id(1) == 0)
    def _():
        st_ref[...] = jnp.zeros_like(st_ref)

    z = _dot(g_ref[...], wgu_ref[...]) + bg_ref[...]
    la_ref[...] = _log_sigmoid(z) * (1.0 / GLA_TAU)

    row = lax.broadcasted_iota(jnp.int32, (CHUNK, CHUNK), 0)
    col = lax.broadcasted_iota(jnp.int32, (CHUNK, CHUNK), 1)
    causal = col <= row
    tri = jnp.where(causal, 1.0, 0.0).astype(BF16)
    lo_half = lax.broadcasted_iota(jnp.int32, (CHUNK, LANE), 1) < (LANE // 2)
    scale = GLA_DK ** -0.5

    def chunk(c, carry):
        r0 = pl.multiple_of(c * CHUNK, CHUNK)
        rows = pl.ds(r0, CHUNK)
        la_hi, la_lo = _split2(la_ref[rows, :])
        b = _dot(tri, la_hi) + _dot(tri, la_lo)
        bl = b[CHUNK - 1:CHUNK, :]
        qc = q_ref[rows, :].astype(F32) * scale
        kc = k_ref[rows, :].astype(F32)
        qd = (qc * jnp.exp(b)).astype(BF16)
        ki = (kc * jnp.exp(-b)).astype(BF16)
        ke = (kc * jnp.exp(bl - b)).astype(BF16)
        dec = jnp.exp(bl)
        unit = []
        for h in range(GLA_HEADS):
            sl = slice(GLA_DK_PAD * h, GLA_DK_PAD * (h + 1))
            qh, kih, keh = qd[:, sl], ki[:, sl], ke[:, sl]
            att = jnp.where(causal, _dot_nt(qh, kih), 0.0).astype(BF16)
            for t, blk in enumerate(GLA_HEAD_BLOCKS[h]):
                u = 2 * h + t
                vb = v_ref[rows, LANE * blk:LANE * (blk + 1)]
                st = st_ref[u]
                unit.append(_dot(att, vb) + _dot_nt(qh, st.astype(BF16)))
                st_ref[u] = st * dec[:, sl] + _dot_tn(vb, keh)
        o_blk = [unit[0], jnp.where(lo_half, unit[1], unit[2]), unit[3],
                 unit[4], jnp.where(lo_half, unit[5], unit[6]), unit[7]]
        sq = [o * o for o in o_blk]
        full = [jnp.sum(s, axis=-1, keepdims=True) for s in sq]
        lo1 = jnp.sum(jnp.where(lo_half, sq[1], 0.0), axis=-1, keepdims=True)
        hi1 = jnp.sum(jnp.where(lo_half, 0.0, sq[1]), axis=-1, keepdims=True)
        lo4 = jnp.sum(jnp.where(lo_half, sq[4], 0.0), axis=-1, keepdims=True)
        hi4 = jnp.sum(jnp.where(lo_half, 0.0, sq[4]), axis=-1, keepdims=True)
        ss = [full[0] + lo1, hi1 + full[2], full[3] + lo4, hi4 + full[5]]
        inv = [lax.rsqrt(s * (1.0 / GLA_DV) + RMS_EPS) for s in ss]
        inv_blk = [inv[0], jnp.where(lo_half, inv[0], inv[1]), inv[1],
                   inv[2], jnp.where(lo_half, inv[2], inv[3]), inv[3]]
        for blk in range(GLA_VBLOCKS):
            cs = slice(LANE * blk, LANE * (blk + 1))
            rg = r_ref[rows, cs].astype(F32)
            y = o_blk[blk] * inv_blk[blk] * gn_ref[:, cs]
            o_ref[rows, cs] = (y * (rg * _sigmoid(rg))).astype(o_ref.dtype)
        return carry

    lax.fori_loop(0, tg // CHUNK, chunk, 0)


def _gla(q, k, v, r, g, wgu, bg, gn, batch, seq, tg):
    m = batch * seq
    nt = seq // tg
    row_map = lambda b, i: (b * nt + i, 0)
    const = lambda b, i: (0, 0)
    return pl.pallas_call(
        _gla_kernel,
        grid=(batch, nt),
        in_specs=[pl.BlockSpec((tg, GLA_KPAD), row_map),
                  pl.BlockSpec((tg, GLA_KPAD), row_map),
                  pl.BlockSpec((tg, SELF_W), row_map),
                  pl.BlockSpec((tg, SELF_W), row_map),
                  pl.BlockSpec((tg, LANE), row_map),
                  pl.BlockSpec(wgu.shape, const),
                  pl.BlockSpec(bg.shape, const),
                  pl.BlockSpec(gn.shape, const)],
        out_specs=pl.BlockSpec((tg, SELF_W), row_map),
        out_shape=jax.ShapeDtypeStruct((m, SELF_W), BF16),
        scratch_shapes=[pltpu.VMEM((2 * GLA_HEADS, LANE, GLA_DK_PAD), F32),
                        pltpu.VMEM((tg, GLA_KPAD), F32)],
        compiler_params=_cparams("parallel", "arbitrary"),
        name="gla",
    )(q, k, v, r, g, wgu, bg, gn)


def _mem_kv_kernel(m_ref, w_ref, o_ref):
    o_ref[...] = _dot(m_ref[...].astype(BF16), w_ref[...]).astype(o_ref.dtype)


def _mem_kv(mem2d, w):
    m, n = mem2d.shape[0], w.shape[1]
    tm = N_MEM
    return pl.pallas_call(
        _mem_kv_kernel,
        grid=(m // tm,),
        in_specs=[pl.BlockSpec((tm, D_MODEL), lambda i: (i, 0)),
                  pl.BlockSpec(w.shape, lambda i: (0, 0))],
        out_specs=pl.BlockSpec((tm, n), lambda i: (i, 0)),
        out_shape=jax.ShapeDtypeStruct((m, n), BF16),
        compiler_params=_cparams("parallel"),
        name="mem_kv",
    )(mem2d, w)


def _top2_sum(a, b, c, d):
    p, q = jnp.maximum(a, b), jnp.minimum(a, b)
    r, s = jnp.maximum(c, d), jnp.minimum(c, d)
    return jnp.maximum(p, r) + jnp.maximum(jnp.minimum(p, r), jnp.maximum(q, s))


PAIRS = tuple((i, j) for i in range(EXPERTS_PER_GROUP) for j in range(i + 1, EXPERTS_PER_GROUP))
N_CLASSES = N_GROUPS * len(PAIRS)
CLS_PAD = 32
META_W = LANE
ROW_W = D_MODEL + META_W
CLASS_EA = np.array([EXPERTS_PER_GROUP * (c // len(PAIRS)) + PAIRS[c % len(PAIRS)][0] for c in range(N_CLASSES)], np.int32)
CLASS_EB = np.array([EXPERTS_PER_GROUP * (c // len(PAIRS)) + PAIRS[c % len(PAIRS)][1] for c in range(N_CLASSES)], np.int32)


def _router(aff, sel):
    score = [_top2_sum(*sel[EXPERTS_PER_GROUP * g:EXPERTS_PER_GROUP * (g + 1)]) for g in range(N_GROUPS)]
    picked = []
    for g in range(N_GROUPS):
        ok = None
        for i in range(N_GROUPS):
            if i == g:
                continue
            c = (score[g] > score[i]) if i < g else (score[g] >= score[i])
            ok = c if ok is None else (ok & c)
        picked.append(ok)
    chosen, w = [], []
    for g in range(N_GROUPS):
        for j in range(EXPERTS_PER_GROUP):
            ej = EXPERTS_PER_GROUP * g + j
            rank = None
            for i in range(EXPERTS_PER_GROUP):
                if i == j:
                    continue
                ei = EXPERTS_PER_GROUP * g + i
                beats = (sel[ei] >= sel[ej]) if i < j else (sel[ei] > sel[ej])
                beats = jnp.where(beats, 1.0, 0.0)
                rank = beats if rank is None else rank + beats
            chosen.append(picked[g] & (rank < 1.5))
            w.append(jnp.where(chosen[-1], aff[ej], 0.0))
    denom = functools.reduce(lambda a, b: a + b, w)
    gates = [wi / denom for wi in w]
    onehot, w_a, w_b = [], None, None
    for c in range(N_CLASSES):
        oc = chosen[CLASS_EA[c]] & chosen[CLASS_EB[c]]
        onehot.append(oc)
        ga, gb = jnp.where(oc, gates[CLASS_EA[c]], 0.0), jnp.where(oc, gates[CLASS_EB[c]], 0.0)
        w_a = ga if w_a is None else w_a + ga
        w_b = gb if w_b is None else w_b + gb
    return onehot, w_a, w_b


def _out_kernel(o_ref, mq_ref, mk_ref, mv_ref, x_ref, wo_ref, lng_ref, lnb_ref,
                wrh_ref, wrl_ref, br_ref, x1_ref, idx_ref, cnt_ref, carry_ref):
    tm = o_ref.shape[0]

    @pl.when(pl.program_id(0) == 0)
    def _():
        carry_ref[...] = jnp.zeros_like(carry_ref)

    mq = mq_ref[...]
    mk = mk_ref[...]
    mv = mv_ref[...]
    lane = lax.broadcasted_iota(jnp.int32, (N_MEM, MEM_W), 1)
    m_out = jnp.zeros((tm, MEM_W), F32)
    for h in range(MEM_HEADS):
        in_head = (lane >= MEM_HD * h) & (lane < MEM_HD * (h + 1))
        s = _dot_nt(mq, jnp.where(in_head, mk, jnp.zeros_like(mk))) * (MEM_HD ** -0.5)
        p = jnp.exp(s - jnp.max(s, axis=-1, keepdims=True))
        l = jnp.sum(p, axis=-1, keepdims=True)
        m_out = m_out + _dot(p.astype(BF16), jnp.where(in_head, mv, jnp.zeros_like(mv))) / l
    y = _dot(o_ref[...], wo_ref[:SELF_W, :]) + _dot(m_out.astype(BF16), wo_ref[SELF_W:, :])
    x1 = _layer_norm(DN_ALPHA * x_ref[...] + y, lng_ref[...], lnb_ref[...])
    x1_ref[:, :D_MODEL] = x1

    xh, xl = _split2(x1)
    wrh = wrh_ref[...]
    logits = _dot_nt(wrh, xh) + _dot_nt(wrh, xl) + _dot_nt(wrl_ref[...], xh)
    aff = _sigmoid(logits)
    sel = aff + br_ref[...]
    onehot, w_a, w_b = _router([aff[e:e + 1, :] for e in range(N_EXPERTS)],
                               [sel[e:e + 1, :] for e in range(N_EXPERTS)])

    sub = lax.broadcasted_iota(jnp.int32, (META_W, tm), 0)
    meta = jnp.where(sub == 0, w_a, jnp.where(sub == 1, w_b, 0.0))
    x1_ref[:, D_MODEL:] = meta.T

    csub = lax.broadcasted_iota(jnp.int32, (CLS_PAD, tm), 0)
    oh = jnp.zeros((CLS_PAD, tm), F32)
    for c in range(N_CLASSES):
        oh = jnp.where((csub == c) & onehot[c], 1.0, oh)
    row = lax.broadcasted_iota(jnp.int32, (tm, tm), 0)
    col = lax.broadcasted_iota(jnp.int32, (tm, tm), 1)
    tri_u = jnp.where(row <= col, 1.0, 0.0).astype(BF16)
    incl = _dot(oh.astype(BF16), tri_u)
    carry = carry_ref[...]
    rank = jnp.sum(oh * (carry + incl - 1.0), axis=0, keepdims=True)
    cls = jnp.sum(oh * csub.astype(F32), axis=0, keepdims=True)
    idx_ref[0, 0:1, :] = cls.astype(jnp.int32)
    idx_ref[0, 1:2, :] = rank.astype(jnp.int32)
    carry = carry + jnp.sum(oh, axis=1, keepdims=True)
    carry_ref[...] = carry
    cnt_ref[...] = carry[:, :LANE]


def _out_block(o, mq, mkv, layer, x2d, wo, lng, lnb, wrh, wrl, br, seq, tm):
    m = x2d.shape[0]
    nt = seq // tm
    row_map = lambda i: (i, 0)
    const = lambda i: (0, 0)
    return pl.pallas_call(
        _out_kernel,
        grid=(m // tm,),
        in_specs=[pl.BlockSpec((tm, SELF_W), row_map),
                  pl.BlockSpec((tm, MEM_W), row_map),
                  pl.BlockSpec((N_MEM, MEM_W), lambda i: (i // nt, 2 * layer)),
                  pl.BlockSpec((N_MEM, MEM_W), lambda i: (i // nt, 2 * layer + 1)),
                  pl.BlockSpec((tm, D_MODEL), row_map),
                  pl.BlockSpec(wo.shape, const),
                  pl.BlockSpec(lng.shape, const),
                  pl.BlockSpec(lnb.shape, const),
                  pl.BlockSpec(wrh.shape, const),
                  pl.BlockSpec(wrl.shape, const),
                  pl.BlockSpec(br.shape, const)],
        out_specs=[pl.BlockSpec((tm, ROW_W), row_map),
                   pl.BlockSpec((1, 2, tm), lambda i: (i, 0, 0)),
                   pl.BlockSpec((CLS_PAD, LANE), const)],
        out_shape=[jax.ShapeDtypeStruct((m, ROW_W), F32),
                   jax.ShapeDtypeStruct((m // tm, 2, tm), jnp.int32),
                   jax.ShapeDtypeStruct((CLS_PAD, LANE), F32)],
        scratch_shapes=[pltpu.VMEM((CLS_PAD, tm), F32)],
        compiler_params=_cparams("arbitrary"),
        name="out_ln_router",
    )(o, mq, mkv, mkv, x2d, wo, lng, lnb, wrh, wrl, br)


def _moe_plan(counts, tmx, n_tiles):
    cnt = counts[:N_CLASSES, 0].astype(jnp.int32)
    nt = (cnt + (tmx - 1)) // tmx
    tend = jnp.cumsum(nt)
    offs = (tend - nt) * tmx
    n_used = tend[-1:]
    i = jnp.arange(n_tiles, dtype=jnp.int32)
    ieff = jnp.minimum(i, n_used - 1)
    cls = jnp.sum((ieff[:, None] >= tend[None, :]).astype(jnp.int32), axis=1)
    fresh = jnp.concatenate([jnp.ones((1,), jnp.int32), (cls[1:] != cls[:-1]).astype(jnp.int32)])
    last_tile_row = jnp.where(nt > 0, (tend - 1) * tmx, -1)
    spare = n_used + jnp.arange(N_CLASSES, dtype=jnp.int32)
    spare_row = jnp.where(spare < n_tiles, spare * tmx, -1)
    pad = lambda a: jnp.pad(a, (0, CLS_PAD - N_CLASSES))
    return dict(offs=pad(offs), zero_rows=jnp.concatenate([last_tile_row, spare_row]), row_block=ieff,
                ea=jnp.asarray(CLASS_EA)[cls], eb=jnp.asarray(CLASS_EB)[cls], fresh=fresh, n_used=n_used)


def _load_tile_idx(idx_ref, idx_s, sem):
    cp = pltpu.make_async_copy(idx_ref.at[0], idx_s, sem)
    cp.start()
    cp.wait()


def _dispatch_kernel(offs_ref, zrow_ref, idx_ref, x_ref, xs_ref, idx_s, zero_ref, sem_i, sem_z, sem):
    tm = x_ref.shape[0]
    tmx = zero_ref.shape[0]

    @pl.when(pl.program_id(0) == 0)
    def _():
        zero_ref[...] = jnp.zeros_like(zero_ref)

        def zcopy(c):
            return pltpu.make_async_copy(zero_ref, xs_ref.at[pl.ds(pl.multiple_of(zrow_ref[c], tmx), tmx)], sem_z)

        for c in range(2 * N_CLASSES):
            @pl.when(zrow_ref[c] >= 0)
            def _(c=c):
                zcopy(c).start()
        for c in range(2 * N_CLASSES):
            @pl.when(zrow_ref[c] >= 0)
            def _(c=c):
                zcopy(c).wait()

    _load_tile_idx(idx_ref, idx_s, sem_i)

    def issue(r, carry):
        d = offs_ref[idx_s[0, r]] + idx_s[1, r]
        pltpu.make_async_copy(x_ref.at[pl.ds(r, 1)], xs_ref.at[pl.ds(d, 1)], sem).start()
        return carry

    lax.fori_loop(0, tm, issue, 0, unroll=8)
    pltpu.make_async_copy(x_ref, xs_ref.at[pl.ds(0, tm)], sem).wait()


def _dispatch(plan, idx, x1e, tm, tmx, n_rows):
    m = x1e.shape[0]
    return pl.pallas_call(
        _dispatch_kernel,
        grid_spec=pltpu.PrefetchScalarGridSpec(
            num_scalar_prefetch=2,
            grid=(m // tm,),
            in_specs=[pl.BlockSpec((1, 2, tm), lambda i, *_: (i, 0, 0)),
                      pl.BlockSpec((tm, ROW_W), lambda i, *_: (i, 0))],
            out_specs=pl.BlockSpec(memory_space=pl.ANY),
            scratch_shapes=[pltpu.SMEM((2, tm), jnp.int32),
                            pltpu.VMEM((tmx, ROW_W), F32),
                            pltpu.SemaphoreType.DMA, pltpu.SemaphoreType.DMA, pltpu.SemaphoreType.DMA]),
        out_shape=jax.ShapeDtypeStruct((n_rows, ROW_W), F32),
        compiler_params=_cparams("arbitrary"),
        name="moe_dispatch",
    )(plan["offs"], plan["zero_rows"], idx, x1e)


def _moe_ffn_kernel(rb_ref, ea_ref, eb_ref, fresh_ref, nu_ref,
                    xs_ref, wga_ref, wua_ref, wda_ref, wgb_ref, wub_ref, wdb_ref, lng_ref, lnb_ref,
                    ys_ref, sg_ref, su_ref, sd_ref):
    i = pl.program_id(0)

    @pl.when(i >= nu_ref[0])
    def _():
        ys_ref[...] = jnp.zeros_like(ys_ref)

    @pl.when(i < nu_ref[0])
    def _():
        @pl.when(fresh_ref[i] == 1)
        def _():
            for s, (g, u, d) in enumerate(((wga_ref, wua_ref, wda_ref), (wgb_ref, wub_ref, wdb_ref))):
                sg_ref[s] = g[0].astype(BF16)
                su_ref[s] = u[0].astype(BF16)
                sd_ref[s] = d[0].astype(BF16)

        x = xs_ref[:, :D_MODEL]
        xb = x.astype(BF16)
        f = None
        for s in range(2):
            hg = _dot(xb, sg_ref[s])
            hu = _dot(xb, su_ref[s])
            h = (hg * _sigmoid(hg) * hu).astype(BF16)
            fs = xs_ref[:, D_MODEL + s:D_MODEL + s + 1] * _dot(h, sd_ref[s])
            f = fs if f is None else f + fs
        ys_ref[...] = _layer_norm(DN_ALPHA * x + f, lng_ref[...], lnb_ref[...])


def _moe_ffn(plan, xs, wg, wu, wd, lng, lnb, tmx, n_tiles):
    n_rows = xs.shape[0]
    rows = lambda i, rb, *_: (rb[i], 0)
    wa = lambda i, rb, ea, eb, *_: (ea[i], 0, 0)
    wb = lambda i, rb, ea, eb, *_: (eb[i], 0, 0)
    const = lambda i, *_: (0, 0)
    gu, dn = (1, D_MODEL, D_EXPERT), (1, D_EXPERT, D_MODEL)
    return pl.pallas_call(
        _moe_ffn_kernel,
        grid_spec=pltpu.PrefetchScalarGridSpec(
            num_scalar_prefetch=5,
            grid=(n_tiles,),
            in_specs=[pl.BlockSpec((tmx, ROW_W), rows),
                      pl.BlockSpec(gu, wa), pl.BlockSpec(gu, wa), pl.BlockSpec(dn, wa),
                      pl.BlockSpec(gu, wb), pl.BlockSpec(gu, wb), pl.BlockSpec(dn, wb),
                      pl.BlockSpec(lng.shape, const), pl.BlockSpec(lnb.shape, const)],
            out_specs=pl.BlockSpec((tmx, D_MODEL), lambda i, *_: (i, 0)),
            scratch_shapes=[pltpu.VMEM((2, D_MODEL, D_EXPERT), BF16),
                            pltpu.VMEM((2, D_MODEL, D_EXPERT), BF16),
                            pltpu.VMEM((2, D_EXPERT, D_MODEL), BF16)]),
        out_shape=jax.ShapeDtypeStruct((n_rows, D_MODEL), F32),
        compiler_params=_cparams("arbitrary"),
        name="moe_ffn",
    )(plan["row_block"], plan["ea"], plan["eb"], plan["fresh"], plan["n_used"],
      xs, wg, wu, wd, wg, wu, wd, lng, lnb)


def _unpermute_kernel(offs_ref, idx_ref, ys_ref, o_ref, idx_s, sem_i, sem):
    tm = o_ref.shape[0]
    _load_tile_idx(idx_ref, idx_s, sem_i)

    def issue(r, carry):
        d = offs_ref[idx_s[0, r]] + idx_s[1, r]
        pltpu.make_async_copy(ys_ref.at[pl.ds(d, 1)], o_ref.at[pl.ds(r, 1)], sem).start()
        return carry

    lax.fori_loop(0, tm, issue, 0, unroll=8)
    pltpu.make_async_copy(ys_ref.at[pl.ds(0, tm)], o_ref, sem).wait()


def _unpermute(plan, idx, ys, m, tm):
    return pl.pallas_call(
        _unpermute_kernel,
        grid_spec=pltpu.PrefetchScalarGridSpec(
            num_scalar_prefetch=1,
            grid=(m // tm,),
            in_specs=[pl.BlockSpec((1, 2, tm), lambda i, *_: (i, 0, 0)),
                      pl.BlockSpec(memory_space=pl.ANY)],
            out_specs=pl.BlockSpec((tm, D_MODEL), lambda i, *_: (i, 0)),
            scratch_shapes=[pltpu.SMEM((2, tm), jnp.int32),
                            pltpu.SemaphoreType.DMA, pltpu.SemaphoreType.DMA]),
        out_shape=jax.ShapeDtypeStruct((m, D_MODEL), F32),
        compiler_params=_cparams("arbitrary"),
        name="moe_unpermute",
    )(plan["offs"], idx, ys)


FOX_AUG = 6


def _fox_placement():
    pl_q = np.zeros((3 * LANE, SELF_W), np.float32)
    pl_k = np.zeros((3 * LANE, SELF_W), np.float32)
    ones_q = np.zeros((1, SELF_W), np.float32)
    ones_k = np.zeros((1, SELF_W), np.float32)
    for h in range(FOX_HEADS):
        base = LANE * (h // 2) + FOX_AUG * (h % 2)
        for piece in range(3):
            pl_q[piece * LANE + h, base + piece] = 1.0
            pl_k[piece * LANE + h, base + 3 + piece] = -1.0
            ones_q[0, base + 3 + piece] = 1.0
            ones_k[0, base + piece] = 1.0
    return (jnp.asarray(pl_q, BF16), jnp.asarray(pl_k, BF16), jnp.asarray(ones_q), jnp.asarray(ones_k))


def _proj_b_kernel(x_ref, wq_ref, wog_ref, wmq_ref, wk_ref, wv_ref, wf_ref, bf_ref, plq_ref, plk_ref,
                   oq_ref, ok_ref, q_ref, k_ref, v_ref, og_ref, mq_ref, aq_ref, ak_ref, carry_ref):
    tm = x_ref.shape[0]

    @pl.when(pl.program_id(1) == 0)
    def _():
        carry_ref[...] = jnp.zeros_like(carry_ref)

    xb = x_ref[...].astype(BF16)
    for w_ref, o_ref in ((wq_ref, q_ref), (wk_ref, k_ref), (wv_ref, v_ref), (wog_ref, og_ref), (wmq_ref, mq_ref)):
        o_ref[...] = _dot(xb, w_ref[...]).astype(o_ref.dtype)

    log_f = _log_sigmoid(_dot(xb, wf_ref[...]) + bf_ref[...])
    row = lax.broadcasted_iota(jnp.int32, (tm, tm), 0)
    col = lax.broadcasted_iota(jnp.int32, (tm, tm), 1)
    tri = jnp.where(col <= row, 1.0, 0.0).astype(BF16)
    f_hi, f_mid, f_lo = _split3(log_f)
    cum = carry_ref[...] + (_dot(tri, f_hi) + _dot(tri, f_mid) + _dot(tri, f_lo))
    carry_ref[...] = cum[tm - 1:tm, :]
    c3 = jnp.concatenate(_split3(cum * LOG2E), axis=1)
    aq_ref[...] = (_dot(c3, plq_ref[...]) + oq_ref[...]).astype(aq_ref.dtype)
    ak_ref[...] = (_dot(c3, plk_ref[...]) + ok_ref[...]).astype(ak_ref.dtype)


def _proj_b(x2d, wq, wog, wmq, wk, wv, wf, bf, batch, seq, tm):
    m = batch * seq
    nt = seq // tm
    row_map = lambda b, i: (b * nt + i, 0)
    const = lambda b, i: (0, 0)
    consts = (wq, wog, wmq, wk, wv, wf, bf) + _fox_placement()
    widths = (SELF_W, SELF_W, SELF_W, SELF_W, MEM_W, SELF_W, SELF_W)
    return pl.pallas_call(
        _proj_b_kernel,
        grid=(batch, nt),
        in_specs=[pl.BlockSpec((tm, D_MODEL), row_map)] + [pl.BlockSpec(a.shape, const) for a in consts],
        out_specs=[pl.BlockSpec((tm, n), row_map) for n in widths],
        out_shape=[jax.ShapeDtypeStruct((m, n), BF16) for n in widths],
        scratch_shapes=[pltpu.VMEM((1, LANE), F32)],
        compiler_params=_cparams("parallel", "arbitrary"),
        name="proj_b",
    )(x2d, *consts)


def _pair_rms_norm(t, gain):
    lo_half = lax.broadcasted_iota(jnp.int32, t.shape, 1) < FOX_HD
    sq = t * t
    ss_lo = jnp.sum(jnp.where(lo_half, sq, 0.0), axis=-1, keepdims=True)
    ss_hi = jnp.sum(jnp.where(lo_half, 0.0, sq), axis=-1, keepdims=True)
    ss = jnp.where(lo_half, ss_lo, ss_hi)
    return t * lax.rsqrt(ss * (1.0 / FOX_HD) + RMS_EPS) * gain


def _fox_kernel(q_ref, aq_ref, k_ref, ak_ref, v_ref, og_ref, qg_ref, kg_ref, o_ref,
                ka_ref, qa_ref, m_ref, l_ref, acc_ref):
    tq = q_ref.shape[0]
    seq = k_ref.shape[1]
    qi = pl.program_id(2)

    @pl.when(qi == 0)
    def _():
        def fill(c, carry):
            rows = pl.ds(pl.multiple_of(c * tq, tq), tq)
            kn = _pair_rms_norm(k_ref[0, rows, :].astype(F32), kg_ref[...])
            ka_ref[rows, :] = jnp.concatenate([kn.astype(BF16), ak_ref[0, rows, :]], axis=1)
            return carry
        lax.fori_loop(0, seq // tq, fill, 0)

    lane = lax.broadcasted_iota(jnp.int32, (tq, LANE), 1)
    qn = _pair_rms_norm(q_ref[...].astype(F32), qg_ref[...]) * ((FOX_HD ** -0.5) * LOG2E)
    aq = aq_ref[...]
    for hh in range(2):
        feat = (lane >= FOX_HD * hh) & (lane < FOX_HD * (hh + 1))
        bias = (lane >= FOX_AUG * hh) & (lane < FOX_AUG * (hh + 1))
        qa_ref[hh] = jnp.concatenate([jnp.where(feat, qn, 0.0).astype(BF16),
                                      jnp.where(bias, aq, jnp.zeros_like(aq))], axis=1)

    row = lax.broadcasted_iota(jnp.int32, (tq, tq), 0)
    col = lax.broadcasted_iota(jnp.int32, (tq, tq), 1)
    causal = col <= row

    m_ref[...] = jnp.full(m_ref.shape, NEG_BIG, F32)
    l_ref[...] = jnp.zeros(l_ref.shape, F32)
    acc_ref[...] = jnp.zeros(acc_ref.shape, F32)
    n_cb = tq // LANE

    def step(j, masked):
        rows = pl.ds(pl.multiple_of(j * tq, tq), tq)
        vt = v_ref[0, rows, :]
        kt = ka_ref[rows, :]
        for hh in range(2):
            s = _dot_nt(qa_ref[hh], kt)
            if masked:
                s = jnp.where(causal, s, NEG_BIG)
            s_cb = [s[:, LANE * c:LANE * (c + 1)] for c in range(n_cb)]
            m_old = m_ref[hh]
            row_max = jnp.max(functools.reduce(jnp.maximum, s_cb), axis=-1, keepdims=True)
            m_new = jnp.maximum(m_old, jnp.broadcast_to(row_max, (tq, LANE)))
            alpha = jnp.exp2(m_old - m_new)
            p_cb = [jnp.exp2(sc - m_new) for sc in s_cb]
            l_ref[hh] = alpha * l_ref[hh] + functools.reduce(lambda a, b: a + b, p_cb)
            p = jnp.concatenate([pc.astype(BF16) for pc in p_cb], axis=1)
            acc_ref[hh] = alpha * acc_ref[hh] + _dot(p, vt)
            m_ref[hh] = m_new

    def body(j, carry):
        step(j, False)
        return carry

    lax.fori_loop(0, qi, body, 0)
    step(qi, True)

    l0 = jnp.sum(l_ref[0], axis=-1, keepdims=True)
    l1 = jnp.sum(l_ref[1], axis=-1, keepdims=True)
    o = jnp.where(lane < FOX_HD, acc_ref[0] / l0, acc_ref[1] / l1)
    og = og_ref[...].astype(F32)
    o_ref[...] = (o * _sigmoid(og)).astype(o_ref.dtype)


def _fox(q, aq, k, ak, v, og, qg2, kg2, batch, seq, tq):
    nq = seq // tq
    m = batch * seq
    k3, ak3, v3 = (a.reshape(batch, seq, SELF_W) for a in (k, ak, v))
    tile_map = lambda b, p, i: (b * nq + i, p)
    seq_map = lambda b, p, i: (b, 0, p)
    const = lambda b, p, i: (0, 0)
    return pl.pallas_call(
        _fox_kernel,
        grid=(batch, FOX_PAIRS, nq),
        in_specs=[pl.BlockSpec((tq, LANE), tile_map),
                  pl.BlockSpec((tq, LANE), tile_map),
                  pl.BlockSpec((1, seq, LANE), seq_map),
                  pl.BlockSpec((1, seq, LANE), seq_map),
                  pl.BlockSpec((1, seq, LANE), seq_map),
                  pl.BlockSpec((tq, LANE), tile_map),
                  pl.BlockSpec((1, LANE), const),
                  pl.BlockSpec((1, LANE), const)],
        out_specs=pl.BlockSpec((tq, LANE), tile_map),
        out_shape=jax.ShapeDtypeStruct((m, SELF_W), BF16),
        scratch_shapes=[pltpu.VMEM((seq, 2 * LANE), BF16),
                        pltpu.VMEM((2, tq, 2 * LANE), BF16),
                        pltpu.VMEM((2, tq, LANE), F32),
                        pltpu.VMEM((2, tq, LANE), F32),
                        pltpu.VMEM((2, tq, LANE), F32)],
        compiler_params=_cparams("parallel", "parallel", "arbitrary"),
        name="fox",
    )(q, aq, k3, ak3, v3, og, qg2, kg2)


def _pad_heads(w, n_heads, d, d_pad):
    lead = w.shape[:-1]
    w = w.reshape(lead + (n_heads, d))
    w = jnp.pad(w, [(0, 0)] * len(lead) + [(0, 0), (0, d_pad - d)])
    return w.reshape(lead + (n_heads * d_pad,))


def _pad_last(w, n):
    return jnp.pad(w, [(0, 0)] * (w.ndim - 1) + [(0, n - w.shape[-1])])


def kernel(x, mem, w_in_a, w_gate_up_a, b_gate_a, gla_norm_g, w_in_b, q_norm_g, w_kv_shared, b_forget, k_norm_g, w_mem_kv, w_out, ln_mix_g, ln_mix_b, ln_ffn_g, ln_ffn_b, w_router, b_router, w_exp_gate, w_exp_up, w_exp_down):
    batch, seq, d = x.shape
    m = batch * seq
    tm = 512
    x2d = x.reshape(m, d)

    wa = w_in_a[0]
    s0, s1, s2, s3, s4 = (GLA_KDIM, 2 * GLA_KDIM, 2 * GLA_KDIM + SELF_W,
                          2 * GLA_KDIM + SELF_W + GLA_GATE_RANK, 2 * GLA_KDIM + 2 * SELF_W + GLA_GATE_RANK)
    w_a = jnp.concatenate([
        _pad_heads(wa[:, :s0], GLA_HEADS, GLA_DK, GLA_DK_PAD),
        _pad_heads(wa[:, s0:s1], GLA_HEADS, GLA_DK, GLA_DK_PAD),
        wa[:, s1:s2], wa[:, s3:s4], wa[:, s4:], _pad_last(wa[:, s2:s3], LANE)], axis=1).astype(BF16)
    wgu = _pad_heads(w_gate_up_a[0], GLA_HEADS, GLA_DK, GLA_DK_PAD)
    wgu = jnp.pad(wgu, ((0, LANE - GLA_GATE_RANK), (0, 0))).astype(BF16)
    bg = _pad_heads(b_gate_a[0], GLA_HEADS, GLA_DK, GLA_DK_PAD).reshape(1, GLA_KPAD)
    gn = jnp.tile(gla_norm_g[0], GLA_HEADS).reshape(1, SELF_W)

    wb = w_in_b[0]
    wq, wog, wmq = (wb[:, :SELF_W].astype(BF16), wb[:, SELF_W:2 * SELF_W].astype(BF16),
                    wb[:, 2 * SELF_W:].astype(BF16))
    wk, wv = w_kv_shared[:, :SELF_W].astype(BF16), w_kv_shared[:, SELF_W:2 * SELF_W].astype(BF16)
    wf = _pad_last(w_kv_shared[:, 2 * SELF_W:], LANE).astype(BF16)
    bf = _pad_last(b_forget, LANE).reshape(1, LANE)
    qg2 = jnp.tile(q_norm_g[0], 2).reshape(1, LANE)
    kg2 = jnp.tile(k_norm_g, 2).reshape(1, LANE)

    w_mkv = jnp.concatenate([w_mem_kv[l] for l in range(DEPTH)], axis=1).astype(BF16)
    wo = w_out.astype(BF16)
    wr_t = w_router.T
    wrh = wr_t.astype(BF16)
    wrl = (wr_t - wrh.astype(F32)).astype(BF16)
    br = b_router.reshape(N_EXPERTS, 1)
    row = lambda a: a.reshape(1, d)

    mkv = _mem_kv(mem.reshape(batch * N_MEM, d), w_mkv)

    tmx = 256
    n_tiles = m // tmx + N_CLASSES
    n_rows = n_tiles * tmx

    def tail(o, mq, xin, layer):
        x1e, idx, counts = _out_block(o, mq, mkv, layer, xin, wo[layer], row(ln_mix_g[layer]), row(ln_mix_b[layer]),
                                      wrh, wrl, br, seq, tm)
        plan = _moe_plan(counts, tmx, n_tiles)
        xs = _dispatch(plan, idx, x1e, tm, tmx, n_rows)
        ys = _moe_ffn(plan, xs, w_exp_gate[layer], w_exp_up[layer], w_exp_down[layer],
                      row(ln_ffn_g[layer]), row(ln_ffn_b[layer]), tmx, n_tiles)
        return _unpermute(plan, idx, ys, m, tm)

    q, k, v, r, mq, g = _proj_a(x2d, w_a, tm)
    o = _gla(q, k, v, r, g, wgu, bg, gn, batch, seq, tm)
    xa = tail(o, mq, x2d, 0)

    qb, kb, vb, og, mqb, aq, ak = _proj_b(xa, wq, wog, wmq, wk, wv, wf, bf, batch, seq, tm)
    ob = _fox(qb, aq, kb, ak, vb, og, qg2, kg2, batch, seq, tm)
    xb = tail(ob, mqb, xa, 1)
    return xb.reshape(batch, seq, d)
```

```python
import functools
import math

import jax
import jax.numpy as jnp
import numpy as np
from jax import lax
from jax.experimental import pallas as pl
from jax.experimental.pallas import tpu as pltpu

F32 = jnp.float32
BF16 = jnp.bfloat16

D_MODEL = 1024
DEPTH = 2
CHUNK = 64
N_MEM = 256
MEM_HEADS = 4
MEM_W = D_MODEL // 4
MEM_HD = MEM_W // MEM_HEADS
SELF_W = D_MODEL - MEM_W
GLA_HEADS = 4
GLA_KDIM = SELF_W // 2
GLA_DK = GLA_KDIM // GLA_HEADS
GLA_DV = SELF_W // GLA_HEADS
GLA_GATE_RANK = 16
GLA_TAU = 16.0
FOX_HD = 64
FOX_HEADS = SELF_W // FOX_HD
N_EXPERTS = 16
N_GROUPS = 4
EXPERTS_PER_GROUP = N_EXPERTS // N_GROUPS
D_EXPERT = D_MODEL // 2
DN_ALPHA = (2.0 * DEPTH) ** 0.25
LN_EPS = 1e-5
RMS_EPS = 1e-6

LANE = 128
GLA_DK_PAD = LANE
GLA_KPAD = GLA_HEADS * GLA_DK_PAD
GLA_VBLOCKS = SELF_W // LANE
GLA_HEAD_BLOCKS = ((0, 1), (1, 2), (3, 4), (4, 5))
FOX_PAIRS = FOX_HEADS // 2
LOG2E = math.log2(math.e)
NEG_BIG = -1e30
VMEM_LIMIT_BYTES = 48 * 1024 * 1024

NT_DIMS = (((1,), (1,)), ((), ()))
TN_DIMS = (((0,), (0,)), ((), ()))


def _cparams(*sem):
    return pltpu.CompilerParams(dimension_semantics=sem, vmem_limit_bytes=VMEM_LIMIT_BYTES)


def _dot(a, b):
    return jnp.dot(a, b, preferred_element_type=F32)


def _dot_nt(a, b):
    return lax.dot_general(a, b, NT_DIMS, preferred_element_type=F32)


def _dot_tn(a, b):
    return lax.dot_general(a, b, TN_DIMS, preferred_element_type=F32)


def _log_sigmoid(z):
    return -(jnp.maximum(-z, 0.0) + jnp.log1p(jnp.exp(-jnp.abs(z))))


def _sigmoid(z):
    return 1.0 / (1.0 + jnp.exp(-z))


def _split2(x):
    hi = x.astype(BF16)
    lo = (x - hi.astype(F32)).astype(BF16)
    return hi, lo


def _split3(x):
    hi = x.astype(BF16)
    r1 = x - hi.astype(F32)
    mid = r1.astype(BF16)
    lo = (r1 - mid.astype(F32)).astype(BF16)
    return hi, mid, lo


def _layer_norm(x, g, b):
    mu = jnp.mean(x, axis=-1, keepdims=True)
    xc = x - mu
    var = jnp.mean(xc * xc, axis=-1, keepdims=True)
    return xc * lax.rsqrt(var + LN_EPS) * g + b


def _proj_a_kernel(x_ref, w_ref, q_ref, k_ref, v_ref, r_ref, mq_ref, g_ref):
    xb = x_ref[...].astype(BF16)
    c = 0
    for ref in (q_ref, k_ref, v_ref, r_ref, mq_ref, g_ref):
        n = ref.shape[1]
        ref[...] = _dot(xb, w_ref[:, c:c + n]).astype(ref.dtype)
        c += n


def _proj_a(x2d, w, tm):
    m = x2d.shape[0]
    widths = (GLA_KPAD, GLA_KPAD, SELF_W, SELF_W, MEM_W, LANE)
    assert w.shape == (D_MODEL, sum(widths))
    return pl.pallas_call(
        _proj_a_kernel,
        grid=(m // tm,),
        in_specs=[pl.BlockSpec((tm, D_MODEL), lambda i: (i, 0)),
                  pl.BlockSpec(w.shape, lambda i: (0, 0))],
        out_specs=[pl.BlockSpec((tm, n), lambda i: (i, 0)) for n in widths],
        out_shape=[jax.ShapeDtypeStruct((m, n), BF16) for n in widths],
        compiler_params=_cparams("parallel"),
        name="proj_a",
    )(x2d, w)


def _gla_kernel(q_ref, k_ref, v_ref, r_ref, g_ref, wgu_ref, bg_ref, gn_ref, o_ref, st_ref, la_ref):
    tg = q_ref.shape[0]

    @pl.when(pl.program_id(1) == 0)
    def _():
        st_ref[...] = jnp.zeros_like(st_ref)

    z = _dot(g_ref[...], wgu_ref[...]) + bg_ref[...]
    la_ref[...] = _log_sigmoid(z) * (1.0 / GLA_TAU)

    row = lax.broadcasted_iota(jnp.int32, (CHUNK, CHUNK), 0)
    col = lax.broadcasted_iota(jnp.int32, (CHUNK, CHUNK), 1)
    causal = col <= row
    tri = jnp.where(causal, 1.0, 0.0).astype(BF16)
    lo_half = lax.broadcasted_iota(jnp.int32, (CHUNK, LANE), 1) < (LANE // 2)
    scale = GLA_DK ** -0.5

    def chunk(c, carry):
        r0 = pl.multiple_of(c * CHUNK, CHUNK)
        rows = pl.ds(r0, CHUNK)
        la_hi, la_lo = _split2(la_ref[rows, :])
        b = _dot(tri, la_hi) + _dot(tri, la_lo)
        bl = b[CHUNK - 1:CHUNK, :]
        qc = q_ref[rows, :].astype(F32) * scale
        kc = k_ref[rows, :].astype(F32)
        qd = (qc * jnp.exp(b)).astype(BF16)
        ki = (kc * jnp.exp(-b)).astype(BF16)
        ke = (kc * jnp.exp(bl - b)).astype(BF16)
        dec = jnp.exp(bl)
        unit = []
        for h in range(GLA_HEADS):
            sl = slice(GLA_DK_PAD * h, GLA_DK_PAD * (h + 1))
            qh, kih, keh = qd[:, sl], ki[:, sl], ke[:, sl]
            att = jnp.where(causal, _dot_nt(qh, kih), 0.0).astype(BF16)
            for t, blk in enumerate(GLA_HEAD_BLOCKS[h]):
                u = 2 * h + t
                vb = v_ref[rows, LANE * blk:LANE * (blk + 1)]
                st = st_ref[u]
                unit.append(_dot(att, vb) + _dot_nt(qh, st.astype(BF16)))
                st_ref[u] = st * dec[:, sl] + _dot_tn(vb, keh)
        o_blk = [unit[0], jnp.where(lo_half, unit[1], unit[2]), unit[3],
                 unit[4], jnp.where(lo_half, unit[5], unit[6]), unit[7]]
        sq = [o * o for o in o_blk]
        full = [jnp.sum(s, axis=-1, keepdims=True) for s in sq]
        lo1 = jnp.sum(jnp.where(lo_half, sq[1], 0.0), axis=-1, keepdims=True)
        hi1 = jnp.sum(jnp.where(lo_half, 0.0, sq[1]), axis=-1, keepdims=True)
        lo4 = jnp.sum(jnp.where(lo_half, sq[4], 0.0), axis=-1, keepdims=True)
        hi4 = jnp.sum(jnp.where(lo_half, 0.0, sq[4]), axis=-1, keepdims=True)
        ss = [full[0] + lo1, hi1 + full[2], full[3] + lo4, hi4 + full[5]]
        inv = [lax.rsqrt(s * (1.0 / GLA_DV) + RMS_EPS) for s in ss]
        inv_blk = [inv[0], jnp.where(lo_half, inv[0], inv[1]), inv[1],
                   inv[2], jnp.where(lo_half, inv[2], inv[3]), inv[3]]
        for blk in range(GLA_VBLOCKS):
            cs = slice(LANE * blk, LANE * (blk + 1))
            rg = r_ref[rows, cs].astype(F32)
            y = o_blk[blk] * inv_blk[blk] * gn_ref[:, cs]
            o_ref[rows, cs] = (y * (rg * _sigmoid(rg))).astype(o_ref.dtype)
        return carry

    lax.fori_loop(0, tg // CHUNK, chunk, 0)


def _gla(q, k, v, r, g, wgu, bg, gn, batch, seq, tg):
    m = batch * seq
    nt = seq // tg
    row_map = lambda b, i: (b * nt + i, 0)
    const = lambda b, i: (0, 0)
    return pl.pallas_call(
        _gla_kernel,
        grid=(batch, nt),
        in_specs=[pl.BlockSpec((tg, GLA_KPAD), row_map),
                  pl.BlockSpec((tg, GLA_KPAD), row_map),
                  pl.BlockSpec((tg, SELF_W), row_map),
                  pl.BlockSpec((tg, SELF_W), row_map),
                  pl.BlockSpec((tg, LANE), row_map),
                  pl.BlockSpec(wgu.shape, const),
                  pl.BlockSpec(bg.shape, const),
                  pl.BlockSpec(gn.shape, const)],
        out_specs=pl.BlockSpec((tg, SELF_W), row_map),
        out_shape=jax.ShapeDtypeStruct((m, SELF_W), BF16),
        scratch_shapes=[pltpu.VMEM((2 * GLA_HEADS, LANE, GLA_DK_PAD), F32),
                        pltpu.VMEM((tg, GLA_KPAD), F32)],
        compiler_params=_cparams("parallel", "arbitrary"),
        name="gla",
    )(q, k, v, r, g, wgu, bg, gn)


def _mem_kv_kernel(m_ref, w_ref, o_ref):
    o_ref[...] = _dot(m_ref[...].astype(BF16), w_ref[...]).astype(o_ref.dtype)


def _mem_kv(mem2d, w):
    m, n = mem2d.shape[0], w.shape[1]
    tm = N_MEM
    return pl.pallas_call(
        _mem_kv_kernel,
        grid=(m // tm,),
        in_specs=[pl.BlockSpec((tm, D_MODEL), lambda i: (i, 0)),
                  pl.BlockSpec(w.shape, lambda i: (0, 0))],
        out_specs=pl.BlockSpec((tm, n), lambda i: (i, 0)),
        out_shape=jax.ShapeDtypeStruct((m, n), BF16),
        compiler_params=_cparams("parallel"),
        name="mem_kv",
    )(mem2d, w)


def _top2_sum(a, b, c, d):
    p, q = jnp.maximum(a, b), jnp.minimum(a, b)
    r, s = jnp.maximum(c, d), jnp.minimum(c, d)
    return jnp.maximum(p, r) + jnp.maximum(jnp.minimum(p, r), jnp.maximum(q, s))


PAIRS = tuple((i, j) for i in range(EXPERTS_PER_GROUP) for j in range(i + 1, EXPERTS_PER_GROUP))
N_CLASSES = N_GROUPS * len(PAIRS)
CLS_PAD = 32
META_W = LANE
ROW_W = D_MODEL + META_W
CLASS_EA = np.array([EXPERTS_PER_GROUP * (c // len(PAIRS)) + PAIRS[c % len(PAIRS)][0] for c in range(N_CLASSES)], np.int32)
CLASS_EB = np.array([EXPERTS_PER_GROUP * (c // len(PAIRS)) + PAIRS[c % len(PAIRS)][1] for c in range(N_CLASSES)], np.int32)


def _router(aff, sel):
    score = [_top2_sum(*sel[EXPERTS_PER_GROUP * g:EXPERTS_PER_GROUP * (g + 1)]) for g in range(N_GROUPS)]
    picked = []
    for g in range(N_GROUPS):
        ok = None
        for i in range(N_GROUPS):
            if i == g:
                continue
            c = (score[g] > score[i]) if i < g else (score[g] >= score[i])
            ok = c if ok is None else (ok & c)
        picked.append(ok)
    chosen, w = [], []
    for g in range(N_GROUPS):
        for j in range(EXPERTS_PER_GROUP):
            ej = EXPERTS_PER_GROUP * g + j
            rank = None
            for i in range(EXPERTS_PER_GROUP):
                if i == j:
                    continue
                ei = EXPERTS_PER_GROUP * g + i
                beats = (sel[ei] >= sel[ej]) if i < j else (sel[ei] > sel[ej])
                beats = jnp.where(beats, 1.0, 0.0)
                rank = beats if rank is None else rank + beats
            chosen.append(picked[g] & (rank < 1.5))
            w.append(jnp.where(chosen[-1], aff[ej], 0.0))
    denom = functools.reduce(lambda a, b: a + b, w)
    gates = [wi / denom for wi in w]
    onehot, w_a, w_b = [], None, None
    for c in range(N_CLASSES):
        oc = chosen[CLASS_EA[c]] & chosen[CLASS_EB[c]]
        onehot.append(oc)
        ga, gb = jnp.where(oc, gates[CLASS_EA[c]], 0.0), jnp.where(oc, gates[CLASS_EB[c]], 0.0)
        w_a = ga if w_a is None else w_a + ga
        w_b = gb if w_b is None else w_b + gb
    return onehot, w_a, w_b


def _out_kernel(o_ref, mq_ref, mk_ref, mv_ref, x_ref, wo_ref, lng_ref, lnb_ref,
                wrh_ref, wrl_ref, br_ref, x1_ref, idx_ref, cnt_ref, carry_ref):
    tm = o_ref.shape[0]

    @pl.when(pl.program_id(0) == 0)
    def _():
        carry_ref[...] = jnp.zeros_like(carry_ref)

    mq = mq_ref[...]
    mk = mk_ref[...]
    mv = mv_ref[...]
    lane = lax.broadcasted_iota(jnp.int32, (N_MEM, MEM_W), 1)
    m_out = jnp.zeros((tm, MEM_W), F32)
    for h in range(MEM_HEADS):
        in_head = (lane >= MEM_HD * h) & (lane < MEM_HD * (h + 1))
        s = _dot_nt(mq, jnp.where(in_head, mk, jnp.zeros_like(mk))) * (MEM_HD ** -0.5)
        p = jnp.exp(s - jnp.max(s, axis=-1, keepdims=True))
        l = jnp.sum(p, axis=-1, keepdims=True)
        m_out = m_out + _dot(p.astype(BF16), jnp.where(in_head, mv, jnp.zeros_like(mv))) / l
    y = _dot(o_ref[...], wo_ref[:SELF_W, :]) + _dot(m_out.astype(BF16), wo_ref[SELF_W:, :])
    x1 = _layer_norm(DN_ALPHA * x_ref[...] + y, lng_ref[...], lnb_ref[...])
    x1_ref[:, :D_MODEL] = x1

    xh, xl = _split2(x1)
    wrh = wrh_ref[...]
    logits = _dot_nt(wrh, xh) + _dot_nt(wrh, xl) + _dot_nt(wrl_ref[...], xh)
    aff = _sigmoid(logits)
    sel = aff + br_ref[...]
    onehot, w_a, w_b = _router([aff[e:e + 1, :] for e in range(N_EXPERTS)],
                               [sel[e:e + 1, :] for e in range(N_EXPERTS)])

    sub = lax.broadcasted_iota(jnp.int32, (META_W, tm), 0)
    meta = jnp.where(sub == 0, w_a, jnp.where(sub == 1, w_b, 0.0))
    x1_ref[:, D_MODEL:] = meta.T

    csub = lax.broadcasted_iota(jnp.int32, (CLS_PAD, tm), 0)
    oh = jnp.zeros((CLS_PAD, tm), F32)
    for c in range(N_CLASSES):
        oh = jnp.where((csub == c) & onehot[c], 1.0, oh)
    row = lax.broadcasted_iota(jnp.int32, (tm, tm), 0)
    col = lax.broadcasted_iota(jnp.int32, (tm, tm), 1)
    tri_u = jnp.where(row <= col, 1.0, 0.0).astype(BF16)
    incl = _dot(oh.astype(BF16), tri_u)
    carry = carry_ref[...]
    rank = jnp.sum(oh * (carry + incl - 1.0), axis=0, keepdims=True)
    cls = jnp.sum(oh * csub.astype(F32), axis=0, keepdims=True)
    idx_ref[0, 0:1, :] = cls.astype(jnp.int32)
    idx_ref[0, 1:2, :] = rank.astype(jnp.int32)
    carry = carry + jnp.sum(oh, axis=1, keepdims=True)
    carry_ref[...] = carry
    cnt_ref[...] = carry[:, :LANE]


def _out_block(o, mq, mkv, layer, x2d, wo, lng, lnb, wrh, wrl, br, seq, tm):
    m = x2d.shape[0]
    nt = seq // tm
    row_map = lambda i: (i, 0)
    const = lambda i: (0, 0)
    return pl.pallas_call(
        _out_kernel,
        grid=(m // tm,),
        in_specs=[pl.BlockSpec((tm, SELF_W), row_map),
                  pl.BlockSpec((tm, MEM_W), row_map),
                  pl.BlockSpec((N_MEM, MEM_W), lambda i: (i // nt, 2 * layer)),
                  pl.BlockSpec((N_MEM, MEM_W), lambda i: (i // nt, 2 * layer + 1)),
                  pl.BlockSpec((tm, D_MODEL), row_map),
                  pl.BlockSpec(wo.shape, const),
                  pl.BlockSpec(lng.shape, const),
                  pl.BlockSpec(lnb.shape, const),
                  pl.BlockSpec(wrh.shape, const),
                  pl.BlockSpec(wrl.shape, const),
                  pl.BlockSpec(br.shape, const)],
        out_specs=[pl.BlockSpec((tm, ROW_W), row_map),
                   pl.BlockSpec((1, 2, tm), lambda i: (i, 0, 0)),
                   pl.BlockSpec((CLS_PAD, LANE), const)],
        out_shape=[jax.ShapeDtypeStruct((m, ROW_W), F32),
                   jax.ShapeDtypeStruct((m // tm, 2, tm), jnp.int32),
                   jax.ShapeDtypeStruct((CLS_PAD, LANE), F32)],
        scratch_shapes=[pltpu.VMEM((CLS_PAD, tm), F32)],
        compiler_params=_cparams("arbitrary"),
        name="out_ln_router",
    )(o, mq, mkv, mkv, x2d, wo, lng, lnb, wrh, wrl, br)


def _moe_plan(counts, tmx, n_tiles):
    cnt = counts[:N_CLASSES, 0].astype(jnp.int32)
    nt = (cnt + (tmx - 1)) // tmx
    tend = jnp.cumsum(nt)
    offs = (tend - nt) * tmx
    n_used = tend[-1:]
    i = jnp.arange(n_tiles, dtype=jnp.int32)
    ieff = jnp.minimum(i, n_used - 1)
    cls = jnp.sum((ieff[:, None] >= tend[None, :]).astype(jnp.int32), axis=1)
    fresh = jnp.concatenate([jnp.ones((1,), jnp.int32), (cls[1:] != cls[:-1]).astype(jnp.int32)])
    last_tile_row = jnp.where(nt > 0, (tend - 1) * tmx, -1)
    spare = n_used + jnp.arange(N_CLASSES, dtype=jnp.int32)
    spare_row = jnp.where(spare < n_tiles, spare * tmx, -1)
    pad = lambda a: jnp.pad(a, (0, CLS_PAD - N_CLASSES))
    return dict(offs=pad(offs), zero_rows=jnp.concatenate([last_tile_row, spare_row]), row_block=ieff,
                ea=jnp.asarray(CLASS_EA)[cls], eb=jnp.asarray(CLASS_EB)[cls], fresh=fresh, n_used=n_used)


ROW_DMA_UNROLL = 16


def _tile_dest(offs_ref, idx_ref, dest_v, dest_s, sem):
    cls = idx_ref[0, 0:1, :]
    dest = idx_ref[0, 1:2, :]
    for c in range(N_CLASSES):
        dest = dest + jnp.where(cls == c, offs_ref[c], 0)
    dest_v[...] = dest
    cp = pltpu.make_async_copy(dest_v, dest_s, sem)
    cp.start()
    cp.wait()


def _issue_row_copies(n_rows, make_copy):
    def group(g, carry):
        base = pl.multiple_of(g * ROW_DMA_UNROLL, ROW_DMA_UNROLL)
        for k in range(ROW_DMA_UNROLL):
            make_copy(base + k).start(priority=k % 2)
        return carry
    lax.fori_loop(0, n_rows // ROW_DMA_UNROLL, group, 0)


def _dispatch_kernel(offs_ref, zrow_ref, idx_ref, x_ref, xs_ref, dest_v, dest_s, zero_ref, sem_i, sem_z, sem):
    tm = x_ref.shape[0]
    tmx = zero_ref.shape[0]

    @pl.when(pl.program_id(0) == 0)
    def _():
        zero_ref[...] = jnp.zeros_like(zero_ref)

        def zcopy(c):
            return pltpu.make_async_copy(zero_ref, xs_ref.at[pl.ds(pl.multiple_of(zrow_ref[c], tmx), tmx)], sem_z)

        for c in range(2 * N_CLASSES):
            @pl.when(zrow_ref[c] >= 0)
            def _(c=c):
                zcopy(c).start()
        for c in range(2 * N_CLASSES):
            @pl.when(zrow_ref[c] >= 0)
            def _(c=c):
                zcopy(c).wait()

    _tile_dest(offs_ref, idx_ref, dest_v, dest_s, sem_i)
    _issue_row_copies(tm, lambda r: pltpu.make_async_copy(
        x_ref.at[pl.ds(r, 1)], xs_ref.at[pl.ds(dest_s[0, r], 1)], sem))
    pltpu.make_async_copy(x_ref, xs_ref.at[pl.ds(0, tm)], sem).wait()


def _dispatch(plan, idx, x1e, tm, tmx, n_rows):
    m = x1e.shape[0]
    return pl.pallas_call(
        _dispatch_kernel,
        grid_spec=pltpu.PrefetchScalarGridSpec(
            num_scalar_prefetch=2,
            grid=(m // tm,),
            in_specs=[pl.BlockSpec((1, 2, tm), lambda i, *_: (i, 0, 0)),
                      pl.BlockSpec((tm, ROW_W), lambda i, *_: (i, 0))],
            out_specs=pl.BlockSpec(memory_space=pl.ANY),
            scratch_shapes=[pltpu.VMEM((1, tm), jnp.int32),
                            pltpu.SMEM((1, tm), jnp.int32),
                            pltpu.VMEM((tmx, ROW_W), F32),
                            pltpu.SemaphoreType.DMA, pltpu.SemaphoreType.DMA, pltpu.SemaphoreType.DMA]),
        out_shape=jax.ShapeDtypeStruct((n_rows, ROW_W), F32),
        compiler_params=_cparams("arbitrary"),
        name="moe_dispatch",
    )(plan["offs"], plan["zero_rows"], idx, x1e)


def _moe_ffn_kernel(rb_ref, ea_ref, eb_ref, fresh_ref, nu_ref,
                    xs_ref, wga_ref, wua_ref, wda_ref, wgb_ref, wub_ref, wdb_ref, lng_ref, lnb_ref,
                    ys_ref, sg_ref, su_ref, sd_ref):
    i = pl.program_id(0)

    @pl.when(i >= nu_ref[0])
    def _():
        ys_ref[...] = jnp.zeros_like(ys_ref)

    @pl.when(i < nu_ref[0])
    def _():
        @pl.when(fresh_ref[i] == 1)
        def _():
            for s, (g, u, d) in enumerate(((wga_ref, wua_ref, wda_ref), (wgb_ref, wub_ref, wdb_ref))):
                sg_ref[s] = g[0, 0].astype(BF16)
                su_ref[s] = u[0, 0].astype(BF16)
                sd_ref[s] = d[0, 0].astype(BF16)

        x = xs_ref[:, :D_MODEL]
        xb = x.astype(BF16)
        f = None
        for s in range(2):
            hg = _dot(xb, sg_ref[s])
            hu = _dot(xb, su_ref[s])
            h = (hg * _sigmoid(hg) * hu).astype(BF16)
            fs = xs_ref[:, D_MODEL + s:D_MODEL + s + 1] * _dot(h, sd_ref[s])
            f = fs if f is None else f + fs
        ys_ref[...] = _layer_norm(DN_ALPHA * x + f, lng_ref[...], lnb_ref[...])


def _moe_ffn(plan, xs, layer, wg, wu, wd, lng, lnb, tmx, n_tiles):
    n_rows = xs.shape[0]
    rows = lambda i, rb, *_: (rb[i], 0)
    wa = lambda i, rb, ea, eb, *_: (layer, ea[i], 0, 0)
    wb = lambda i, rb, ea, eb, *_: (layer, eb[i], 0, 0)
    const = lambda i, *_: (0, 0)
    gu, dn = (1, 1, D_MODEL, D_EXPERT), (1, 1, D_EXPERT, D_MODEL)
    return pl.pallas_call(
        _moe_ffn_kernel,
        grid_spec=pltpu.PrefetchScalarGridSpec(
            num_scalar_prefetch=5,
            grid=(n_tiles,),
            in_specs=[pl.BlockSpec((tmx, ROW_W), rows),
                      pl.BlockSpec(gu, wa), pl.BlockSpec(gu, wa), pl.BlockSpec(dn, wa),
                      pl.BlockSpec(gu, wb), pl.BlockSpec(gu, wb), pl.BlockSpec(dn, wb),
                      pl.BlockSpec(lng.shape, const), pl.BlockSpec(lnb.shape, const)],
            out_specs=pl.BlockSpec((tmx, D_MODEL), lambda i, *_: (i, 0)),
            scratch_shapes=[pltpu.VMEM((2, D_MODEL, D_EXPERT), BF16),
                            pltpu.VMEM((2, D_MODEL, D_EXPERT), BF16),
                            pltpu.VMEM((2, D_EXPERT, D_MODEL), BF16)]),
        out_shape=jax.ShapeDtypeStruct((n_rows, D_MODEL), F32),
        compiler_params=_cparams("arbitrary"),
        name="moe_ffn",
    )(plan["row_block"], plan["ea"], plan["eb"], plan["fresh"], plan["n_used"],
      xs, wg, wu, wd, wg, wu, wd, lng, lnb)


def _unpermute_kernel(offs_ref, idx_ref, ys_ref, o_ref, dest_v, dest_s, sem_i, sem):
    tm = o_ref.shape[0]
    _tile_dest(offs_ref, idx_ref, dest_v, dest_s, sem_i)
    _issue_row_copies(tm, lambda r: pltpu.make_async_copy(
        ys_ref.at[pl.ds(dest_s[0, r], 1)], o_ref.at[pl.ds(r, 1)], sem))
    pltpu.make_async_copy(ys_ref.at[pl.ds(0, tm)], o_ref, sem).wait()


def _unpermute(plan, idx, ys, m, tm):
    return pl.pallas_call(
        _unpermute_kernel,
        grid_spec=pltpu.PrefetchScalarGridSpec(
            num_scalar_prefetch=1,
            grid=(m // tm,),
            in_specs=[pl.BlockSpec((1, 2, tm), lambda i, *_: (i, 0, 0)),
                      pl.BlockSpec(memory_space=pl.ANY)],
            out_specs=pl.BlockSpec((tm, D_MODEL), lambda i, *_: (i, 0)),
            scratch_shapes=[pltpu.VMEM((1, tm), jnp.int32),
                            pltpu.SMEM((1, tm), jnp.int32),
                            pltpu.SemaphoreType.DMA, pltpu.SemaphoreType.DMA]),
        out_shape=jax.ShapeDtypeStruct((m, D_MODEL), F32),
        compiler_params=_cparams("arbitrary"),
        name="moe_unpermute",
    )(plan["offs"], idx, ys)


FOX_AUG = 6


def _fox_placement():
    pl_q = np.zeros((3 * LANE, SELF_W), np.float32)
    pl_k = np.zeros((3 * LANE, SELF_W), np.float32)
    ones_q = np.zeros((1, SELF_W), np.float32)
    ones_k = np.zeros((1, SELF_W), np.float32)
    for h in range(FOX_HEADS):
        base = LANE * (h // 2) + FOX_AUG * (h % 2)
        for piece in range(3):
            pl_q[piece * LANE + h, base + piece] = 1.0
            pl_k[piece * LANE + h, base + 3 + piece] = -1.0
            ones_q[0, base + 3 + piece] = 1.0
            ones_k[0, base + piece] = 1.0
    return (jnp.asarray(pl_q, BF16), jnp.asarray(pl_k, BF16), jnp.asarray(ones_q), jnp.asarray(ones_k))


def _proj_b_kernel(x_ref, wq_ref, wog_ref, wmq_ref, wk_ref, wv_ref, wf_ref, bf_ref, plq_ref, plk_ref,
                   oq_ref, ok_ref, q_ref, k_ref, v_ref, og_ref, mq_ref, aq_ref, ak_ref, carry_ref):
    tm = x_ref.shape[0]

    @pl.when(pl.program_id(1) == 0)
    def _():
        carry_ref[...] = jnp.zeros_like(carry_ref)

    xb = x_ref[...].astype(BF16)
    for w_ref, o_ref in ((wq_ref, q_ref), (wk_ref, k_ref), (wv_ref, v_ref), (wog_ref, og_ref), (wmq_ref, mq_ref)):
        o_ref[...] = _dot(xb, w_ref[...]).astype(o_ref.dtype)

    log_f = _log_sigmoid(_dot(xb, wf_ref[...]) + bf_ref[...])
    row = lax.broadcasted_iota(jnp.int32, (tm, tm), 0)
    col = lax.broadcasted_iota(jnp.int32, (tm, tm), 1)
    tri = jnp.where(col <= row, 1.0, 0.0).astype(BF16)
    f_hi, f_mid, f_lo = _split3(log_f)
    cum = carry_ref[...] + (_dot(tri, f_hi) + _dot(tri, f_mid) + _dot(tri, f_lo))
    carry_ref[...] = cum[tm - 1:tm, :]
    c3 = jnp.concatenate(_split3(cum * LOG2E), axis=1)
    aq_ref[...] = (_dot(c3, plq_ref[...]) + oq_ref[...]).astype(aq_ref.dtype)
    ak_ref[...] = (_dot(c3, plk_ref[...]) + ok_ref[...]).astype(ak_ref.dtype)


def _proj_b(x2d, wq, wog, wmq, wk, wv, wf, bf, batch, seq, tm):
    m = batch * seq
    nt = seq // tm
    row_map = lambda b, i: (b * nt + i, 0)
    const = lambda b, i: (0, 0)
    consts = (wq, wog, wmq, wk, wv, wf, bf) + _fox_placement()
    widths = (SELF_W, SELF_W, SELF_W, SELF_W, MEM_W, SELF_W, SELF_W)
    return pl.pallas_call(
        _proj_b_kernel,
        grid=(batch, nt),
        in_specs=[pl.BlockSpec((tm, D_MODEL), row_map)] + [pl.BlockSpec(a.shape, const) for a in consts],
        out_specs=[pl.BlockSpec((tm, n), row_map) for n in widths],
        out_shape=[jax.ShapeDtypeStruct((m, n), BF16) for n in widths],
        scratch_shapes=[pltpu.VMEM((1, LANE), F32)],
        compiler_params=_cparams("parallel", "arbitrary"),
        name="proj_b",
    )(x2d, *consts)


def _pair_rms_norm(t, gain):
    lo_half = lax.broadcasted_iota(jnp.int32, t.shape, 1) < FOX_HD
    sq = t * t
    ss_lo = jnp.sum(jnp.where(lo_half, sq, 0.0), axis=-1, keepdims=True)
    ss_hi = jnp.sum(jnp.where(lo_half, 0.0, sq), axis=-1, keepdims=True)
    ss = jnp.where(lo_half, ss_lo, ss_hi)
    return t * lax.rsqrt(ss * (1.0 / FOX_HD) + RMS_EPS) * gain


def _fox_kernel(q_ref, aq_ref, k_ref, ak_ref, v_ref, og_ref, qg_ref, kg_ref, o_ref,
                ka_ref, qa_ref, m_ref, l_ref, acc_ref, sa_ref, sb_ref):
    tq = q_ref.shape[0]
    seq = k_ref.shape[1]
    qi = pl.program_id(2)

    @pl.when(qi == 0)
    def _():
        def fill(c, carry):
            rows = pl.ds(pl.multiple_of(c * tq, tq), tq)
            kn = _pair_rms_norm(k_ref[0, rows, :].astype(F32), kg_ref[...])
            ka_ref[rows, :] = jnp.concatenate([kn.astype(BF16), ak_ref[0, rows, :]], axis=1)
            return carry
        lax.fori_loop(0, seq // tq, fill, 0)

    lane = lax.broadcasted_iota(jnp.int32, (tq, LANE), 1)
    qn = _pair_rms_norm(q_ref[...].astype(F32), qg_ref[...]) * ((FOX_HD ** -0.5) * LOG2E)
    aq = aq_ref[...]
    for hh in range(2):
        feat = (lane >= FOX_HD * hh) & (lane < FOX_HD * (hh + 1))
        bias = (lane >= FOX_AUG * hh) & (lane < FOX_AUG * (hh + 1))
        qa_ref[hh] = jnp.concatenate([jnp.where(feat, qn, 0.0).astype(BF16),
                                      jnp.where(bias, aq, jnp.zeros_like(aq))], axis=1)

    row = lax.broadcasted_iota(jnp.int32, (tq, tq), 0)
    col = lax.broadcasted_iota(jnp.int32, (tq, tq), 1)
    causal = col <= row

    m_ref[...] = jnp.full(m_ref.shape, NEG_BIG, F32)
    l_ref[...] = jnp.zeros(l_ref.shape, F32)
    acc_ref[...] = jnp.zeros(acc_ref.shape, F32)
    n_cb = tq // LANE

    def scores(j, s_ref):
        kt = ka_ref[pl.ds(pl.multiple_of(j * tq, tq), tq), :]
        for hh in range(2):
            s_ref[hh] = _dot_nt(qa_ref[hh], kt)

    def consume(j, s_ref, masked):
        vt = v_ref[0, pl.ds(pl.multiple_of(j * tq, tq), tq), :]
        for hh in range(2):
            s = s_ref[hh]
            if masked:
                s = jnp.where(causal, s, NEG_BIG)
            s_cb = [s[:, LANE * c:LANE * (c + 1)] for c in range(n_cb)]
            m_old = m_ref[hh]
            row_max = jnp.max(functools.reduce(jnp.maximum, s_cb), axis=-1, keepdims=True)
            m_new = jnp.maximum(m_old, jnp.broadcast_to(row_max, (tq, LANE)))
            alpha = jnp.exp2(m_old - m_new)
            p_cb = [jnp.exp2(sc - m_new) for sc in s_cb]
            l_ref[hh] = alpha * l_ref[hh] + functools.reduce(lambda a, b: a + b, p_cb)
            p = jnp.concatenate([pc.astype(BF16) for pc in p_cb], axis=1)
            acc_ref[hh] = alpha * acc_ref[hh] + _dot(p, vt)
            m_ref[hh] = m_new

    scores(0, sa_ref)

    def pair(t, carry):
        j = 2 * t
        scores(j + 1, sb_ref)
        consume(j, sa_ref, False)
        scores(j + 2, sa_ref)
        consume(j + 1, sb_ref, False)
        return carry

    lax.fori_loop(0, qi // 2, pair, 0)
    odd = (qi % 2) == 1

    @pl.when(odd)
    def _():
        scores(qi, sb_ref)
        consume(qi - 1, sa_ref, False)
        consume(qi, sb_ref, True)

    @pl.when(jnp.logical_not(odd))
    def _():
        consume(qi, sa_ref, True)

    l0 = jnp.sum(l_ref[0], axis=-1, keepdims=True)
    l1 = jnp.sum(l_ref[1], axis=-1, keepdims=True)
    o = jnp.where(lane < FOX_HD, acc_ref[0] / l0, acc_ref[1] / l1)
    og = og_ref[...].astype(F32)
    o_ref[...] = (o * _sigmoid(og)).astype(o_ref.dtype)


def _fox(q, aq, k, ak, v, og, qg2, kg2, batch, seq, tq):
    nq = seq // tq
    m = batch * seq
    k3, ak3, v3 = (a.reshape(batch, seq, SELF_W) for a in (k, ak, v))
    tile_map = lambda b, p, i: (b * nq + i, p)
    seq_map = lambda b, p, i: (b, 0, p)
    const = lambda b, p, i: (0, 0)
    return pl.pallas_call(
        _fox_kernel,
        grid=(batch, FOX_PAIRS, nq),
        in_specs=[pl.BlockSpec((tq, LANE), tile_map),
                  pl.BlockSpec((tq, LANE), tile_map),
                  pl.BlockSpec((1, seq, LANE), seq_map),
                  pl.BlockSpec((1, seq, LANE), seq_map),
                  pl.BlockSpec((1, seq, LANE), seq_map),
                  pl.BlockSpec((tq, LANE), tile_map),
                  pl.BlockSpec((1, LANE), const),
                  pl.BlockSpec((1, LANE), const)],
        out_specs=pl.BlockSpec((tq, LANE), tile_map),
        out_shape=jax.ShapeDtypeStruct((m, SELF_W), BF16),
        scratch_shapes=[pltpu.VMEM((seq, 2 * LANE), BF16),
                        pltpu.VMEM((2, tq, 2 * LANE), BF16),
                        pltpu.VMEM((2, tq, LANE), F32),
                        pltpu.VMEM((2, tq, LANE), F32),
                        pltpu.VMEM((2, tq, LANE), F32),
                        pltpu.VMEM((2, tq, tq), F32),
                        pltpu.VMEM((2, tq, tq), F32)],
        compiler_params=_cparams("parallel", "parallel", "arbitrary"),
        name="fox",
    )(q, aq, k3, ak3, v3, og, qg2, kg2)


def _pad_heads(w, n_heads, d, d_pad):
    lead = w.shape[:-1]
    w = w.reshape(lead + (n_heads, d))
    w = jnp.pad(w, [(0, 0)] * len(lead) + [(0, 0), (0, d_pad - d)])
    return w.reshape(lead + (n_heads * d_pad,))


def _pad_last(w, n):
    return jnp.pad(w, [(0, 0)] * (w.ndim - 1) + [(0, n - w.shape[-1])])


def kernel(x, mem, w_in_a, w_gate_up_a, b_gate_a, gla_norm_g, w_in_b, q_norm_g, w_kv_shared, b_forget, k_norm_g, w_mem_kv, w_out, ln_mix_g, ln_mix_b, ln_ffn_g, ln_ffn_b, w_router, b_router, w_exp_gate, w_exp_up, w_exp_down):
    batch, seq, d = x.shape
    m = batch * seq
    tm = 512
    x2d = x.reshape(m, d)

    wa = w_in_a[0]
    s0, s1, s2, s3, s4 = (GLA_KDIM, 2 * GLA_KDIM, 2 * GLA_KDIM + SELF_W,
                          2 * GLA_KDIM + SELF_W + GLA_GATE_RANK, 2 * GLA_KDIM + 2 * SELF_W + GLA_GATE_RANK)
    w_a = jnp.concatenate([
        _pad_heads(wa[:, :s0], GLA_HEADS, GLA_DK, GLA_DK_PAD),
        _pad_heads(wa[:, s0:s1], GLA_HEADS, GLA_DK, GLA_DK_PAD),
        wa[:, s1:s2], wa[:, s3:s4], wa[:, s4:], _pad_last(wa[:, s2:s3], LANE)], axis=1).astype(BF16)
    wgu = _pad_heads(w_gate_up_a[0], GLA_HEADS, GLA_DK, GLA_DK_PAD)
    wgu = jnp.pad(wgu, ((0, LANE - GLA_GATE_RANK), (0, 0))).astype(BF16)
    bg = _pad_heads(b_gate_a[0], GLA_HEADS, GLA_DK, GLA_DK_PAD).reshape(1, GLA_KPAD)
    gn = jnp.tile(gla_norm_g[0], GLA_HEADS).reshape(1, SELF_W)

    wb = w_in_b[0]
    wq, wog, wmq = (wb[:, :SELF_W].astype(BF16), wb[:, SELF_W:2 * SELF_W].astype(BF16),
                    wb[:, 2 * SELF_W:].astype(BF16))
    wk, wv = w_kv_shared[:, :SELF_W].astype(BF16), w_kv_shared[:, SELF_W:2 * SELF_W].astype(BF16)
    wf = _pad_last(w_kv_shared[:, 2 * SELF_W:], LANE).astype(BF16)
    bf = _pad_last(b_forget, LANE).reshape(1, LANE)
    qg2 = jnp.tile(q_norm_g[0], 2).reshape(1, LANE)
    kg2 = jnp.tile(k_norm_g, 2).reshape(1, LANE)

    w_mkv = jnp.concatenate([w_mem_kv[l] for l in range(DEPTH)], axis=1).astype(BF16)
    wo = w_out.astype(BF16)
    wr_t = w_router.T
    wrh = wr_t.astype(BF16)
    wrl = (wr_t - wrh.astype(F32)).astype(BF16)
    br = b_router.reshape(N_EXPERTS, 1)
    row = lambda a: a.reshape(1, d)

    mkv = _mem_kv(mem.reshape(batch * N_MEM, d), w_mkv)

    tmx = 256
    n_tiles = m // tmx + N_CLASSES
    n_rows = n_tiles * tmx

    def tail(o, mq, xin, layer):
        x1e, idx, counts = _out_block(o, mq, mkv, layer, xin, wo[layer], row(ln_mix_g[layer]), row(ln_mix_b[layer]),
                                      wrh, wrl, br, seq, tm)
        plan = _moe_plan(counts, tmx, n_tiles)
        xs = _dispatch(plan, idx, x1e, tm, tmx, n_rows)
        ys = _moe_ffn(plan, xs, layer, w_exp_gate, w_exp_up, w_exp_down,
                      row(ln_ffn_g[layer]), row(ln_ffn_b[layer]), tmx, n_tiles)
        return _unpermute(plan, idx, ys, m, tm)

    q, k, v, r, mq, g = _proj_a(x2d, w_a, tm)
    o = _gla(q, k, v, r, g, wgu, bg, gn, batch, seq, tm)
    xa = tail(o, mq, x2d, 0)

    qb, kb, vb, og, mqb, aq, ak = _proj_b(xa, wq, wog, wmq, wk, wv, wf, bf, batch, seq, tm)
    ob = _fox(qb, aq, kb, ak, vb, og, qg2, kg2, batch, seq, tm)
    xb = tail(ob, mqb, xa, 1)
    return xb.reshape(batch, seq, d)
```

```python
import functools
import math

import jax
import jax.numpy as jnp
import numpy as np
from jax import lax
from jax.experimental import pallas as pl
from jax.experimental.pallas import tpu as pltpu

F32 = jnp.float32
BF16 = jnp.bfloat16

D_MODEL = 1024
DEPTH = 2
CHUNK = 64
N_MEM = 256
MEM_HEADS = 4
MEM_W = D_MODEL // 4
MEM_HD = MEM_W // MEM_HEADS
SELF_W = D_MODEL - MEM_W
GLA_HEADS = 4
GLA_KDIM = SELF_W // 2
GLA_DK = GLA_KDIM // GLA_HEADS
GLA_DV = SELF_W // GLA_HEADS
GLA_GATE_RANK = 16
GLA_TAU = 16.0
FOX_HD = 64
FOX_HEADS = SELF_W // FOX_HD
N_EXPERTS = 16
N_GROUPS = 4
EXPERTS_PER_GROUP = N_EXPERTS // N_GROUPS
D_EXPERT = D_MODEL // 2
DN_ALPHA = (2.0 * DEPTH) ** 0.25
LN_EPS = 1e-5
RMS_EPS = 1e-6

LANE = 128
GLA_DK_PAD = LANE
GLA_KPAD = GLA_HEADS * GLA_DK_PAD
GLA_VBLOCKS = SELF_W // LANE
GLA_HEAD_BLOCKS = ((0, 1), (1, 2), (3, 4), (4, 5))
FOX_PAIRS = FOX_HEADS // 2
LOG2E = math.log2(math.e)
NEG_BIG = -1e30
VMEM_LIMIT_BYTES = 48 * 1024 * 1024

NT_DIMS = (((1,), (1,)), ((), ()))
TN_DIMS = (((0,), (0,)), ((), ()))


def _cparams(*sem):
    return pltpu.CompilerParams(dimension_semantics=sem, vmem_limit_bytes=VMEM_LIMIT_BYTES)


def _dot(a, b):
    return jnp.dot(a, b, preferred_element_type=F32)


def _dot_nt(a, b):
    return lax.dot_general(a, b, NT_DIMS, preferred_element_type=F32)


def _dot_tn(a, b):
    return lax.dot_general(a, b, TN_DIMS, preferred_element_type=F32)


def _log_sigmoid(z):
    return jnp.minimum(z, 0.0) - jnp.log(1.0 + jnp.exp(-jnp.abs(z)))


def _sigmoid(z):
    return 1.0 / (1.0 + jnp.exp(-z))


def _split2(x):
    hi = x.astype(BF16)
    lo = (x - hi.astype(F32)).astype(BF16)
    return hi, lo


def _split3(x):
    hi = x.astype(BF16)
    r1 = x - hi.astype(F32)
    mid = r1.astype(BF16)
    lo = (r1 - mid.astype(F32)).astype(BF16)
    return hi, mid, lo


def _layer_norm(x, g, b):
    mu = jnp.mean(x, axis=-1, keepdims=True)
    xc = x - mu
    var = jnp.mean(xc * xc, axis=-1, keepdims=True)
    return xc * lax.rsqrt(var + LN_EPS) * g + b


def _proj_a_kernel(x_ref, w_ref, q_ref, k_ref, v_ref, r_ref, mq_ref, g_ref):
    xb = x_ref[...].astype(BF16)
    c = 0
    for ref in (q_ref, k_ref, v_ref, r_ref, mq_ref, g_ref):
        n = ref.shape[1]
        ref[...] = _dot(xb, w_ref[:, c:c + n]).astype(ref.dtype)
        c += n


def _proj_a(x2d, w, tm):
    m = x2d.shape[0]
    widths = (GLA_KPAD, GLA_KPAD, SELF_W, SELF_W, MEM_W, LANE)
    assert w.shape == (D_MODEL, sum(widths))
    return pl.pallas_call(
        _proj_a_kernel,
        grid=(m // tm,),
        in_specs=[pl.BlockSpec((tm, D_MODEL), lambda i: (i, 0)),
                  pl.BlockSpec(w.shape, lambda i: (0, 0))],
        out_specs=[pl.BlockSpec((tm, n), lambda i: (i, 0)) for n in widths],
        out_shape=[jax.ShapeDtypeStruct((m, n), BF16) for n in widths],
        compiler_params=_cparams("parallel"),
        name="proj_a",
    )(x2d, w)


def _gla_kernel(q_ref, k_ref, v_ref, r_ref, g_ref, wgu_ref, bg_ref, gn_ref, o_ref, st_ref, la_ref):
    tg = q_ref.shape[0]

    @pl.when(pl.program_id(1) == 0)
    def _():
        st_ref[...] = jnp.zeros_like(st_ref)

    z = _dot(g_ref[...], wgu_ref[...]) + bg_ref[...]
    la_ref[...] = _log_sigmoid(z) * (1.0 / GLA_TAU)

    row = lax.broadcasted_iota(jnp.int32, (CHUNK, CHUNK), 0)
    col = lax.broadcasted_iota(jnp.int32, (CHUNK, CHUNK), 1)
    causal = col <= row
    tri = jnp.where(causal, 1.0, 0.0).astype(BF16)
    lo_half = lax.broadcasted_iota(jnp.int32, (CHUNK, LANE), 1) < (LANE // 2)
    scale = GLA_DK ** -0.5

    def chunk(c, carry):
        rows = pl.ds(c * CHUNK, CHUNK)
        la_hi, la_lo = _split2(la_ref[rows, :])
        b = _dot(tri, la_hi) + _dot(tri, la_lo)
        bl = b[CHUNK - 1:CHUNK, :]
        qc = q_ref[rows, :].astype(F32) * scale
        kc = k_ref[rows, :].astype(F32)
        qd = (qc * jnp.exp(b)).astype(BF16)
        ki = (kc * jnp.exp(-b)).astype(BF16)
        ke = (kc * jnp.exp(bl - b)).astype(BF16)
        dec = jnp.exp(bl)
        unit = []
        for h in range(GLA_HEADS):
            sl = slice(GLA_DK_PAD * h, GLA_DK_PAD * (h + 1))
            qh, kih, keh = qd[:, sl], ki[:, sl], ke[:, sl]
            att = jnp.where(causal, _dot_nt(qh, kih), 0.0).astype(BF16)
            for t, blk in enumerate(GLA_HEAD_BLOCKS[h]):
                u = 2 * h + t
                vb = v_ref[rows, LANE * blk:LANE * (blk + 1)]
                st = st_ref[u]
                unit.append(_dot(att, vb) + _dot_nt(qh, st.astype(BF16)))
                st_ref[u] = st * dec[:, sl] + _dot_tn(vb, keh)
        o_blk = [unit[0], jnp.where(lo_half, unit[1], unit[2]), unit[3],
                 unit[4], jnp.where(lo_half, unit[5], unit[6]), unit[7]]
        sq = [o * o for o in o_blk]
        full = [jnp.sum(s, axis=-1, keepdims=True) for s in sq]
        lo1 = jnp.sum(jnp.where(lo_half, sq[1], 0.0), axis=-1, keepdims=True)
        hi1 = jnp.sum(jnp.where(lo_half, 0.0, sq[1]), axis=-1, keepdims=True)
        lo4 = jnp.sum(jnp.where(lo_half, sq[4], 0.0), axis=-1, keepdims=True)
        hi4 = jnp.sum(jnp.where(lo_half, 0.0, sq[4]), axis=-1, keepdims=True)
        ss = [full[0] + lo1, hi1 + full[2], full[3] + lo4, hi4 + full[5]]
        inv = [lax.rsqrt(s * (1.0 / GLA_DV) + RMS_EPS) for s in ss]
        inv_blk = [inv[0], jnp.where(lo_half, inv[0], inv[1]), inv[1],
                   inv[2], jnp.where(lo_half, inv[2], inv[3]), inv[3]]
        for blk in range(GLA_VBLOCKS):
            cs = slice(LANE * blk, LANE * (blk + 1))
            rg = r_ref[rows, cs].astype(F32)
            y = o_blk[blk] * inv_blk[blk] * gn_ref[:, cs]
            o_ref[rows, cs] = (y * (rg * _sigmoid(rg))).astype(o_ref.dtype)
        return carry

    for c in range(tg // CHUNK):
        chunk(c, 0)


def _gla(q, k, v, r, g, wgu, bg, gn, batch, seq, tg):
    m = batch * seq
    nt = seq // tg
    row_map = lambda b, i: (b * nt + i, 0)
    const = lambda b, i: (0, 0)
    return pl.pallas_call(
        _gla_kernel,
        grid=(batch, nt),
        in_specs=[pl.BlockSpec((tg, GLA_KPAD), row_map),
                  pl.BlockSpec((tg, GLA_KPAD), row_map),
                  pl.BlockSpec((tg, SELF_W), row_map),
                  pl.BlockSpec((tg, SELF_W), row_map),
                  pl.BlockSpec((tg, LANE), row_map),
                  pl.BlockSpec(wgu.shape, const),
                  pl.BlockSpec(bg.shape, const),
                  pl.BlockSpec(gn.shape, const)],
        out_specs=pl.BlockSpec((tg, SELF_W), row_map),
        out_shape=jax.ShapeDtypeStruct((m, SELF_W), BF16),
        scratch_shapes=[pltpu.VMEM((2 * GLA_HEADS, LANE, GLA_DK_PAD), F32),
                        pltpu.VMEM((tg, GLA_KPAD), F32)],
        compiler_params=_cparams("parallel", "arbitrary"),
        name="gla",
    )(q, k, v, r, g, wgu, bg, gn)


def _mem_kv_kernel(m_ref, w_ref, o_ref):
    o_ref[...] = _dot(m_ref[...].astype(BF16), w_ref[...]).astype(o_ref.dtype)


def _mem_kv(mem2d, w):
    m, n = mem2d.shape[0], w.shape[1]
    tm = N_MEM
    return pl.pallas_call(
        _mem_kv_kernel,
        grid=(m // tm,),
        in_specs=[pl.BlockSpec((tm, D_MODEL), lambda i: (i, 0)),
                  pl.BlockSpec(w.shape, lambda i: (0, 0))],
        out_specs=pl.BlockSpec((tm, n), lambda i: (i, 0)),
        out_shape=jax.ShapeDtypeStruct((m, n), BF16),
        compiler_params=_cparams("parallel"),
        name="mem_kv",
    )(mem2d, w)


def _top2_sum(a, b, c, d):
    p, q = jnp.maximum(a, b), jnp.minimum(a, b)
    r, s = jnp.maximum(c, d), jnp.minimum(c, d)
    return jnp.maximum(p, r) + jnp.maximum(jnp.minimum(p, r), jnp.maximum(q, s))


PAIRS = tuple((i, j) for i in range(EXPERTS_PER_GROUP) for j in range(i + 1, EXPERTS_PER_GROUP))
N_CLASSES = N_GROUPS * len(PAIRS)
CLS_PAD = 32
SLOT = D_MODEL // LANE


def _to_slots(ref, x):
    rows = x.shape[0]
    for j in range(SLOT):
        ref[pl.ds(j, rows, stride=SLOT), :] = x[:, LANE * j:LANE * (j + 1)]


def _from_slots(ref, rows):
    return jnp.concatenate([ref[pl.ds(j, rows, stride=SLOT), :] for j in range(SLOT)], axis=1)


def _slot(ref, r):
    return ref.at[pl.ds(pl.multiple_of(r * SLOT, SLOT), SLOT)]


CLASS_EA = np.array([EXPERTS_PER_GROUP * (c // len(PAIRS)) + PAIRS[c % len(PAIRS)][0] for c in range(N_CLASSES)], np.int32)
CLASS_EB = np.array([EXPERTS_PER_GROUP * (c // len(PAIRS)) + PAIRS[c % len(PAIRS)][1] for c in range(N_CLASSES)], np.int32)


def _router(sel):
    score = [_top2_sum(*sel[EXPERTS_PER_GROUP * g:EXPERTS_PER_GROUP * (g + 1)]) for g in range(N_GROUPS)]
    picked = []
    for g in range(N_GROUPS):
        ok = None
        for i in range(N_GROUPS):
            if i == g:
                continue
            c = (score[g] > score[i]) if i < g else (score[g] >= score[i])
            ok = c if ok is None else (ok & c)
        picked.append(ok)
    chosen = []
    for g in range(N_GROUPS):
        for j in range(EXPERTS_PER_GROUP):
            ej = EXPERTS_PER_GROUP * g + j
            rank = None
            for i in range(EXPERTS_PER_GROUP):
                if i == j:
                    continue
                ei = EXPERTS_PER_GROUP * g + i
                beats = (sel[ei] >= sel[ej]) if i < j else (sel[ei] > sel[ej])
                beats = jnp.where(beats, 1.0, 0.0)
                rank = beats if rank is None else rank + beats
            chosen.append(picked[g] & (rank < 1.5))
    return [chosen[CLASS_EA[c]] & chosen[CLASS_EB[c]] for c in range(N_CLASSES)]


def _out_kernel(o_ref, mq_ref, mk_ref, mv_ref, x_ref, wo_ref, lng_ref, lnb_ref,
                wrh_ref, wrl_ref, br_ref, x1_ref, idx_ref, cnt_ref, carry_ref):
    tm = o_ref.shape[0]

    @pl.when(pl.program_id(0) == 0)
    def _():
        carry_ref[...] = jnp.zeros_like(carry_ref)

    mq = mq_ref[...]
    mk = mk_ref[...]
    mv = mv_ref[...]
    lane = lax.broadcasted_iota(jnp.int32, (N_MEM, MEM_W), 1)
    m_out = jnp.zeros((tm, MEM_W), F32)
    for h in range(MEM_HEADS):
        in_head = (lane >= MEM_HD * h) & (lane < MEM_HD * (h + 1))
        s = _dot_nt(mq, jnp.where(in_head, mk, jnp.zeros_like(mk))) * (MEM_HD ** -0.5)
        p = jnp.exp(s - jnp.max(s, axis=-1, keepdims=True))
        l = jnp.sum(p, axis=-1, keepdims=True)
        m_out = m_out + _dot(p.astype(BF16), jnp.where(in_head, mv, jnp.zeros_like(mv))) / l
    y = _dot(o_ref[...], wo_ref[:SELF_W, :]) + _dot(m_out.astype(BF16), wo_ref[SELF_W:, :])
    x1 = _layer_norm(DN_ALPHA * x_ref[...] + y, lng_ref[...], lnb_ref[...])
    _to_slots(x1_ref, x1)

    xh, xl = _split2(x1)
    wrh = wrh_ref[...]
    logits = _dot_nt(wrh, xh) + _dot_nt(wrh, xl) + _dot_nt(wrl_ref[...], xh)
    sel = _sigmoid(logits) + br_ref[...]
    onehot = _router([sel[e:e + 1, :] for e in range(N_EXPERTS)])

    csub = lax.broadcasted_iota(jnp.int32, (CLS_PAD, tm), 0)
    oh = jnp.zeros((CLS_PAD, tm), F32)
    for c in range(N_CLASSES):
        oh = jnp.where((csub == c) & onehot[c], 1.0, oh)
    row = lax.broadcasted_iota(jnp.int32, (tm, tm), 0)
    col = lax.broadcasted_iota(jnp.int32, (tm, tm), 1)
    tri_u = jnp.where(row <= col, 1.0, 0.0).astype(BF16)
    incl = _dot(oh.astype(BF16), tri_u)
    carry = carry_ref[...]
    rank = jnp.sum(oh * (carry + incl - 1.0), axis=0, keepdims=True)
    cls = jnp.sum(oh * csub.astype(F32), axis=0, keepdims=True)
    idx_ref[0, 0:1, :] = cls.astype(jnp.int32)
    idx_ref[0, 1:2, :] = rank.astype(jnp.int32)
    carry = carry + jnp.sum(oh, axis=1, keepdims=True)
    carry_ref[...] = carry
    cnt_ref[...] = carry[:, :LANE]


def _out_block(o, mq, mkv, layer, x2d, wo, lng, lnb, wrh, wrl, br, seq, tm):
    m = x2d.shape[0]
    nt = seq // tm
    row_map = lambda i: (i, 0)
    const = lambda i: (0, 0)
    return pl.pallas_call(
        _out_kernel,
        grid=(m // tm,),
        in_specs=[pl.BlockSpec((tm, SELF_W), row_map),
                  pl.BlockSpec((tm, MEM_W), row_map),
                  pl.BlockSpec((N_MEM, MEM_W), lambda i: (i // nt, 2 * layer)),
                  pl.BlockSpec((N_MEM, MEM_W), lambda i: (i // nt, 2 * layer + 1)),
                  pl.BlockSpec((tm, D_MODEL), row_map),
                  pl.BlockSpec(wo.shape, const),
                  pl.BlockSpec(lng.shape, const),
                  pl.BlockSpec(lnb.shape, const),
                  pl.BlockSpec(wrh.shape, const),
                  pl.BlockSpec(wrl.shape, const),
                  pl.BlockSpec(br.shape, const)],
        out_specs=[pl.BlockSpec((tm * SLOT, LANE), row_map),
                   pl.BlockSpec((1, 2, tm), lambda i: (i, 0, 0)),
                   pl.BlockSpec((CLS_PAD, LANE), const)],
        out_shape=[jax.ShapeDtypeStruct((m * SLOT, LANE), F32),
                   jax.ShapeDtypeStruct((m // tm, 2, tm), jnp.int32),
                   jax.ShapeDtypeStruct((CLS_PAD, LANE), F32)],
        scratch_shapes=[pltpu.VMEM((CLS_PAD, tm), F32)],
        compiler_params=_cparams("arbitrary"),
        name="out_ln_router",
    )(o, mq, mkv, mkv, x2d, wo, lng, lnb, wrh, wrl, br)


def _moe_plan(counts, tmx, n_tiles):
    cnt = counts[:N_CLASSES, 0].astype(jnp.int32)
    nt = (cnt + (tmx - 1)) // tmx
    tend = jnp.cumsum(nt)
    offs = (tend - nt) * tmx
    n_used = tend[-1:]
    i = jnp.arange(n_tiles, dtype=jnp.int32)
    ieff = jnp.minimum(i, n_used - 1)
    cls = jnp.sum((ieff[:, None] >= tend[None, :]).astype(jnp.int32), axis=1)
    fresh = jnp.concatenate([jnp.ones((1,), jnp.int32), (cls[1:] != cls[:-1]).astype(jnp.int32)])
    last_tile_row = jnp.where(nt > 0, (tend - 1) * tmx, -1)
    spare = n_used + jnp.arange(N_CLASSES, dtype=jnp.int32)
    spare_row = jnp.where(spare < n_tiles, spare * tmx, -1)
    pad = lambda a: jnp.pad(a, (0, CLS_PAD - N_CLASSES))
    return dict(offs=pad(offs), zero_rows=jnp.concatenate([last_tile_row, spare_row]), row_block=ieff,
                ea=jnp.asarray(CLASS_EA)[cls], eb=jnp.asarray(CLASS_EB)[cls], fresh=fresh, n_used=n_used)


ROW_DMA_UNROLL = 16


def _tile_dest(offs_ref, idx_ref, dest_v, dest_s, sem):
    cls = idx_ref[0, 0:1, :]
    dest = idx_ref[0, 1:2, :]
    for c in range(N_CLASSES):
        dest = dest + jnp.where(cls == c, offs_ref[c], 0)
    dest_v[...] = dest
    cp = pltpu.make_async_copy(dest_v, dest_s, sem)
    cp.start()
    cp.wait()


def _issue_row_copies(n_rows, make_copy):
    def group(g, carry):
        base = pl.multiple_of(g * ROW_DMA_UNROLL, ROW_DMA_UNROLL)
        for k in range(ROW_DMA_UNROLL):
            make_copy(base + k).start(priority=k % 2)
        return carry
    lax.fori_loop(0, n_rows // ROW_DMA_UNROLL, group, 0)


def _dispatch_kernel(offs_ref, zrow_ref, idx_ref, x_ref, xs_ref, dest_v, dest_s, zero_ref, sem_i, sem_z, sem):
    tm = x_ref.shape[0] // SLOT
    zrows = zero_ref.shape[0]

    @pl.when(pl.program_id(0) == 0)
    def _():
        zero_ref[...] = jnp.zeros_like(zero_ref)

        def zcopy(c):
            start = pl.multiple_of(zrow_ref[c] * SLOT, zrows)
            return pltpu.make_async_copy(zero_ref, xs_ref.at[pl.ds(start, zrows)], sem_z)

        for c in range(2 * N_CLASSES):
            @pl.when(zrow_ref[c] >= 0)
            def _(c=c):
                zcopy(c).start()
        for c in range(2 * N_CLASSES):
            @pl.when(zrow_ref[c] >= 0)
            def _(c=c):
                zcopy(c).wait()

    _tile_dest(offs_ref, idx_ref, dest_v, dest_s, sem_i)
    _issue_row_copies(tm, lambda r: pltpu.make_async_copy(_slot(x_ref, r), _slot(xs_ref, dest_s[0, r]), sem))
    pltpu.make_async_copy(x_ref, xs_ref.at[pl.ds(0, tm * SLOT)], sem).wait()


def _dispatch(plan, idx, x1t, tm, tmx, n_rows):
    m = x1t.shape[0] // SLOT
    return pl.pallas_call(
        _dispatch_kernel,
        grid_spec=pltpu.PrefetchScalarGridSpec(
            num_scalar_prefetch=2,
            grid=(m // tm,),
            in_specs=[pl.BlockSpec((1, 2, tm), lambda i, *_: (i, 0, 0)),
                      pl.BlockSpec((tm * SLOT, LANE), lambda i, *_: (i, 0))],
            out_specs=pl.BlockSpec(memory_space=pl.ANY),
            scratch_shapes=[pltpu.VMEM((1, tm), jnp.int32),
                            pltpu.SMEM((1, tm), jnp.int32),
                            pltpu.VMEM((tmx * SLOT, LANE), F32),
                            pltpu.SemaphoreType.DMA, pltpu.SemaphoreType.DMA, pltpu.SemaphoreType.DMA]),
        out_shape=jax.ShapeDtypeStruct((n_rows * SLOT, LANE), F32),
        compiler_params=_cparams("arbitrary"),
        name="moe_dispatch",
    )(plan["offs"], plan["zero_rows"], idx, x1t)


def _moe_ffn_kernel(rb_ref, ea_ref, eb_ref, fresh_ref, nu_ref,
                    xs_ref, wga_ref, wua_ref, wda_ref, wgb_ref, wub_ref, wdb_ref, wr_ref, lng_ref, lnb_ref,
                    ys_ref, sg_ref, su_ref, sd_ref):
    i = pl.program_id(0)
    tmx = ys_ref.shape[0] // SLOT

    @pl.when(i >= nu_ref[0])
    def _():
        ys_ref[...] = jnp.zeros_like(ys_ref)

    @pl.when(i < nu_ref[0])
    def _():
        @pl.when(fresh_ref[i] == 1)
        def _():
            for s, (g, u, d) in enumerate(((wga_ref, wua_ref, wda_ref), (wgb_ref, wub_ref, wdb_ref))):
                sg_ref[s] = g[0, 0].astype(BF16)
                su_ref[s] = u[0, 0].astype(BF16)
                sd_ref[s] = d[0, 0].astype(BF16)

        x = _from_slots(xs_ref, tmx)
        xb = x.astype(BF16)
        aff = _sigmoid(_dot(xb, wr_ref[...]))
        lane = lax.broadcasted_iota(jnp.int32, aff.shape, 1)
        g = [jnp.sum(jnp.where(lane == e_ref[i], aff, 0.0), axis=-1, keepdims=True) for e_ref in (ea_ref, eb_ref)]
        f = None
        for s in range(2):
            hg = _dot(xb, sg_ref[s])
            hu = _dot(xb, su_ref[s])
            h = (hg * _sigmoid(hg) * hu).astype(BF16)
            fs = (g[s] / (g[0] + g[1])) * _dot(h, sd_ref[s])
            f = fs if f is None else f + fs
        _to_slots(ys_ref, _layer_norm(DN_ALPHA * x + f, lng_ref[...], lnb_ref[...]))


def _moe_ffn(plan, xs, layer, wg, wu, wd, wr, lng, lnb, tmx, n_tiles):
    rows = lambda i, rb, *_: (rb[i], 0)
    wa = lambda i, rb, ea, eb, *_: (layer, ea[i], 0, 0)
    wb = lambda i, rb, ea, eb, *_: (layer, eb[i], 0, 0)
    const = lambda i, *_: (0, 0)
    gu, dn = (1, 1, D_MODEL, D_EXPERT), (1, 1, D_EXPERT, D_MODEL)
    return pl.pallas_call(
        _moe_ffn_kernel,
        grid_spec=pltpu.PrefetchScalarGridSpec(
            num_scalar_prefetch=5,
            grid=(n_tiles,),
            in_specs=[pl.BlockSpec((tmx * SLOT, LANE), rows),
                      pl.BlockSpec(gu, wa), pl.BlockSpec(gu, wa), pl.BlockSpec(dn, wa),
                      pl.BlockSpec(gu, wb), pl.BlockSpec(gu, wb), pl.BlockSpec(dn, wb),
                      pl.BlockSpec(wr.shape, const),
                      pl.BlockSpec(lng.shape, const), pl.BlockSpec(lnb.shape, const)],
            out_specs=pl.BlockSpec((tmx * SLOT, LANE), lambda i, *_: (i, 0)),
            scratch_shapes=[pltpu.VMEM((2, D_MODEL, D_EXPERT), BF16),
                            pltpu.VMEM((2, D_MODEL, D_EXPERT), BF16),
                            pltpu.VMEM((2, D_EXPERT, D_MODEL), BF16)]),
        out_shape=jax.ShapeDtypeStruct(xs.shape, F32),
        compiler_params=_cparams("arbitrary"),
        name="moe_ffn",
    )(plan["row_block"], plan["ea"], plan["eb"], plan["fresh"], plan["n_used"],
      xs, wg, wu, wd, wg, wu, wd, wr, lng, lnb)


def _unpermute_kernel(offs_ref, idx_ref, ys_ref, o_ref, dest_v, dest_s, buf_ref, sem_i, sem):
    tm = o_ref.shape[0]
    _tile_dest(offs_ref, idx_ref, dest_v, dest_s, sem_i)
    _issue_row_copies(tm, lambda r: pltpu.make_async_copy(_slot(ys_ref, dest_s[0, r]), _slot(buf_ref, r), sem))
    pltpu.make_async_copy(ys_ref.at[pl.ds(0, tm * SLOT)], buf_ref, sem).wait()
    o_ref[...] = _from_slots(buf_ref, tm)


def _unpermute(plan, idx, ys, m, tm):
    return pl.pallas_call(
        _unpermute_kernel,
        grid_spec=pltpu.PrefetchScalarGridSpec(
            num_scalar_prefetch=1,
            grid=(m // tm,),
            in_specs=[pl.BlockSpec((1, 2, tm), lambda i, *_: (i, 0, 0)),
                      pl.BlockSpec(memory_space=pl.ANY)],
            out_specs=pl.BlockSpec((tm, D_MODEL), lambda i, *_: (i, 0)),
            scratch_shapes=[pltpu.VMEM((1, tm), jnp.int32),
                            pltpu.SMEM((1, tm), jnp.int32),
                            pltpu.VMEM((tm * SLOT, LANE), F32),
                            pltpu.SemaphoreType.DMA, pltpu.SemaphoreType.DMA]),
        out_shape=jax.ShapeDtypeStruct((m, D_MODEL), F32),
        compiler_params=_cparams("arbitrary"),
        name="moe_unpermute",
    )(plan["offs"], idx, ys)


FOX_AUG = 6


def _fox_placement():
    pl_q = np.zeros((3 * LANE, SELF_W), np.float32)
    pl_k = np.zeros((3 * LANE, SELF_W), np.float32)
    ones_q = np.zeros((1, SELF_W), np.float32)
    ones_k = np.zeros((1, SELF_W), np.float32)
    for h in range(FOX_HEADS):
        base = LANE * (h // 2) + FOX_AUG * (h % 2)
        for piece in range(3):
            pl_q[piece * LANE + h, base + piece] = 1.0
            pl_k[piece * LANE + h, base + 3 + piece] = -1.0
            ones_q[0, base + 3 + piece] = 1.0
            ones_k[0, base + piece] = 1.0
    return (jnp.asarray(pl_q, BF16), jnp.asarray(pl_k, BF16), jnp.asarray(ones_q), jnp.asarray(ones_k))


def _proj_b_kernel(x_ref, wq_ref, wog_ref, wmq_ref, wk_ref, wv_ref, wf_ref, bf_ref, plq_ref, plk_ref,
                   oq_ref, ok_ref, q_ref, k_ref, v_ref, og_ref, mq_ref, aq_ref, ak_ref, carry_ref):
    tm = x_ref.shape[0]

    @pl.when(pl.program_id(1) == 0)
    def _():
        carry_ref[...] = jnp.zeros_like(carry_ref)

    xb = x_ref[...].astype(BF16)
    for w_ref, o_ref in ((wq_ref, q_ref), (wk_ref, k_ref), (wv_ref, v_ref), (wog_ref, og_ref), (wmq_ref, mq_ref)):
        o_ref[...] = _dot(xb, w_ref[...]).astype(o_ref.dtype)

    log_f = _log_sigmoid(_dot(xb, wf_ref[...]) + bf_ref[...])
    row = lax.broadcasted_iota(jnp.int32, (tm, tm), 0)
    col = lax.broadcasted_iota(jnp.int32, (tm, tm), 1)
    tri = jnp.where(col <= row, 1.0, 0.0).astype(BF16)
    f_hi, f_mid, f_lo = _split3(log_f)
    cum = carry_ref[...] + (_dot(tri, f_hi) + _dot(tri, f_mid) + _dot(tri, f_lo))
    carry_ref[...] = cum[tm - 1:tm, :]
    c3 = jnp.concatenate(_split3(cum * LOG2E), axis=1)
    aq_ref[...] = (_dot(c3, plq_ref[...]) + oq_ref[...]).astype(aq_ref.dtype)
    ak_ref[...] = (_dot(c3, plk_ref[...]) + ok_ref[...]).astype(ak_ref.dtype)


def _proj_b(x2d, wq, wog, wmq, wk, wv, wf, bf, batch, seq, tm):
    m = batch * seq
    nt = seq // tm
    row_map = lambda b, i: (b * nt + i, 0)
    const = lambda b, i: (0, 0)
    consts = (wq, wog, wmq, wk, wv, wf, bf) + _fox_placement()
    widths = (SELF_W, SELF_W, SELF_W, SELF_W, MEM_W, SELF_W, SELF_W)
    return pl.pallas_call(
        _proj_b_kernel,
        grid=(batch, nt),
        in_specs=[pl.BlockSpec((tm, D_MODEL), row_map)] + [pl.BlockSpec(a.shape, const) for a in consts],
        out_specs=[pl.BlockSpec((tm, n), row_map) for n in widths],
        out_shape=[jax.ShapeDtypeStruct((m, n), BF16) for n in widths],
        scratch_shapes=[pltpu.VMEM((1, LANE), F32)],
        compiler_params=_cparams("parallel", "arbitrary"),
        name="proj_b",
    )(x2d, *consts)


def _pair_rms_norm(t, gain):
    lo_half = lax.broadcasted_iota(jnp.int32, t.shape, 1) < FOX_HD
    sq = t * t
    ss_lo = jnp.sum(jnp.where(lo_half, sq, 0.0), axis=-1, keepdims=True)
    ss_hi = jnp.sum(jnp.where(lo_half, 0.0, sq), axis=-1, keepdims=True)
    ss = jnp.where(lo_half, ss_lo, ss_hi)
    return t * lax.rsqrt(ss * (1.0 / FOX_HD) + RMS_EPS) * gain


def _fox_kernel(q_ref, aq_ref, k_ref, ak_ref, v_ref, og_ref, qg_ref, kg_ref, o_ref,
                ka_ref, qa_ref, m_ref, l_ref, acc_ref, sa_ref, sb_ref):
    tq = q_ref.shape[0]
    seq = k_ref.shape[1]
    qi = pl.program_id(2)

    @pl.when(qi == 0)
    def _():
        def fill(c, carry):
            rows = pl.ds(pl.multiple_of(c * tq, tq), tq)
            kn = _pair_rms_norm(k_ref[0, rows, :].astype(F32), kg_ref[...])
            ka_ref[rows, :] = jnp.concatenate([kn.astype(BF16), ak_ref[0, rows, :]], axis=1)
            return carry
        lax.fori_loop(0, seq // tq, fill, 0)

    lane = lax.broadcasted_iota(jnp.int32, (tq, LANE), 1)
    qn = _pair_rms_norm(q_ref[...].astype(F32), qg_ref[...]) * ((FOX_HD ** -0.5) * LOG2E)
    aq = aq_ref[...]
    for hh in range(2):
        feat = (lane >= FOX_HD * hh) & (lane < FOX_HD * (hh + 1))
        bias = (lane >= FOX_AUG * hh) & (lane < FOX_AUG * (hh + 1))
        qa_ref[hh] = jnp.concatenate([jnp.where(feat, qn, 0.0).astype(BF16),
                                      jnp.where(bias, aq, jnp.zeros_like(aq))], axis=1)

    row = lax.broadcasted_iota(jnp.int32, (tq, tq), 0)
    col = lax.broadcasted_iota(jnp.int32, (tq, tq), 1)
    causal = col <= row

    m_ref[...] = jnp.full(m_ref.shape, NEG_BIG, F32)
    l_ref[...] = jnp.zeros(l_ref.shape, F32)
    acc_ref[...] = jnp.zeros(acc_ref.shape, F32)
    n_cb = tq // LANE

    def scores(j, s_ref):
        kt = ka_ref[pl.ds(pl.multiple_of(j * tq, tq), tq), :]
        for hh in range(2):
            s_ref[hh] = _dot_nt(qa_ref[hh], kt)

    def consume(j, s_ref, masked):
        vt = v_ref[0, pl.ds(pl.multiple_of(j * tq, tq), tq), :]
        for hh in range(2):
            s = s_ref[hh]
            if masked:
                s = jnp.where(causal, s, NEG_BIG)
            s_cb = [s[:, LANE * c:LANE * (c + 1)] for c in range(n_cb)]
            m_old = m_ref[hh]
            row_max = jnp.max(functools.reduce(jnp.maximum, s_cb), axis=-1, keepdims=True)
            m_new = jnp.maximum(m_old, jnp.broadcast_to(row_max, (tq, LANE)))
            alpha = jnp.exp2(m_old - m_new)
            p_cb = [jnp.exp2(sc - m_new) for sc in s_cb]
            l_ref[hh] = alpha * l_ref[hh] + functools.reduce(lambda a, b: a + b, p_cb)
            p = jnp.concatenate([pc.astype(BF16) for pc in p_cb], axis=1)
            acc_ref[hh] = alpha * acc_ref[hh] + _dot(p, vt)
            m_ref[hh] = m_new

    scores(0, sa_ref)

    def pair(t, carry):
        j = 2 * t
        scores(j + 1, sb_ref)
        consume(j, sa_ref, False)
        scores(j + 2, sa_ref)
        consume(j + 1, sb_ref, False)
        return carry

    lax.fori_loop(0, qi // 2, pair, 0)
    odd = (qi % 2) == 1

    @pl.when(odd)
    def _():
        scores(qi, sb_ref)
        consume(qi - 1, sa_ref, False)
        consume(qi, sb_ref, True)

    @pl.when(jnp.logical_not(odd))
    def _():
        consume(qi, sa_ref, True)

    l0 = jnp.sum(l_ref[0], axis=-1, keepdims=True)
    l1 = jnp.sum(l_ref[1], axis=-1, keepdims=True)
    o = jnp.where(lane < FOX_HD, acc_ref[0] / l0, acc_ref[1] / l1)
    og = og_ref[...].astype(F32)
    o_ref[...] = (o * _sigmoid(og)).astype(o_ref.dtype)


def _fox(q, aq, k, ak, v, og, qg2, kg2, batch, seq, tq):
    nq = seq // tq
    m = batch * seq
    k3, ak3, v3 = (a.reshape(batch, seq, SELF_W) for a in (k, ak, v))
    tile_map = lambda b, p, i: (b * nq + i, p)
    seq_map = lambda b, p, i: (b, 0, p)
    const = lambda b, p, i: (0, 0)
    return pl.pallas_call(
        _fox_kernel,
        grid=(batch, FOX_PAIRS, nq),
        in_specs=[pl.BlockSpec((tq, LANE), tile_map),
                  pl.BlockSpec((tq, LANE), tile_map),
                  pl.BlockSpec((1, seq, LANE), seq_map),
                  pl.BlockSpec((1, seq, LANE), seq_map),
                  pl.BlockSpec((1, seq, LANE), seq_map),
                  pl.BlockSpec((tq, LANE), tile_map),
                  pl.BlockSpec((1, LANE), const),
                  pl.BlockSpec((1, LANE), const)],
        out_specs=pl.BlockSpec((tq, LANE), tile_map),
        out_shape=jax.ShapeDtypeStruct((m, SELF_W), BF16),
        scratch_shapes=[pltpu.VMEM((seq, 2 * LANE), BF16),
                        pltpu.VMEM((2, tq, 2 * LANE), BF16),
                        pltpu.VMEM((2, tq, LANE), F32),
                        pltpu.VMEM((2, tq, LANE), F32),
                        pltpu.VMEM((2, tq, LANE), F32),
                        pltpu.VMEM((2, tq, tq), F32),
                        pltpu.VMEM((2, tq, tq), F32)],
        compiler_params=_cparams("parallel", "parallel", "arbitrary"),
        name="fox",
    )(q, aq, k3, ak3, v3, og, qg2, kg2)


def _pad_heads(w, n_heads, d, d_pad):
    lead = w.shape[:-1]
    w = w.reshape(lead + (n_heads, d))
    w = jnp.pad(w, [(0, 0)] * len(lead) + [(0, 0), (0, d_pad - d)])
    return w.reshape(lead + (n_heads * d_pad,))


def _pad_last(w, n):
    return jnp.pad(w, [(0, 0)] * (w.ndim - 1) + [(0, n - w.shape[-1])])


def kernel(x, mem, w_in_a, w_gate_up_a, b_gate_a, gla_norm_g, w_in_b, q_norm_g, w_kv_shared, b_forget, k_norm_g, w_mem_kv, w_out, ln_mix_g, ln_mix_b, ln_ffn_g, ln_ffn_b, w_router, b_router, w_exp_gate, w_exp_up, w_exp_down):
    batch, seq, d = x.shape
    m = batch * seq
    tm = 512
    x2d = x.reshape(m, d)

    wa = w_in_a[0]
    s0, s1, s2, s3, s4 = (GLA_KDIM, 2 * GLA_KDIM, 2 * GLA_KDIM + SELF_W,
                          2 * GLA_KDIM + SELF_W + GLA_GATE_RANK, 2 * GLA_KDIM + 2 * SELF_W + GLA_GATE_RANK)
    w_a = jnp.concatenate([
        _pad_heads(wa[:, :s0], GLA_HEADS, GLA_DK, GLA_DK_PAD),
        _pad_heads(wa[:, s0:s1], GLA_HEADS, GLA_DK, GLA_DK_PAD),
        wa[:, s1:s2], wa[:, s3:s4], wa[:, s4:], _pad_last(wa[:, s2:s3], LANE)], axis=1).astype(BF16)
    wgu = _pad_heads(w_gate_up_a[0], GLA_HEADS, GLA_DK, GLA_DK_PAD)
    wgu = jnp.pad(wgu, ((0, LANE - GLA_GATE_RANK), (0, 0))).astype(BF16)
    bg = _pad_heads(b_gate_a[0], GLA_HEADS, GLA_DK, GLA_DK_PAD).reshape(1, GLA_KPAD)
    gn = jnp.tile(gla_norm_g[0], GLA_HEADS).reshape(1, SELF_W)

    wb = w_in_b[0]
    wq, wog, wmq = (wb[:, :SELF_W].astype(BF16), wb[:, SELF_W:2 * SELF_W].astype(BF16),
                    wb[:, 2 * SELF_W:].astype(BF16))
    wk, wv = w_kv_shared[:, :SELF_W].astype(BF16), w_kv_shared[:, SELF_W:2 * SELF_W].astype(BF16)
    wf = _pad_last(w_kv_shared[:, 2 * SELF_W:], LANE).astype(BF16)
    bf = _pad_last(b_forget, LANE).reshape(1, LANE)
    qg2 = jnp.tile(q_norm_g[0], 2).reshape(1, LANE)
    kg2 = jnp.tile(k_norm_g, 2).reshape(1, LANE)

    w_mkv = jnp.concatenate([w_mem_kv[l] for l in range(DEPTH)], axis=1).astype(BF16)
    wo = w_out.astype(BF16)
    wr_t = w_router.T
    wrh = wr_t.astype(BF16)
    wrl = (wr_t - wrh.astype(F32)).astype(BF16)
    br = b_router.reshape(N_EXPERTS, 1)
    row = lambda a: a.reshape(1, d)

    mkv = _mem_kv(mem.reshape(batch * N_MEM, d), w_mkv)

    tmx = 256
    n_tiles = m // tmx + N_CLASSES
    n_rows = n_tiles * tmx

    wr_pad = _pad_last(w_router, LANE).astype(BF16)

    def tail(o, mq, xin, layer):
        x1t, idx, counts = _out_block(o, mq, mkv, layer, xin, wo[layer], row(ln_mix_g[layer]), row(ln_mix_b[layer]),
                                      wrh, wrl, br, seq, tm)
        plan = _moe_plan(counts, tmx, n_tiles)
        xs = _dispatch(plan, idx, x1t, tm, tmx, n_rows)
        ys = _moe_ffn(plan, xs, layer, w_exp_gate, w_exp_up, w_exp_down, wr_pad,
                      row(ln_ffn_g[layer]), row(ln_ffn_b[layer]), tmx, n_tiles)
        return _unpermute(plan, idx, ys, m, tm)

    q, k, v, r, mq, g = _proj_a(x2d, w_a, tm)
    o = _gla(q, k, v, r, g, wgu, bg, gn, batch, seq, tm)
    xa = tail(o, mq, x2d, 0)

    qb, kb, vb, og, mqb, aq, ak = _proj_b(xa, wq, wog, wmq, wk, wv, wf, bf, batch, seq, tm)
    ob = _fox(qb, aq, kb, ak, vb, og, qg2, kg2, batch, seq, tm)
    xb = tail(ob, mqb, xa, 1)
    return xb.reshape(batch, seq, d)
```

```python
import functools
import math

import jax
import jax.numpy as jnp
import numpy as np
from jax import lax
from jax.experimental import pallas as pl
from jax.experimental.pallas import tpu as pltpu

F32 = jnp.float32
BF16 = jnp.bfloat16

D_MODEL = 1024
DEPTH = 2
CHUNK = 64
N_MEM = 256
MEM_HEADS = 4
MEM_W = D_MODEL // 4
MEM_HD = MEM_W // MEM_HEADS
SELF_W = D_MODEL - MEM_W
GLA_HEADS = 4
GLA_KDIM = SELF_W // 2
GLA_DK = GLA_KDIM // GLA_HEADS
GLA_DV = SELF_W // GLA_HEADS
GLA_GATE_RANK = 16
GLA_TAU = 16.0
FOX_HD = 64
FOX_HEADS = SELF_W // FOX_HD
N_EXPERTS = 16
N_GROUPS = 4
EXPERTS_PER_GROUP = N_EXPERTS // N_GROUPS
D_EXPERT = D_MODEL // 2
DN_ALPHA = (2.0 * DEPTH) ** 0.25
LN_EPS = 1e-5
RMS_EPS = 1e-6

LANE = 128
GLA_DK_PAD = LANE
GLA_KPAD = GLA_HEADS * GLA_DK_PAD
GLA_VBLOCKS = SELF_W // LANE
GLA_HEAD_BLOCKS = ((0, 1), (1, 2), (3, 4), (4, 5))
FOX_PAIRS = FOX_HEADS // 2
LOG2E = math.log2(math.e)
NEG_BIG = -1e30
VMEM_LIMIT_BYTES = 48 * 1024 * 1024

NT_DIMS = (((1,), (1,)), ((), ()))
TN_DIMS = (((0,), (0,)), ((), ()))


def _cparams(*sem):
    return pltpu.CompilerParams(dimension_semantics=sem, vmem_limit_bytes=VMEM_LIMIT_BYTES)


def _dot(a, b):
    return jnp.dot(a, b, preferred_element_type=F32)


def _dot_nt(a, b):
    return lax.dot_general(a, b, NT_DIMS, preferred_element_type=F32)


def _dot_tn(a, b):
    return lax.dot_general(a, b, TN_DIMS, preferred_element_type=F32)


def _log_sigmoid(z):
    return jnp.minimum(z, 0.0) - jnp.log(1.0 + jnp.exp(-jnp.abs(z)))


def _sigmoid(z):
    return 1.0 / (1.0 + jnp.exp(-z))


def _split2(x):
    hi = x.astype(BF16)
    lo = (x - hi.astype(F32)).astype(BF16)
    return hi, lo


def _split3(x):
    hi = x.astype(BF16)
    r1 = x - hi.astype(F32)
    mid = r1.astype(BF16)
    lo = (r1 - mid.astype(F32)).astype(BF16)
    return hi, mid, lo


def _layer_norm(x, g, b):
    mu = jnp.mean(x, axis=-1, keepdims=True)
    xc = x - mu
    var = jnp.mean(xc * xc, axis=-1, keepdims=True)
    return xc * lax.rsqrt(var + LN_EPS) * g + b


def _proj_a_kernel(x_ref, w_ref, q_ref, k_ref, v_ref, r_ref, mq_ref, g_ref):
    xb = x_ref[...].astype(BF16)
    c = 0
    for ref in (q_ref, k_ref, v_ref, r_ref, mq_ref, g_ref):
        n = ref.shape[1]
        ref[...] = _dot(xb, w_ref[:, c:c + n]).astype(ref.dtype)
        c += n


def _proj_a(x2d, w, tm):
    m = x2d.shape[0]
    widths = (GLA_KPAD, GLA_KPAD, SELF_W, SELF_W, MEM_W, LANE)
    assert w.shape == (D_MODEL, sum(widths))
    return pl.pallas_call(
        _proj_a_kernel,
        grid=(m // tm,),
        in_specs=[pl.BlockSpec((tm, D_MODEL), lambda i: (i, 0)),
                  pl.BlockSpec(w.shape, lambda i: (0, 0))],
        out_specs=[pl.BlockSpec((tm, n), lambda i: (i, 0)) for n in widths],
        out_shape=[jax.ShapeDtypeStruct((m, n), BF16) for n in widths],
        compiler_params=_cparams("parallel"),
        name="proj_a",
    )(x2d, w)


def _gla_kernel(q_ref, k_ref, v_ref, r_ref, g_ref, wgu_ref, bg_ref, gn_ref, o_ref, st_ref, la_ref):
    tg = q_ref.shape[0]

    @pl.when(pl.program_id(1) == 0)
    def _():
        st_ref[...] = jnp.zeros_like(st_ref)

    z = _dot(g_ref[...], wgu_ref[...]) + bg_ref[...]
    la_ref[...] = _log_sigmoid(z) * (1.0 / GLA_TAU)

    row = lax.broadcasted_iota(jnp.int32, (CHUNK, CHUNK), 0)
    col = lax.broadcasted_iota(jnp.int32, (CHUNK, CHUNK), 1)
    causal = col <= row
    tri = jnp.where(causal, 1.0, 0.0).astype(BF16)
    lo_half = lax.broadcasted_iota(jnp.int32, (CHUNK, LANE), 1) < (LANE // 2)
    scale = GLA_DK ** -0.5

    def chunk(c, carry):
        rows = pl.ds(c * CHUNK, CHUNK)
        la_hi, la_lo = _split2(la_ref[rows, :])
        b = _dot(tri, la_hi) + _dot(tri, la_lo)
        bl = b[CHUNK - 1:CHUNK, :]
        qc = q_ref[rows, :].astype(F32) * scale
        kc = k_ref[rows, :].astype(F32)
        qd = (qc * jnp.exp(b)).astype(BF16)
        ki = (kc * jnp.exp(-b)).astype(BF16)
        ke = (kc * jnp.exp(bl - b)).astype(BF16)
        dec = jnp.exp(bl)
        unit = []
        for h in range(GLA_HEADS):
            sl = slice(GLA_DK_PAD * h, GLA_DK_PAD * (h + 1))
            qh, kih, keh = qd[:, sl], ki[:, sl], ke[:, sl]
            att = jnp.where(causal, _dot_nt(qh, kih), 0.0).astype(BF16)
            for t, blk in enumerate(GLA_HEAD_BLOCKS[h]):
                u = 2 * h + t
                vb = v_ref[rows, LANE * blk:LANE * (blk + 1)]
                st = st_ref[u]
                unit.append(_dot(att, vb) + _dot_nt(qh, st.astype(BF16)))
                st_ref[u] = st * dec[:, sl] + _dot_tn(vb, keh)
        o_blk = [unit[0], jnp.where(lo_half, unit[1], unit[2]), unit[3],
                 unit[4], jnp.where(lo_half, unit[5], unit[6]), unit[7]]
        sq = [o * o for o in o_blk]
        full = [jnp.sum(s, axis=-1, keepdims=True) for s in sq]
        lo1 = jnp.sum(jnp.where(lo_half, sq[1], 0.0), axis=-1, keepdims=True)
        hi1 = jnp.sum(jnp.where(lo_half, 0.0, sq[1]), axis=-1, keepdims=True)
        lo4 = jnp.sum(jnp.where(lo_half, sq[4], 0.0), axis=-1, keepdims=True)
        hi4 = jnp.sum(jnp.where(lo_half, 0.0, sq[4]), axis=-1, keepdims=True)
        ss = [full[0] + lo1, hi1 + full[2], full[3] + lo4, hi4 + full[5]]
        inv = [lax.rsqrt(s * (1.0 / GLA_DV) + RMS_EPS) for s in ss]
        inv_blk = [inv[0], jnp.where(lo_half, inv[0], inv[1]), inv[1],
                   inv[2], jnp.where(lo_half, inv[2], inv[3]), inv[3]]
        for blk in range(GLA_VBLOCKS):
            cs = slice(LANE * blk, LANE * (blk + 1))
            rg = r_ref[rows, cs].astype(F32)
            y = o_blk[blk] * inv_blk[blk] * gn_ref[:, cs]
            o_ref[rows, cs] = (y * (rg * _sigmoid(rg))).astype(o_ref.dtype)
        return carry

    for c in range(tg // CHUNK):
        chunk(c, 0)


def _gla(q, k, v, r, g, wgu, bg, gn, batch, seq, tg):
    m = batch * seq
    nt = seq // tg
    row_map = lambda b, i: (b * nt + i, 0)
    const = lambda b, i: (0, 0)
    return pl.pallas_call(
        _gla_kernel,
        grid=(batch, nt),
        in_specs=[pl.BlockSpec((tg, GLA_KPAD), row_map),
                  pl.BlockSpec((tg, GLA_KPAD), row_map),
                  pl.BlockSpec((tg, SELF_W), row_map),
                  pl.BlockSpec((tg, SELF_W), row_map),
                  pl.BlockSpec((tg, LANE), row_map),
                  pl.BlockSpec(wgu.shape, const),
                  pl.BlockSpec(bg.shape, const),
                  pl.BlockSpec(gn.shape, const)],
        out_specs=pl.BlockSpec((tg, SELF_W), row_map),
        out_shape=jax.ShapeDtypeStruct((m, SELF_W), BF16),
        scratch_shapes=[pltpu.VMEM((2 * GLA_HEADS, LANE, GLA_DK_PAD), F32),
                        pltpu.VMEM((tg, GLA_KPAD), F32)],
        compiler_params=_cparams("parallel", "arbitrary"),
        name="gla",
    )(q, k, v, r, g, wgu, bg, gn)


def _mem_kv_kernel(m_ref, w_ref, o_ref):
    o_ref[...] = _dot(m_ref[...].astype(BF16), w_ref[...]).astype(o_ref.dtype)


def _mem_kv(mem2d, w):
    m, n = mem2d.shape[0], w.shape[1]
    tm = N_MEM
    return pl.pallas_call(
        _mem_kv_kernel,
        grid=(m // tm,),
        in_specs=[pl.BlockSpec((tm, D_MODEL), lambda i: (i, 0)),
                  pl.BlockSpec(w.shape, lambda i: (0, 0))],
        out_specs=pl.BlockSpec((tm, n), lambda i: (i, 0)),
        out_shape=jax.ShapeDtypeStruct((m, n), BF16),
        compiler_params=_cparams("parallel"),
        name="mem_kv",
    )(mem2d, w)


def _top2_sum(a, b, c, d):
    p, q = jnp.maximum(a, b), jnp.minimum(a, b)
    r, s = jnp.maximum(c, d), jnp.minimum(c, d)
    return jnp.maximum(p, r) + jnp.maximum(jnp.minimum(p, r), jnp.maximum(q, s))


PAIRS = tuple((i, j) for i in range(EXPERTS_PER_GROUP) for j in range(i + 1, EXPERTS_PER_GROUP))
N_CLASSES = N_GROUPS * len(PAIRS)
CLS_PAD = 32
SLOT = D_MODEL // LANE


def _to_slots(ref, x):
    rows = x.shape[0]
    for j in range(SLOT):
        ref[pl.ds(j, rows, stride=SLOT), :] = x[:, LANE * j:LANE * (j + 1)]


def _from_slots(ref, rows):
    return jnp.concatenate([ref[pl.ds(j, rows, stride=SLOT), :] for j in range(SLOT)], axis=1)


def _slot(ref, r):
    return ref.at[pl.ds(pl.multiple_of(r * SLOT, SLOT), SLOT)]


CLASS_EA = np.array([EXPERTS_PER_GROUP * (c // len(PAIRS)) + PAIRS[c % len(PAIRS)][0] for c in range(N_CLASSES)], np.int32)
CLASS_EB = np.array([EXPERTS_PER_GROUP * (c // len(PAIRS)) + PAIRS[c % len(PAIRS)][1] for c in range(N_CLASSES)], np.int32)


def _router(sel):
    score = [_top2_sum(*sel[EXPERTS_PER_GROUP * g:EXPERTS_PER_GROUP * (g + 1)]) for g in range(N_GROUPS)]
    picked = []
    for g in range(N_GROUPS):
        ok = None
        for i in range(N_GROUPS):
            if i == g:
                continue
            c = (score[g] > score[i]) if i < g else (score[g] >= score[i])
            ok = c if ok is None else (ok & c)
        picked.append(ok)
    chosen = []
    for g in range(N_GROUPS):
        for j in range(EXPERTS_PER_GROUP):
            ej = EXPERTS_PER_GROUP * g + j
            rank = None
            for i in range(EXPERTS_PER_GROUP):
                if i == j:
                    continue
                ei = EXPERTS_PER_GROUP * g + i
                beats = (sel[ei] >= sel[ej]) if i < j else (sel[ei] > sel[ej])
                beats = jnp.where(beats, 1.0, 0.0)
                rank = beats if rank is None else rank + beats
            chosen.append(picked[g] & (rank < 1.5))
    return [chosen[CLASS_EA[c]] & chosen[CLASS_EB[c]] for c in range(N_CLASSES)]


def _out_kernel(o_ref, mq_ref, mk_ref, mv_ref, x_ref, wo_ref, lng_ref, lnb_ref,
                wrh_ref, wrl_ref, br_ref, x1_ref, idx_ref, cnt_ref, carry_ref):
    tm = o_ref.shape[0]

    @pl.when(pl.program_id(0) == 0)
    def _():
        carry_ref[...] = jnp.zeros_like(carry_ref)

    mq = mq_ref[...]
    mk = mk_ref[...]
    mv = mv_ref[...]
    lane = lax.broadcasted_iota(jnp.int32, (N_MEM, MEM_W), 1)
    m_out = jnp.zeros((tm, MEM_W), F32)
    for h in range(MEM_HEADS):
        in_head = (lane >= MEM_HD * h) & (lane < MEM_HD * (h + 1))
        s = _dot_nt(mq, jnp.where(in_head, mk, jnp.zeros_like(mk))) * (MEM_HD ** -0.5)
        p = jnp.exp(s - jnp.max(s, axis=-1, keepdims=True))
        l = jnp.sum(p, axis=-1, keepdims=True)
        m_out = m_out + _dot(p.astype(BF16), jnp.where(in_head, mv, jnp.zeros_like(mv))) / l
    y = _dot(o_ref[...], wo_ref[:SELF_W, :]) + _dot(m_out.astype(BF16), wo_ref[SELF_W:, :])
    x1 = _layer_norm(DN_ALPHA * x_ref[...] + y, lng_ref[...], lnb_ref[...])
    _to_slots(x1_ref, x1)

    xh, xl = _split2(x1)
    wrh = wrh_ref[...]
    logits = _dot_nt(wrh, xh) + _dot_nt(wrh, xl) + _dot_nt(wrl_ref[...], xh)
    sel = _sigmoid(logits) + br_ref[...]
    onehot = _router([sel[e:e + 1, :] for e in range(N_EXPERTS)])

    csub = lax.broadcasted_iota(jnp.int32, (CLS_PAD, tm), 0)
    oh = jnp.zeros((CLS_PAD, tm), F32)
    for c in range(N_CLASSES):
        oh = jnp.where((csub == c) & onehot[c], 1.0, oh)
    row = lax.broadcasted_iota(jnp.int32, (tm, tm), 0)
    col = lax.broadcasted_iota(jnp.int32, (tm, tm), 1)
    tri_u = jnp.where(row <= col, 1.0, 0.0).astype(BF16)
    incl = _dot(oh.astype(BF16), tri_u)
    carry = carry_ref[...]
    rank = jnp.sum(oh * (carry + incl - 1.0), axis=0, keepdims=True)
    cls = jnp.sum(oh * csub.astype(F32), axis=0, keepdims=True)
    idx_ref[0, 0:1, :] = cls.astype(jnp.int32)
    idx_ref[0, 1:2, :] = rank.astype(jnp.int32)
    carry = carry + jnp.sum(oh, axis=1, keepdims=True)
    carry_ref[...] = carry
    cnt_ref[...] = carry[:, :LANE]


def _out_block(o, mq, mkv, layer, x2d, wo, lng, lnb, wrh, wrl, br, seq, tm):
    m = x2d.shape[0]
    nt = seq // tm
    row_map = lambda i: (i, 0)
    const = lambda i: (0, 0)
    return pl.pallas_call(
        _out_kernel,
        grid=(m // tm,),
        in_specs=[pl.BlockSpec((tm, SELF_W), row_map),
                  pl.BlockSpec((tm, MEM_W), row_map),
                  pl.BlockSpec((N_MEM, MEM_W), lambda i: (i // nt, 2 * layer)),
                  pl.BlockSpec((N_MEM, MEM_W), lambda i: (i // nt, 2 * layer + 1)),
                  pl.BlockSpec((tm, D_MODEL), row_map),
                  pl.BlockSpec(wo.shape, const),
                  pl.BlockSpec(lng.shape, const),
                  pl.BlockSpec(lnb.shape, const),
                  pl.BlockSpec(wrh.shape, const),
                  pl.BlockSpec(wrl.shape, const),
                  pl.BlockSpec(br.shape, const)],
        out_specs=[pl.BlockSpec((tm * SLOT, LANE), row_map),
                   pl.BlockSpec((1, 2, tm), lambda i: (i, 0, 0)),
                   pl.BlockSpec((CLS_PAD, LANE), const)],
        out_shape=[jax.ShapeDtypeStruct((m * SLOT, LANE), F32),
                   jax.ShapeDtypeStruct((m // tm, 2, tm), jnp.int32),
                   jax.ShapeDtypeStruct((CLS_PAD, LANE), F32)],
        scratch_shapes=[pltpu.VMEM((CLS_PAD, tm), F32)],
        compiler_params=_cparams("arbitrary"),
        name="out_ln_router",
    )(o, mq, mkv, mkv, x2d, wo, lng, lnb, wrh, wrl, br)


def _moe_plan(counts, tmx, n_tiles):
    cnt = counts[:N_CLASSES, 0].astype(jnp.int32)
    nt = (cnt + (tmx - 1)) // tmx
    tend = jnp.cumsum(nt)
    offs = (tend - nt) * tmx
    n_used = tend[-1:]
    i = jnp.arange(n_tiles, dtype=jnp.int32)
    ieff = jnp.minimum(i, n_used - 1)
    cls = jnp.sum((ieff[:, None] >= tend[None, :]).astype(jnp.int32), axis=1)
    fresh = jnp.concatenate([jnp.ones((1,), jnp.int32), (cls[1:] != cls[:-1]).astype(jnp.int32)])
    last_tile_row = jnp.where(nt > 0, (tend - 1) * tmx, -1)
    spare = n_used + jnp.arange(N_CLASSES, dtype=jnp.int32)
    spare_row = jnp.where(spare < n_tiles, spare * tmx, -1)
    pad = lambda a: jnp.pad(a, (0, CLS_PAD - N_CLASSES))
    return dict(offs=pad(offs), zero_rows=jnp.concatenate([last_tile_row, spare_row]), row_block=ieff,
                ea=jnp.asarray(CLASS_EA)[cls], eb=jnp.asarray(CLASS_EB)[cls], fresh=fresh, n_used=n_used)


ROW_DMA_UNROLL = 16


def _tile_dest(offs_ref, idx_ref, dest_v, dest_s, sem):
    cls = idx_ref[0, 0:1, :]
    dest = idx_ref[0, 1:2, :]
    for c in range(N_CLASSES):
        dest = dest + jnp.where(cls == c, offs_ref[c], 0)
    dest_v[...] = dest
    cp = pltpu.make_async_copy(dest_v, dest_s, sem)
    cp.start()
    cp.wait()


def _issue_row_copies(n_rows, make_copy):
    def group(g, carry):
        base = pl.multiple_of(g * ROW_DMA_UNROLL, ROW_DMA_UNROLL)
        for k in range(ROW_DMA_UNROLL):
            make_copy(base + k).start(priority=k % 2)
        return carry
    lax.fori_loop(0, n_rows // ROW_DMA_UNROLL, group, 0)


def _dispatch_kernel(offs_ref, zrow_ref, idx_ref, x_ref, xs_ref, dest_v, dest_s, zero_ref, sem_i, sem_z, sem):
    tm = x_ref.shape[0] // SLOT
    zrows = zero_ref.shape[0]

    @pl.when(pl.program_id(0) == 0)
    def _():
        zero_ref[...] = jnp.zeros_like(zero_ref)

        def zcopy(c):
            start = pl.multiple_of(zrow_ref[c] * SLOT, zrows)
            return pltpu.make_async_copy(zero_ref, xs_ref.at[pl.ds(start, zrows)], sem_z)

        for c in range(2 * N_CLASSES):
            @pl.when(zrow_ref[c] >= 0)
            def _(c=c):
                zcopy(c).start()
        for c in range(2 * N_CLASSES):
            @pl.when(zrow_ref[c] >= 0)
            def _(c=c):
                zcopy(c).wait()

    _tile_dest(offs_ref, idx_ref, dest_v, dest_s, sem_i)
    _issue_row_copies(tm, lambda r: pltpu.make_async_copy(_slot(x_ref, r), _slot(xs_ref, dest_s[0, r]), sem))
    pltpu.make_async_copy(x_ref, xs_ref.at[pl.ds(0, tm * SLOT)], sem).wait()


def _dispatch(plan, idx, x1t, tm, tmx, n_rows):
    m = x1t.shape[0] // SLOT
    return pl.pallas_call(
        _dispatch_kernel,
        grid_spec=pltpu.PrefetchScalarGridSpec(
            num_scalar_prefetch=2,
            grid=(m // tm,),
            in_specs=[pl.BlockSpec((1, 2, tm), lambda i, *_: (i, 0, 0)),
                      pl.BlockSpec((tm * SLOT, LANE), lambda i, *_: (i, 0))],
            out_specs=pl.BlockSpec(memory_space=pl.ANY),
            scratch_shapes=[pltpu.VMEM((1, tm), jnp.int32),
                            pltpu.SMEM((1, tm), jnp.int32),
                            pltpu.VMEM((tmx * SLOT, LANE), F32),
                            pltpu.SemaphoreType.DMA, pltpu.SemaphoreType.DMA, pltpu.SemaphoreType.DMA]),
        out_shape=jax.ShapeDtypeStruct((n_rows * SLOT, LANE), F32),
        compiler_params=_cparams("arbitrary"),
        name="moe_dispatch",
    )(plan["offs"], plan["zero_rows"], idx, x1t)


def _moe_ffn_kernel(rb_ref, ea_ref, eb_ref, fresh_ref, nu_ref,
                    xs_ref, wga_ref, wua_ref, wda_ref, wgb_ref, wub_ref, wdb_ref, wr_ref, lng_ref, lnb_ref,
                    ys_ref, sg_ref, su_ref, sd_ref):
    i = pl.program_id(0)
    tmx = ys_ref.shape[0] // SLOT

    @pl.when(i >= nu_ref[0])
    def _():
        ys_ref[...] = jnp.zeros_like(ys_ref)

    @pl.when(i < nu_ref[0])
    def _():
        @pl.when(fresh_ref[i] == 1)
        def _():
            for s, (g, u, d) in enumerate(((wga_ref, wua_ref, wda_ref), (wgb_ref, wub_ref, wdb_ref))):
                sg_ref[s] = g[0, 0].astype(BF16)
                su_ref[s] = u[0, 0].astype(BF16)
                sd_ref[s] = d[0, 0].astype(BF16)

        x = _from_slots(xs_ref, tmx)
        xb = x.astype(BF16)
        aff = _sigmoid(_dot(xb, wr_ref[...]))
        lane = lax.broadcasted_iota(jnp.int32, aff.shape, 1)
        g = [jnp.sum(jnp.where(lane == e_ref[i], aff, 0.0), axis=-1, keepdims=True) for e_ref in (ea_ref, eb_ref)]
        f = None
        for s in range(2):
            hg = _dot(xb, sg_ref[s])
            hu = _dot(xb, su_ref[s])
            h = (hg * _sigmoid(hg) * hu).astype(BF16)
            fs = (g[s] / (g[0] + g[1])) * _dot(h, sd_ref[s])
            f = fs if f is None else f + fs
        _to_slots(ys_ref, _layer_norm(DN_ALPHA * x + f, lng_ref[...], lnb_ref[...]))


def _moe_ffn(plan, xs, layer, wg, wu, wd, wr, lng, lnb, tmx, n_tiles):
    rows = lambda i, rb, *_: (rb[i], 0)
    wa = lambda i, rb, ea, eb, *_: (layer, ea[i], 0, 0)
    wb = lambda i, rb, ea, eb, *_: (layer, eb[i], 0, 0)
    const = lambda i, *_: (0, 0)
    gu, dn = (1, 1, D_MODEL, D_EXPERT), (1, 1, D_EXPERT, D_MODEL)
    return pl.pallas_call(
        _moe_ffn_kernel,
        grid_spec=pltpu.PrefetchScalarGridSpec(
            num_scalar_prefetch=5,
            grid=(n_tiles,),
            in_specs=[pl.BlockSpec((tmx * SLOT, LANE), rows),
                      pl.BlockSpec(gu, wa), pl.BlockSpec(gu, wa), pl.BlockSpec(dn, wa),
                      pl.BlockSpec(gu, wb), pl.BlockSpec(gu, wb), pl.BlockSpec(dn, wb),
                      pl.BlockSpec(wr.shape, const),
                      pl.BlockSpec(lng.shape, const), pl.BlockSpec(lnb.shape, const)],
            out_specs=pl.BlockSpec((tmx * SLOT, LANE), lambda i, *_: (i, 0)),
            scratch_shapes=[pltpu.VMEM((2, D_MODEL, D_EXPERT), BF16),
                            pltpu.VMEM((2, D_MODEL, D_EXPERT), BF16),
                            pltpu.VMEM((2, D_EXPERT, D_MODEL), BF16)]),
        out_shape=jax.ShapeDtypeStruct(xs.shape, F32),
        compiler_params=_cparams("arbitrary"),
        name="moe_ffn",
    )(plan["row_block"], plan["ea"], plan["eb"], plan["fresh"], plan["n_used"],
      xs, wg, wu, wd, wg, wu, wd, wr, lng, lnb)


def _unpermute_kernel(offs_ref, idx_ref, ys_ref, o_ref, dest_v, dest_s, buf_ref, sem_i, sem):
    tm = o_ref.shape[0]
    _tile_dest(offs_ref, idx_ref, dest_v, dest_s, sem_i)
    _issue_row_copies(tm, lambda r: pltpu.make_async_copy(_slot(ys_ref, dest_s[0, r]), _slot(buf_ref, r), sem))
    pltpu.make_async_copy(ys_ref.at[pl.ds(0, tm * SLOT)], buf_ref, sem).wait()
    o_ref[...] = _from_slots(buf_ref, tm)


def _unpermute(plan, idx, ys, m, tm):
    return pl.pallas_call(
        _unpermute_kernel,
        grid_spec=pltpu.PrefetchScalarGridSpec(
            num_scalar_prefetch=1,
            grid=(m // tm,),
            in_specs=[pl.BlockSpec((1, 2, tm), lambda i, *_: (i, 0, 0)),
                      pl.BlockSpec(memory_space=pl.ANY)],
            out_specs=pl.BlockSpec((tm, D_MODEL), lambda i, *_: (i, 0)),
            scratch_shapes=[pltpu.VMEM((1, tm), jnp.int32),
                            pltpu.SMEM((1, tm), jnp.int32),
                            pltpu.VMEM((tm * SLOT, LANE), F32),
                            pltpu.SemaphoreType.DMA, pltpu.SemaphoreType.DMA]),
        out_shape=jax.ShapeDtypeStruct((m, D_MODEL), F32),
        compiler_params=_cparams("arbitrary"),
        name="moe_unpermute",
    )(plan["offs"], idx, ys)


FOX_AUG = 6


def _fox_placement():
    pl_q = np.zeros((3 * LANE, SELF_W), np.float32)
    pl_k = np.zeros((3 * LANE, SELF_W), np.float32)
    ones_q = np.zeros((1, SELF_W), np.float32)
    ones_k = np.zeros((1, SELF_W), np.float32)
    for h in range(FOX_HEADS):
        base = LANE * (h // 2) + FOX_AUG * (h % 2)
        for piece in range(3):
            pl_q[piece * LANE + h, base + piece] = 1.0
            pl_k[piece * LANE + h, base + 3 + piece] = -1.0
            ones_q[0, base + 3 + piece] = 1.0
            ones_k[0, base + piece] = 1.0
    return (jnp.asarray(pl_q, BF16), jnp.asarray(pl_k, BF16), jnp.asarray(ones_q), jnp.asarray(ones_k))


def _proj_b_kernel(x_ref, wq_ref, wog_ref, wmq_ref, wk_ref, wv_ref, wf_ref, bf_ref, plq_ref, plk_ref,
                   oq_ref, ok_ref, q_ref, k_ref, v_ref, og_ref, mq_ref, aq_ref, ak_ref, carry_ref):
    tm = x_ref.shape[0]

    @pl.when(pl.program_id(1) == 0)
    def _():
        carry_ref[...] = jnp.zeros_like(carry_ref)

    xb = x_ref[...].astype(BF16)
    for w_ref, o_ref in ((wq_ref, q_ref), (wk_ref, k_ref), (wv_ref, v_ref), (wog_ref, og_ref), (wmq_ref, mq_ref)):
        o_ref[...] = _dot(xb, w_ref[...]).astype(o_ref.dtype)

    log_f = _log_sigmoid(_dot(xb, wf_ref[...]) + bf_ref[...])
    row = lax.broadcasted_iota(jnp.int32, (tm, tm), 0)
    col = lax.broadcasted_iota(jnp.int32, (tm, tm), 1)
    tri = jnp.where(col <= row, 1.0, 0.0).astype(BF16)
    f_hi, f_mid, f_lo = _split3(log_f)
    cum = carry_ref[...] + (_dot(tri, f_hi) + _dot(tri, f_mid) + _dot(tri, f_lo))
    carry_ref[...] = cum[tm - 1:tm, :]
    c3 = jnp.concatenate(_split3(cum * LOG2E), axis=1)
    aq_ref[...] = (_dot(c3, plq_ref[...]) + oq_ref[...]).astype(aq_ref.dtype)
    ak_ref[...] = (_dot(c3, plk_ref[...]) + ok_ref[...]).astype(ak_ref.dtype)


def _proj_b(x2d, wq, wog, wmq, wk, wv, wf, bf, batch, seq, tm):
    m = batch * seq
    nt = seq // tm
    row_map = lambda b, i: (b * nt + i, 0)
    const = lambda b, i: (0, 0)
    consts = (wq, wog, wmq, wk, wv, wf, bf) + _fox_placement()
    widths = (SELF_W, SELF_W, SELF_W, SELF_W, MEM_W, SELF_W, SELF_W)
    return pl.pallas_call(
        _proj_b_kernel,
        grid=(batch, nt),
        in_specs=[pl.BlockSpec((tm, D_MODEL), row_map)] + [pl.BlockSpec(a.shape, const) for a in consts],
        out_specs=[pl.BlockSpec((tm, n), row_map) for n in widths],
        out_shape=[jax.ShapeDtypeStruct((m, n), BF16) for n in widths],
        scratch_shapes=[pltpu.VMEM((1, LANE), F32)],
        compiler_params=_cparams("parallel", "arbitrary"),
        name="proj_b",
    )(x2d, *consts)


def _pair_rms_norm(t, gain):
    lo_half = lax.broadcasted_iota(jnp.int32, t.shape, 1) < FOX_HD
    sq = t * t
    ss_lo = jnp.sum(jnp.where(lo_half, sq, 0.0), axis=-1, keepdims=True)
    ss_hi = jnp.sum(jnp.where(lo_half, 0.0, sq), axis=-1, keepdims=True)
    ss = jnp.where(lo_half, ss_lo, ss_hi)
    return t * lax.rsqrt(ss * (1.0 / FOX_HD) + RMS_EPS) * gain


SUB = 8


def _sublane_all(x, op):
    shift = SUB // 2
    while shift:
        x = op(x, pltpu.roll(x, shift, 0))
        shift //= 2
    return x


def _fox_kernel(q_ref, aq_ref, k_ref, ak_ref, v_ref, og_ref, qg_ref, kg_ref, o_ref,
                ka_ref, vt_ref, qa_ref, m_ref, l_ref, acc_ref, sa_ref, sb_ref):
    tq = q_ref.shape[0]
    seq = k_ref.shape[1]
    qi = pl.program_id(2)

    @pl.when(qi == 0)
    def _():
        def fill(c, carry):
            rows = pl.ds(pl.multiple_of(c * tq, tq), tq)
            kn = _pair_rms_norm(k_ref[0, rows, :].astype(F32), kg_ref[...])
            ka_ref[rows, :] = jnp.concatenate([kn.astype(BF16), ak_ref[0, rows, :]], axis=1)
            vt_ref[:, rows] = v_ref[0, rows, :].astype(F32).T.astype(BF16)
            return carry
        lax.fori_loop(0, seq // tq, fill, 0)

    lane = lax.broadcasted_iota(jnp.int32, (tq, LANE), 1)
    qn = _pair_rms_norm(q_ref[...].astype(F32), qg_ref[...]) * ((FOX_HD ** -0.5) * LOG2E)
    aq = aq_ref[...]
    for hh in range(2):
        feat = (lane >= FOX_HD * hh) & (lane < FOX_HD * (hh + 1))
        bias = (lane >= FOX_AUG * hh) & (lane < FOX_AUG * (hh + 1))
        qa_ref[hh] = jnp.concatenate([jnp.where(feat, qn, 0.0).astype(BF16),
                                      jnp.where(bias, aq, jnp.zeros_like(aq))], axis=1)

    krow = lax.broadcasted_iota(jnp.int32, (tq, tq), 0)
    qcol = lax.broadcasted_iota(jnp.int32, (tq, tq), 1)
    causal = krow <= qcol
    nb = tq // SUB

    m_ref[...] = jnp.full(m_ref.shape, NEG_BIG, F32)
    l_ref[...] = jnp.zeros(l_ref.shape, F32)
    acc_ref[...] = jnp.zeros(acc_ref.shape, F32)

    def scores(j, s_ref):
        kt = ka_ref[pl.ds(pl.multiple_of(j * tq, tq), tq), :]
        for hh in range(2):
            s_ref[hh] = _dot_nt(kt, qa_ref[hh])

    def consume(j, s_ref, masked):
        vt = vt_ref[:, pl.ds(pl.multiple_of(j * tq, tq), tq)]
        for hh in range(2):
            s = s_ref[hh]
            if masked:
                s = jnp.where(causal, s, NEG_BIG)
            s3 = s.reshape(nb, SUB, tq)
            m_old = m_ref[hh]
            m_new = jnp.maximum(m_old, _sublane_all(jnp.max(s3, axis=0), jnp.maximum))
            alpha = jnp.exp2(m_old - m_new)
            p3 = jnp.exp2(s3 - m_new[None])
            l_ref[hh] = alpha * l_ref[hh] + jnp.sum(p3, axis=0)
            pv = _dot(vt[FOX_HD * hh:FOX_HD * (hh + 1), :], p3.reshape(tq, tq).astype(BF16))
            acc3 = acc_ref[hh].reshape(FOX_HD // SUB, SUB, tq) * alpha[None]
            acc_ref[hh] = acc3.reshape(FOX_HD, tq) + pv
            m_ref[hh] = m_new

    scores(0, sa_ref)

    def pair(t, carry):
        j = 2 * t
        scores(j + 1, sb_ref)
        consume(j, sa_ref, False)
        scores(j + 2, sa_ref)
        consume(j + 1, sb_ref, False)
        return carry

    lax.fori_loop(0, qi // 2, pair, 0)
    odd = (qi % 2) == 1

    @pl.when(odd)
    def _():
        scores(qi, sb_ref)
        consume(qi - 1, sa_ref, False)
        consume(qi, sb_ref, True)

    @pl.when(jnp.logical_not(odd))
    def _():
        consume(qi, sa_ref, True)

    o_t = []
    for hh in range(2):
        l = _sublane_all(l_ref[hh], lambda a, b: a + b)
        o_t.append((acc_ref[hh].reshape(FOX_HD // SUB, SUB, tq) / l[None]).reshape(FOX_HD, tq))
    o = jnp.concatenate(o_t, axis=0).T
    og = og_ref[...].astype(F32)
    o_ref[...] = (o * _sigmoid(og)).astype(o_ref.dtype)


def _fox(q, aq, k, ak, v, og, qg2, kg2, batch, seq, tq):
    nq = seq // tq
    m = batch * seq
    k3, ak3, v3 = (a.reshape(batch, seq, SELF_W) for a in (k, ak, v))
    tile_map = lambda b, p, i: (b * nq + i, p)
    seq_map = lambda b, p, i: (b, 0, p)
    const = lambda b, p, i: (0, 0)
    return pl.pallas_call(
        _fox_kernel,
        grid=(batch, FOX_PAIRS, nq),
        in_specs=[pl.BlockSpec((tq, LANE), tile_map),
                  pl.BlockSpec((tq, LANE), tile_map),
                  pl.BlockSpec((1, seq, LANE), seq_map),
                  pl.BlockSpec((1, seq, LANE), seq_map),
                  pl.BlockSpec((1, seq, LANE), seq_map),
                  pl.BlockSpec((tq, LANE), tile_map),
                  pl.BlockSpec((1, LANE), const),
                  pl.BlockSpec((1, LANE), const)],
        out_specs=pl.BlockSpec((tq, LANE), tile_map),
        out_shape=jax.ShapeDtypeStruct((m, SELF_W), BF16),
        scratch_shapes=[pltpu.VMEM((seq, 2 * LANE), BF16),
                        pltpu.VMEM((LANE, seq), BF16),
                        pltpu.VMEM((2, tq, 2 * LANE), BF16),
                        pltpu.VMEM((2, SUB, tq), F32),
                        pltpu.VMEM((2, SUB, tq), F32),
                        pltpu.VMEM((2, FOX_HD, tq), F32),
                        pltpu.VMEM((2, tq, tq), F32),
                        pltpu.VMEM((2, tq, tq), F32)],
        compiler_params=_cparams("parallel", "parallel", "arbitrary"),
        name="fox",
    )(q, aq, k3, ak3, v3, og, qg2, kg2)


def _pad_heads(w, n_heads, d, d_pad):
    lead = w.shape[:-1]
    w = w.reshape(lead + (n_heads, d))
    w = jnp.pad(w, [(0, 0)] * len(lead) + [(0, 0), (0, d_pad - d)])
    return w.reshape(lead + (n_heads * d_pad,))


def _pad_last(w, n):
    return jnp.pad(w, [(0, 0)] * (w.ndim - 1) + [(0, n - w.shape[-1])])


def kernel(x, mem, w_in_a, w_gate_up_a, b_gate_a, gla_norm_g, w_in_b, q_norm_g, w_kv_shared, b_forget, k_norm_g, w_mem_kv, w_out, ln_mix_g, ln_mix_b, ln_ffn_g, ln_ffn_b, w_router, b_router, w_exp_gate, w_exp_up, w_exp_down):
    batch, seq, d = x.shape
    m = batch * seq
    tm = 512
    x2d = x.reshape(m, d)

    wa = w_in_a[0]
    s0, s1, s2, s3, s4 = (GLA_KDIM, 2 * GLA_KDIM, 2 * GLA_KDIM + SELF_W,
                          2 * GLA_KDIM + SELF_W + GLA_GATE_RANK, 2 * GLA_KDIM + 2 * SELF_W + GLA_GATE_RANK)
    w_a = jnp.concatenate([
        _pad_heads(wa[:, :s0], GLA_HEADS, GLA_DK, GLA_DK_PAD),
        _pad_heads(wa[:, s0:s1], GLA_HEADS, GLA_DK, GLA_DK_PAD),
        wa[:, s1:s2], wa[:, s3:s4], wa[:, s4:], _pad_last(wa[:, s2:s3], LANE)], axis=1).astype(BF16)
    wgu = _pad_heads(w_gate_up_a[0], GLA_HEADS, GLA_DK, GLA_DK_PAD)
    wgu = jnp.pad(wgu, ((0, LANE - GLA_GATE_RANK), (0, 0))).astype(BF16)
    bg = _pad_heads(b_gate_a[0], GLA_HEADS, GLA_DK, GLA_DK_PAD).reshape(1, GLA_KPAD)
    gn = jnp.tile(gla_norm_g[0], GLA_HEADS).reshape(1, SELF_W)

    wb = w_in_b[0]
    wq, wog, wmq = (wb[:, :SELF_W].astype(BF16), wb[:, SELF_W:2 * SELF_W].astype(BF16),
                    wb[:, 2 * SELF_W:].astype(BF16))
    wk, wv = w_kv_shared[:, :SELF_W].astype(BF16), w_kv_shared[:, SELF_W:2 * SELF_W].astype(BF16)
    wf = _pad_last(w_kv_shared[:, 2 * SELF_W:], LANE).astype(BF16)
    bf = _pad_last(b_forget, LANE).reshape(1, LANE)
    qg2 = jnp.tile(q_norm_g[0], 2).reshape(1, LANE)
    kg2 = jnp.tile(k_norm_g, 2).reshape(1, LANE)

    w_mkv = jnp.concatenate([w_mem_kv[l] for l in range(DEPTH)], axis=1).astype(BF16)
    wo = w_out.astype(BF16)
    wr_t = w_router.T
    wrh = wr_t.astype(BF16)
    wrl = (wr_t - wrh.astype(F32)).astype(BF16)
    br = b_router.reshape(N_EXPERTS, 1)
    row = lambda a: a.reshape(1, d)

    mkv = _mem_kv(mem.reshape(batch * N_MEM, d), w_mkv)

    tmx = 256
    n_tiles = m // tmx + N_CLASSES
    n_rows = n_tiles * tmx

    wr_pad = _pad_last(w_router, LANE).astype(BF16)

    def tail(o, mq, xin, layer):
        x1t, idx, counts = _out_block(o, mq, mkv, layer, xin, wo[layer], row(ln_mix_g[layer]), row(ln_mix_b[layer]),
                                      wrh, wrl, br, seq, tm)
        plan = _moe_plan(counts, tmx, n_tiles)
        xs = _dispatch(plan, idx, x1t, tm, tmx, n_rows)
        ys = _moe_ffn(plan, xs, layer, w_exp_gate, w_exp_up, w_exp_down, wr_pad,
                      row(ln_ffn_g[layer]), row(ln_ffn_b[layer]), tmx, n_tiles)
        return _unpermute(plan, idx, ys, m, tm)

    q, k, v, r, mq, g = _proj_a(x2d, w_a, tm)
    o = _gla(q, k, v, r, g, wgu, bg, gn, batch, seq, tm)
    xa = tail(o, mq, x2d, 0)

    qb, kb, vb, og, mqb, aq, ak = _proj_b(xa, wq, wog, wmq, wk, wv, wf, bf, batch, seq, tm)
    ob = _fox(qb, aq, kb, ak, vb, og, qg2, kg2, batch, seq, tm)
    xb = tail(ob, mqb, xa, 1)
    return xb.reshape(batch, seq, d)
```

```python
import functools
import math

import jax
import jax.numpy as jnp
import numpy as np
from jax import lax
from jax.experimental import pallas as pl
from jax.experimental.pallas import tpu as pltpu

F32 = jnp.float32
BF16 = jnp.bfloat16

D_MODEL = 1024
DEPTH = 2
CHUNK = 64
N_MEM = 256
MEM_HEADS = 4
MEM_W = D_MODEL // 4
MEM_HD = MEM_W // MEM_HEADS
SELF_W = D_MODEL - MEM_W
GLA_HEADS = 4
GLA_KDIM = SELF_W // 2
GLA_DK = GLA_KDIM // GLA_HEADS
GLA_DV = SELF_W // GLA_HEADS
GLA_GATE_RANK = 16
GLA_TAU = 16.0
FOX_HD = 64
FOX_HEADS = SELF_W // FOX_HD
N_EXPERTS = 16
N_GROUPS = 4
EXPERTS_PER_GROUP = N_EXPERTS // N_GROUPS
D_EXPERT = D_MODEL // 2
DN_ALPHA = (2.0 * DEPTH) ** 0.25
LN_EPS = 1e-5
RMS_EPS = 1e-6

LANE = 128
GLA_DK_PAD = LANE
GLA_KPAD = GLA_HEADS * GLA_DK_PAD
GLA_VBLOCKS = SELF_W // LANE
GLA_HEAD_BLOCKS = ((0, 1), (1, 2), (3, 4), (4, 5))
FOX_PAIRS = FOX_HEADS // 2
LOG2E = math.log2(math.e)
NEG_BIG = -1e30
VMEM_LIMIT_BYTES = 48 * 1024 * 1024

NT_DIMS = (((1,), (1,)), ((), ()))
TN_DIMS = (((0,), (0,)), ((), ()))


def _cparams(*sem):
    return pltpu.CompilerParams(dimension_semantics=sem, vmem_limit_bytes=VMEM_LIMIT_BYTES)


def _dot(a, b):
    return jnp.dot(a, b, preferred_element_type=F32)


def _dot_nt(a, b):
    return lax.dot_general(a, b, NT_DIMS, preferred_element_type=F32)


def _dot_tn(a, b):
    return lax.dot_general(a, b, TN_DIMS, preferred_element_type=F32)


def _log_sigmoid(z):
    return jnp.minimum(z, 0.0) - jnp.log(1.0 + jnp.exp(-jnp.abs(z)))


def _sigmoid(z):
    return 1.0 / (1.0 + jnp.exp(-z))


def _split2(x):
    hi = x.astype(BF16)
    lo = (x - hi.astype(F32)).astype(BF16)
    return hi, lo


def _split3(x):
    hi = x.astype(BF16)
    r1 = x - hi.astype(F32)
    mid = r1.astype(BF16)
    lo = (r1 - mid.astype(F32)).astype(BF16)
    return hi, mid, lo


def _layer_norm(x, g, b):
    mu = jnp.mean(x, axis=-1, keepdims=True)
    xc = x - mu
    var = jnp.mean(xc * xc, axis=-1, keepdims=True)
    return xc * lax.rsqrt(var + LN_EPS) * g + b


def _proj_a_kernel(x_ref, w_ref, q_ref, k_ref, v_ref, r_ref, mq_ref, g_ref):
    xb = x_ref[...].astype(BF16)
    c = 0
    for ref in (q_ref, k_ref, v_ref, r_ref, mq_ref, g_ref):
        n = ref.shape[1]
        ref[...] = _dot(xb, w_ref[:, c:c + n]).astype(ref.dtype)
        c += n


def _proj_a(x2d, w, tm):
    m = x2d.shape[0]
    widths = (GLA_KPAD, GLA_KPAD, SELF_W, SELF_W, MEM_W, LANE)
    assert w.shape == (D_MODEL, sum(widths))
    return pl.pallas_call(
        _proj_a_kernel,
        grid=(m // tm,),
        in_specs=[pl.BlockSpec((tm, D_MODEL), lambda i: (i, 0)),
                  pl.BlockSpec(w.shape, lambda i: (0, 0))],
        out_specs=[pl.BlockSpec((tm, n), lambda i: (i, 0)) for n in widths],
        out_shape=[jax.ShapeDtypeStruct((m, n), BF16) for n in widths],
        compiler_params=_cparams("parallel"),
        name="proj_a",
    )(x2d, w)


def _gla_kernel(q_ref, k_ref, v_ref, r_ref, g_ref, wgu_ref, bg_ref, gn_ref, o_ref, st_ref, la_ref):
    tg = q_ref.shape[0]

    @pl.when(pl.program_id(1) == 0)
    def _():
        st_ref[...] = jnp.zeros_like(st_ref)

    z = _dot(g_ref[...], wgu_ref[...]) + bg_ref[...]
    la_ref[...] = _log_sigmoid(z) * (1.0 / GLA_TAU)

    row = lax.broadcasted_iota(jnp.int32, (CHUNK, CHUNK), 0)
    col = lax.broadcasted_iota(jnp.int32, (CHUNK, CHUNK), 1)
    causal = col <= row
    tri = jnp.where(causal, 1.0, 0.0).astype(BF16)
    lo_half = lax.broadcasted_iota(jnp.int32, (CHUNK, LANE), 1) < (LANE // 2)
    scale = GLA_DK ** -0.5

    def chunk(c, carry):
        rows = pl.ds(c * CHUNK, CHUNK)
        la_hi, la_lo = _split2(la_ref[rows, :])
        b = _dot(tri, la_hi) + _dot(tri, la_lo)
        bl = b[CHUNK - 1:CHUNK, :]
        qc = q_ref[rows, :].astype(F32) * scale
        kc = k_ref[rows, :].astype(F32)
        qd = (qc * jnp.exp(b)).astype(BF16)
        ki = (kc * jnp.exp(-b)).astype(BF16)
        ke = (kc * jnp.exp(bl - b)).astype(BF16)
        dec = jnp.exp(bl)
        unit = []
        for h in range(GLA_HEADS):
            sl = slice(GLA_DK_PAD * h, GLA_DK_PAD * (h + 1))
            qh, kih, keh = qd[:, sl], ki[:, sl], ke[:, sl]
            att = jnp.where(causal, _dot_nt(qh, kih), 0.0).astype(BF16)
            for t, blk in enumerate(GLA_HEAD_BLOCKS[h]):
                u = 2 * h + t
                vb = v_ref[rows, LANE * blk:LANE * (blk + 1)]
                st = st_ref[u]
                unit.append(_dot(att, vb) + _dot_nt(qh, st.astype(BF16)))
                st_ref[u] = st * dec[:, sl] + _dot_tn(vb, keh)
        o_blk = [unit[0], jnp.where(lo_half, unit[1], unit[2]), unit[3],
                 unit[4], jnp.where(lo_half, unit[5], unit[6]), unit[7]]
        sq = [o * o for o in o_blk]
        full = [jnp.sum(s, axis=-1, keepdims=True) for s in sq]
        lo1 = jnp.sum(jnp.where(lo_half, sq[1], 0.0), axis=-1, keepdims=True)
        hi1 = jnp.sum(jnp.where(lo_half, 0.0, sq[1]), axis=-1, keepdims=True)
        lo4 = jnp.sum(jnp.where(lo_half, sq[4], 0.0), axis=-1, keepdims=True)
        hi4 = jnp.sum(jnp.where(lo_half, 0.0, sq[4]), axis=-1, keepdims=True)
        ss = [full[0] + lo1, hi1 + full[2], full[3] + lo4, hi4 + full[5]]
        inv = [lax.rsqrt(s * (1.0 / GLA_DV) + RMS_EPS) for s in ss]
        inv_blk = [inv[0], jnp.where(lo_half, inv[0], inv[1]), inv[1],
                   inv[2], jnp.where(lo_half, inv[2], inv[3]), inv[3]]
        for blk in range(GLA_VBLOCKS):
            cs = slice(LANE * blk, LANE * (blk + 1))
            rg = r_ref[rows, cs].astype(F32)
            y = o_blk[blk] * inv_blk[blk] * gn_ref[:, cs]
            o_ref[rows, cs] = (y * (rg * _sigmoid(rg))).astype(o_ref.dtype)
        return carry

    for c in range(tg // CHUNK):
        chunk(c, 0)


def _gla(q, k, v, r, g, wgu, bg, gn, batch, seq, tg):
    m = batch * seq
    nt = seq // tg
    row_map = lambda b, i: (b * nt + i, 0)
    const = lambda b, i: (0, 0)
    return pl.pallas_call(
        _gla_kernel,
        grid=(batch, nt),
        in_specs=[pl.BlockSpec((tg, GLA_KPAD), row_map),
                  pl.BlockSpec((tg, GLA_KPAD), row_map),
                  pl.BlockSpec((tg, SELF_W), row_map),
                  pl.BlockSpec((tg, SELF_W), row_map),
                  pl.BlockSpec((tg, LANE), row_map),
                  pl.BlockSpec(wgu.shape, const),
                  pl.BlockSpec(bg.shape, const),
                  pl.BlockSpec(gn.shape, const)],
        out_specs=pl.BlockSpec((tg, SELF_W), row_map),
        out_shape=jax.ShapeDtypeStruct((m, SELF_W), BF16),
        scratch_shapes=[pltpu.VMEM((2 * GLA_HEADS, LANE, GLA_DK_PAD), F32),
                        pltpu.VMEM((tg, GLA_KPAD), F32)],
        compiler_params=_cparams("parallel", "arbitrary"),
        name="gla",
    )(q, k, v, r, g, wgu, bg, gn)


def _mem_kv_kernel(m_ref, w_ref, o_ref):
    o_ref[...] = _dot(m_ref[...].astype(BF16), w_ref[...]).astype(o_ref.dtype)


def _mem_kv(mem2d, w):
    m, n = mem2d.shape[0], w.shape[1]
    tm = N_MEM
    return pl.pallas_call(
        _mem_kv_kernel,
        grid=(m // tm,),
        in_specs=[pl.BlockSpec((tm, D_MODEL), lambda i: (i, 0)),
                  pl.BlockSpec(w.shape, lambda i: (0, 0))],
        out_specs=pl.BlockSpec((tm, n), lambda i: (i, 0)),
        out_shape=jax.ShapeDtypeStruct((m, n), BF16),
        compiler_params=_cparams("parallel"),
        name="mem_kv",
    )(mem2d, w)


def _top2_sum(a, b, c, d):
    p, q = jnp.maximum(a, b), jnp.minimum(a, b)
    r, s = jnp.maximum(c, d), jnp.minimum(c, d)
    return jnp.maximum(p, r) + jnp.maximum(jnp.minimum(p, r), jnp.maximum(q, s))


PAIRS = tuple((i, j) for i in range(EXPERTS_PER_GROUP) for j in range(i + 1, EXPERTS_PER_GROUP))
N_CLASSES = N_GROUPS * len(PAIRS)
CLS_PAD = 32
SLOT = D_MODEL // LANE


def _to_slots(ref, x):
    rows = x.shape[0]
    for j in range(SLOT):
        ref[pl.ds(j, rows, stride=SLOT), :] = x[:, LANE * j:LANE * (j + 1)]


def _from_slots(ref, rows):
    return jnp.concatenate([ref[pl.ds(j, rows, stride=SLOT), :] for j in range(SLOT)], axis=1)


def _slot(ref, r):
    return ref.at[pl.ds(pl.multiple_of(r * SLOT, SLOT), SLOT)]


CLASS_EA = np.array([EXPERTS_PER_GROUP * (c // len(PAIRS)) + PAIRS[c % len(PAIRS)][0] for c in range(N_CLASSES)], np.int32)
CLASS_EB = np.array([EXPERTS_PER_GROUP * (c // len(PAIRS)) + PAIRS[c % len(PAIRS)][1] for c in range(N_CLASSES)], np.int32)


def _router(sel):
    score = [_top2_sum(*sel[EXPERTS_PER_GROUP * g:EXPERTS_PER_GROUP * (g + 1)]) for g in range(N_GROUPS)]
    picked = []
    for g in range(N_GROUPS):
        ok = None
        for i in range(N_GROUPS):
            if i == g:
                continue
            c = (score[g] > score[i]) if i < g else (score[g] >= score[i])
            ok = c if ok is None else (ok & c)
        picked.append(ok)
    chosen = []
    for g in range(N_GROUPS):
        for j in range(EXPERTS_PER_GROUP):
            ej = EXPERTS_PER_GROUP * g + j
            rank = None
            for i in range(EXPERTS_PER_GROUP):
                if i == j:
                    continue
                ei = EXPERTS_PER_GROUP * g + i
                beats = (sel[ei] >= sel[ej]) if i < j else (sel[ei] > sel[ej])
                beats = jnp.where(beats, 1.0, 0.0)
                rank = beats if rank is None else rank + beats
            chosen.append(picked[g] & (rank < 1.5))
    return [chosen[CLASS_EA[c]] & chosen[CLASS_EB[c]] for c in range(N_CLASSES)]


def _out_kernel(o_ref, mq_ref, mk_ref, mv_ref, x_ref, wo_ref, lng_ref, lnb_ref,
                wrh_ref, wrl_ref, br_ref, x1_ref, idx_ref, cnt_ref, carry_ref):
    tm = o_ref.shape[0]

    @pl.when(pl.program_id(0) == 0)
    def _():
        carry_ref[...] = jnp.zeros_like(carry_ref)

    mq = mq_ref[...]
    mk = mk_ref[...]
    mv = mv_ref[...]
    lane = lax.broadcasted_iota(jnp.int32, (N_MEM, MEM_W), 1)
    m_out = jnp.zeros((tm, MEM_W), F32)
    for h in range(MEM_HEADS):
        in_head = (lane >= MEM_HD * h) & (lane < MEM_HD * (h + 1))
        s = _dot_nt(mq, jnp.where(in_head, mk, jnp.zeros_like(mk))) * (MEM_HD ** -0.5)
        p = jnp.exp(s - jnp.max(s, axis=-1, keepdims=True))
        l = jnp.sum(p, axis=-1, keepdims=True)
        m_out = m_out + _dot(p.astype(BF16), jnp.where(in_head, mv, jnp.zeros_like(mv))) / l
    y = _dot(o_ref[...], wo_ref[:SELF_W, :]) + _dot(m_out.astype(BF16), wo_ref[SELF_W:, :])
    x1 = _layer_norm(DN_ALPHA * x_ref[...] + y, lng_ref[...], lnb_ref[...])
    _to_slots(x1_ref, x1)

    xh, xl = _split2(x1)
    wrh = wrh_ref[...]
    logits = _dot_nt(wrh, xh) + _dot_nt(wrh, xl) + _dot_nt(wrl_ref[...], xh)
    sel = _sigmoid(logits) + br_ref[...]
    onehot = _router([sel[e:e + 1, :] for e in range(N_EXPERTS)])

    csub = lax.broadcasted_iota(jnp.int32, (CLS_PAD, tm), 0)
    oh = jnp.zeros((CLS_PAD, tm), F32)
    for c in range(N_CLASSES):
        oh = jnp.where((csub == c) & onehot[c], 1.0, oh)
    row = lax.broadcasted_iota(jnp.int32, (tm, tm), 0)
    col = lax.broadcasted_iota(jnp.int32, (tm, tm), 1)
    tri_u = jnp.where(row <= col, 1.0, 0.0).astype(BF16)
    incl = _dot(oh.astype(BF16), tri_u)
    carry = carry_ref[...]
    rank = jnp.sum(oh * (carry + incl - 1.0), axis=0, keepdims=True)
    cls = jnp.sum(oh * csub.astype(F32), axis=0, keepdims=True)
    idx_ref[0, 0:1, :] = cls.astype(jnp.int32)
    idx_ref[0, 1:2, :] = rank.astype(jnp.int32)
    carry = carry + jnp.sum(oh, axis=1, keepdims=True)
    carry_ref[...] = carry
    cnt_ref[...] = carry[:, :LANE]


def _out_block(o, mq, mkv, layer, x2d, wo, lng, lnb, wrh, wrl, br, seq, tm):
    m = x2d.shape[0]
    nt = seq // tm
    row_map = lambda i: (i, 0)
    const = lambda i: (0, 0)
    return pl.pallas_call(
        _out_kernel,
        grid=(m // tm,),
        in_specs=[pl.BlockSpec((tm, SELF_W), row_map),
                  pl.BlockSpec((tm, MEM_W), row_map),
                  pl.BlockSpec((N_MEM, MEM_W), lambda i: (i // nt, 2 * layer)),
                  pl.BlockSpec((N_MEM, MEM_W), lambda i: (i // nt, 2 * layer + 1)),
                  pl.BlockSpec((tm, D_MODEL), row_map),
                  pl.BlockSpec(wo.shape, const),
                  pl.BlockSpec(lng.shape, const),
                  pl.BlockSpec(lnb.shape, const),
                  pl.BlockSpec(wrh.shape, const),
                  pl.BlockSpec(wrl.shape, const),
                  pl.BlockSpec(br.shape, const)],
        out_specs=[pl.BlockSpec((tm * SLOT, LANE), row_map),
                   pl.BlockSpec((1, 2, tm), lambda i: (i, 0, 0)),
                   pl.BlockSpec((CLS_PAD, LANE), const)],
        out_shape=[jax.ShapeDtypeStruct((m * SLOT, LANE), F32),
                   jax.ShapeDtypeStruct((m // tm, 2, tm), jnp.int32),
                   jax.ShapeDtypeStruct((CLS_PAD, LANE), F32)],
        scratch_shapes=[pltpu.VMEM((CLS_PAD, tm), F32)],
        compiler_params=_cparams("arbitrary"),
        name="out_ln_router",
    )(o, mq, mkv, mkv, x2d, wo, lng, lnb, wrh, wrl, br)


def _moe_plan(counts, tmx, n_tiles):
    cnt = counts[:N_CLASSES, 0].astype(jnp.int32)
    nt = (cnt + (tmx - 1)) // tmx
    tend = jnp.cumsum(nt)
    offs = (tend - nt) * tmx
    n_used = tend[-1:]
    i = jnp.arange(n_tiles + 1, dtype=jnp.int32)
    ieff = jnp.minimum(i, n_used - 1)
    cls = jnp.sum((ieff[:, None] >= tend[None, :]).astype(jnp.int32), axis=1)
    fresh = jnp.concatenate([jnp.ones((1,), jnp.int32), (cls[1:] != cls[:-1]).astype(jnp.int32)])
    last_tile_row = jnp.where(nt > 0, (tend - 1) * tmx, -1)
    spare = n_used + jnp.arange(N_CLASSES, dtype=jnp.int32)
    spare_row = jnp.where(spare < n_tiles, spare * tmx, -1)
    pad = lambda a: jnp.pad(a, (0, CLS_PAD - N_CLASSES))
    return dict(offs=pad(offs), zero_rows=jnp.concatenate([last_tile_row, spare_row]), row_block=ieff,
                ea=jnp.asarray(CLASS_EA)[cls], eb=jnp.asarray(CLASS_EB)[cls], fresh=fresh, n_used=n_used)


ROW_DMA_UNROLL = 16


def _tile_dest(offs_ref, idx_ref, dest_v, dest_s, sem):
    cls = idx_ref[0, 0:1, :]
    dest = idx_ref[0, 1:2, :]
    for c in range(N_CLASSES):
        dest = dest + jnp.where(cls == c, offs_ref[c], 0)
    dest_v[...] = dest
    cp = pltpu.make_async_copy(dest_v, dest_s, sem)
    cp.start()
    cp.wait()


def _issue_row_copies(n_rows, make_copy):
    def group(g, carry):
        base = pl.multiple_of(g * ROW_DMA_UNROLL, ROW_DMA_UNROLL)
        for k in range(ROW_DMA_UNROLL):
            make_copy(base + k).start(priority=k % 2)
        return carry
    lax.fori_loop(0, n_rows // ROW_DMA_UNROLL, group, 0)


def _dispatch_kernel(offs_ref, zrow_ref, idx_ref, x_ref, xs_ref, dest_v, dest_s, zero_ref, sem_i, sem_z, sem):
    tm = x_ref.shape[0] // SLOT
    zrows = zero_ref.shape[0]

    @pl.when(pl.program_id(0) == 0)
    def _():
        zero_ref[...] = jnp.zeros_like(zero_ref)

        def zcopy(c):
            start = pl.multiple_of(zrow_ref[c] * SLOT, zrows)
            return pltpu.make_async_copy(zero_ref, xs_ref.at[pl.ds(start, zrows)], sem_z)

        for c in range(2 * N_CLASSES):
            @pl.when(zrow_ref[c] >= 0)
            def _(c=c):
                zcopy(c).start()
        for c in range(2 * N_CLASSES):
            @pl.when(zrow_ref[c] >= 0)
            def _(c=c):
                zcopy(c).wait()

    _tile_dest(offs_ref, idx_ref, dest_v, dest_s, sem_i)
    _issue_row_copies(tm, lambda r: pltpu.make_async_copy(_slot(x_ref, r), _slot(xs_ref, dest_s[0, r]), sem))
    pltpu.make_async_copy(x_ref, xs_ref.at[pl.ds(0, tm * SLOT)], sem).wait()


def _dispatch(plan, idx, x1t, tm, tmx, n_rows):
    m = x1t.shape[0] // SLOT
    return pl.pallas_call(
        _dispatch_kernel,
        grid_spec=pltpu.PrefetchScalarGridSpec(
            num_scalar_prefetch=2,
            grid=(m // tm,),
            in_specs=[pl.BlockSpec((1, 2, tm), lambda i, *_: (i, 0, 0)),
                      pl.BlockSpec((tm * SLOT, LANE), lambda i, *_: (i, 0))],
            out_specs=pl.BlockSpec(memory_space=pl.ANY),
            scratch_shapes=[pltpu.VMEM((1, tm), jnp.int32),
                            pltpu.SMEM((1, tm), jnp.int32),
                            pltpu.VMEM((tmx * SLOT, LANE), F32),
                            pltpu.SemaphoreType.DMA, pltpu.SemaphoreType.DMA, pltpu.SemaphoreType.DMA]),
        out_shape=jax.ShapeDtypeStruct((n_rows * SLOT, LANE), F32),
        compiler_params=_cparams("arbitrary"),
        name="moe_dispatch",
    )(plan["offs"], plan["zero_rows"], idx, x1t)


def _moe_ffn_kernel(rb_ref, ea_ref, eb_ref, fresh_ref, nu_ref,
                    xs_ref, wga_ref, wua_ref, wgb_ref, wub_ref, wda_ref, wdb_ref, wr_ref, lng_ref, lnb_ref,
                    ys_ref, sg_ref, su_ref, sd_ref, h_ref, x_ref, g_ref):
    i = pl.program_id(0)
    t = i - 1
    n_used = nu_ref[0]
    tmx = ys_ref.shape[0] // SLOT
    slot_a = i & 1
    slot_b = 1 - slot_a

    @pl.when((i < n_used) & (fresh_ref[i] == 1))
    def _():
        for s, (g, u) in enumerate(((wga_ref, wua_ref), (wgb_ref, wub_ref))):
            sg_ref[s] = g[0, 0].astype(BF16)
            su_ref[s] = u[0, 0].astype(BF16)

    @pl.when((t >= 0) & (t < n_used) & (fresh_ref[jnp.maximum(t, 0)] == 1))
    def _():
        for s, d in enumerate((wda_ref, wdb_ref)):
            sd_ref[s] = d[0, 0].astype(BF16)

    def load_tile():
        x = _from_slots(xs_ref, tmx)
        x_ref[slot_a] = x
        aff = _sigmoid(_dot(x.astype(BF16), wr_ref[...]))
        lane = lax.broadcasted_iota(jnp.int32, aff.shape, 1)
        g = [jnp.sum(jnp.where(lane == e_ref[i], aff, 0.0), axis=-1, keepdims=True) for e_ref in (ea_ref, eb_ref)]
        for s in range(2):
            g_ref[slot_a, s] = jnp.broadcast_to(g[s] / (g[0] + g[1]), (tmx, LANE))

    def gate_up():
        xb = x_ref[slot_a].astype(BF16)
        for s in range(2):
            hg = _dot(xb, sg_ref[s])
            hu = _dot(xb, su_ref[s])
            h_ref[slot_a, s] = (hg * _sigmoid(hg) * hu).astype(BF16)

    def down_norm():
        f = None
        for s in range(2):
            fs = g_ref[slot_b, s][:, :1] * _dot(h_ref[slot_b, s], sd_ref[s])
            f = fs if f is None else f + fs
        _to_slots(ys_ref, _layer_norm(DN_ALPHA * x_ref[slot_b] + f, lng_ref[...], lnb_ref[...]))

    @pl.when(i == 0)
    def _():
        load_tile()
        gate_up()

    @pl.when((i >= 1) & (i < n_used))
    def _():
        load_tile()
        down_norm()
        gate_up()

    @pl.when((i >= n_used) & (t < n_used))
    def _():
        down_norm()

    @pl.when(t >= n_used)
    def _():
        ys_ref[...] = jnp.zeros_like(ys_ref)


def _moe_ffn(plan, xs, layer, wg, wu, wd, wr, lng, lnb, tmx, n_tiles):
    prev = lambda i: jnp.maximum(i - 1, 0)
    rows = lambda i, rb, *_: (rb[i], 0)
    wa = lambda i, rb, ea, eb, *_: (layer, ea[i], 0, 0)
    wb = lambda i, rb, ea, eb, *_: (layer, eb[i], 0, 0)
    wa_prev = lambda i, rb, ea, eb, *_: (layer, ea[prev(i)], 0, 0)
    wb_prev = lambda i, rb, ea, eb, *_: (layer, eb[prev(i)], 0, 0)
    const = lambda i, *_: (0, 0)
    gu, dn = (1, 1, D_MODEL, D_EXPERT), (1, 1, D_EXPERT, D_MODEL)
    return pl.pallas_call(
        _moe_ffn_kernel,
        grid_spec=pltpu.PrefetchScalarGridSpec(
            num_scalar_prefetch=5,
            grid=(n_tiles + 1,),
            in_specs=[pl.BlockSpec((tmx * SLOT, LANE), rows),
                      pl.BlockSpec(gu, wa), pl.BlockSpec(gu, wa), pl.BlockSpec(gu, wb), pl.BlockSpec(gu, wb),
                      pl.BlockSpec(dn, wa_prev), pl.BlockSpec(dn, wb_prev),
                      pl.BlockSpec(wr.shape, const),
                      pl.BlockSpec(lng.shape, const), pl.BlockSpec(lnb.shape, const)],
            out_specs=pl.BlockSpec((tmx * SLOT, LANE), lambda i, *_: (prev(i), 0)),
            scratch_shapes=[pltpu.VMEM((2, D_MODEL, D_EXPERT), BF16),
                            pltpu.VMEM((2, D_MODEL, D_EXPERT), BF16),
                            pltpu.VMEM((2, D_EXPERT, D_MODEL), BF16),
                            pltpu.VMEM((2, 2, tmx, D_EXPERT), BF16),
                            pltpu.VMEM((2, tmx, D_MODEL), F32),
                            pltpu.VMEM((2, 2, tmx, LANE), F32)]),
        out_shape=jax.ShapeDtypeStruct(xs.shape, F32),
        compiler_params=_cparams("arbitrary"),
        name="moe_ffn",
    )(plan["row_block"], plan["ea"], plan["eb"], plan["fresh"], plan["n_used"],
      xs, wg, wu, wg, wu, wd, wd, wr, lng, lnb)


def _unpermute_kernel(offs_ref, idx_ref, ys_ref, o_ref, dest_v, dest_s, buf_ref, sem_i, sem):
    tm = o_ref.shape[0]
    _tile_dest(offs_ref, idx_ref, dest_v, dest_s, sem_i)
    _issue_row_copies(tm, lambda r: pltpu.make_async_copy(_slot(ys_ref, dest_s[0, r]), _slot(buf_ref, r), sem))
    pltpu.make_async_copy(ys_ref.at[pl.ds(0, tm * SLOT)], buf_ref, sem).wait()
    o_ref[...] = _from_slots(buf_ref, tm)


def _unpermute(plan, idx, ys, m, tm):
    return pl.pallas_call(
        _unpermute_kernel,
        grid_spec=pltpu.PrefetchScalarGridSpec(
            num_scalar_prefetch=1,
            grid=(m // tm,),
            in_specs=[pl.BlockSpec((1, 2, tm), lambda i, *_: (i, 0, 0)),
                      pl.BlockSpec(memory_space=pl.ANY)],
            out_specs=pl.BlockSpec((tm, D_MODEL), lambda i, *_: (i, 0)),
            scratch_shapes=[pltpu.VMEM((1, tm), jnp.int32),
                            pltpu.SMEM((1, tm), jnp.int32),
                            pltpu.VMEM((tm * SLOT, LANE), F32),
                            pltpu.SemaphoreType.DMA, pltpu.SemaphoreType.DMA]),
        out_shape=jax.ShapeDtypeStruct((m, D_MODEL), F32),
        compiler_params=_cparams("arbitrary"),
        name="moe_unpermute",
    )(plan["offs"], idx, ys)


FOX_AUG = 6


def _fox_placement():
    pl_q = np.zeros((3 * LANE, SELF_W), np.float32)
    pl_k = np.zeros((3 * LANE, SELF_W), np.float32)
    ones_q = np.zeros((1, SELF_W), np.float32)
    ones_k = np.zeros((1, SELF_W), np.float32)
    for h in range(FOX_HEADS):
        base = LANE * (h // 2) + FOX_AUG * (h % 2)
        for piece in range(3):
            pl_q[piece * LANE + h, base + piece] = 1.0
            pl_k[piece * LANE + h, base + 3 + piece] = -1.0
            ones_q[0, base + 3 + piece] = 1.0
            ones_k[0, base + piece] = 1.0
    return (jnp.asarray(pl_q, BF16), jnp.asarray(pl_k, BF16), jnp.asarray(ones_q), jnp.asarray(ones_k))


def _proj_b_kernel(x_ref, wq_ref, wog_ref, wmq_ref, wk_ref, wv_ref, wf_ref, bf_ref, plq_ref, plk_ref,
                   oq_ref, ok_ref, q_ref, k_ref, v_ref, og_ref, mq_ref, aq_ref, ak_ref, carry_ref):
    tm = x_ref.shape[0]

    @pl.when(pl.program_id(1) == 0)
    def _():
        carry_ref[...] = jnp.zeros_like(carry_ref)

    xb = x_ref[...].astype(BF16)
    for w_ref, o_ref in ((wq_ref, q_ref), (wk_ref, k_ref), (wv_ref, v_ref), (wog_ref, og_ref), (wmq_ref, mq_ref)):
        o_ref[...] = _dot(xb, w_ref[...]).astype(o_ref.dtype)

    log_f = _log_sigmoid(_dot(xb, wf_ref[...]) + bf_ref[...])
    row = lax.broadcasted_iota(jnp.int32, (tm, tm), 0)
    col = lax.broadcasted_iota(jnp.int32, (tm, tm), 1)
    tri = jnp.where(col <= row, 1.0, 0.0).astype(BF16)
    f_hi, f_mid, f_lo = _split3(log_f)
    cum = carry_ref[...] + (_dot(tri, f_hi) + _dot(tri, f_mid) + _dot(tri, f_lo))
    carry_ref[...] = cum[tm - 1:tm, :]
    c3 = jnp.concatenate(_split3(cum * LOG2E), axis=1)
    aq_ref[...] = (_dot(c3, plq_ref[...]) + oq_ref[...]).astype(aq_ref.dtype)
    ak_ref[...] = (_dot(c3, plk_ref[...]) + ok_ref[...]).astype(ak_ref.dtype)


def _proj_b(x2d, wq, wog, wmq, wk, wv, wf, bf, batch, seq, tm):
    m = batch * seq
    nt = seq // tm
    row_map = lambda b, i: (b * nt + i, 0)
    const = lambda b, i: (0, 0)
    consts = (wq, wog, wmq, wk, wv, wf, bf) + _fox_placement()
    widths = (SELF_W, SELF_W, SELF_W, SELF_W, MEM_W, SELF_W, SELF_W)
    return pl.pallas_call(
        _proj_b_kernel,
        grid=(batch, nt),
        in_specs=[pl.BlockSpec((tm, D_MODEL), row_map)] + [pl.BlockSpec(a.shape, const) for a in consts],
        out_specs=[pl.BlockSpec((tm, n), row_map) for n in widths],
        out_shape=[jax.ShapeDtypeStruct((m, n), BF16) for n in widths],
        scratch_shapes=[pltpu.VMEM((1, LANE), F32)],
        compiler_params=_cparams("parallel", "arbitrary"),
        name="proj_b",
    )(x2d, *consts)


def _pair_rms_norm(t, gain):
    lo_half = lax.broadcasted_iota(jnp.int32, t.shape, 1) < FOX_HD
    sq = t * t
    ss_lo = jnp.sum(jnp.where(lo_half, sq, 0.0), axis=-1, keepdims=True)
    ss_hi = jnp.sum(jnp.where(lo_half, 0.0, sq), axis=-1, keepdims=True)
    ss = jnp.where(lo_half, ss_lo, ss_hi)
    return t * lax.rsqrt(ss * (1.0 / FOX_HD) + RMS_EPS) * gain


SUB = 8
FOX_VROWS = FOX_HD + 16
FOX_QTILES = 4


def _sublane_all(x, op):
    shift = SUB // 2
    while shift:
        x = op(x, pltpu.roll(x, shift, 0))
        shift //= 2
    return x


def _fox_kernel(q_ref, aq_ref, k_ref, ak_ref, v_ref, og_ref, qg_ref, kg_ref, o_ref,
                ka_ref, vt_ref, qa_ref, m_ref, acc_ref, sa_ref, sb_ref):
    tq = sa_ref.shape[1]
    seq = k_ref.shape[1]
    w = pl.program_id(2)

    @pl.when(w == 0)
    def _():
        def fill(c, carry):
            rows = pl.ds(pl.multiple_of(c * tq, tq), tq)
            kn = _pair_rms_norm(k_ref[0, rows, :].astype(F32), kg_ref[...])
            ka_ref[rows, :] = jnp.concatenate([kn.astype(BF16), ak_ref[0, rows, :]], axis=1)
            vt = v_ref[0, rows, :].astype(F32).T.astype(BF16)
            for hh in range(2):
                vt_ref[hh, :FOX_HD, rows] = vt[FOX_HD * hh:FOX_HD * (hh + 1), :]
                vt_ref[hh, FOX_HD:, rows] = jnp.ones((FOX_VROWS - FOX_HD, tq), BF16)
            return carry
        lax.fori_loop(0, seq // tq, fill, 0)

    lane = lax.broadcasted_iota(jnp.int32, (tq, LANE), 1)
    krow = lax.broadcasted_iota(jnp.int32, (tq, tq), 0)
    qcol = lax.broadcasted_iota(jnp.int32, (tq, tq), 1)
    causal = krow <= qcol
    nb = tq // SUB

    def prep(i):
        rows = slice(i * tq, (i + 1) * tq)
        qn = _pair_rms_norm(q_ref[rows, :].astype(F32), qg_ref[...]) * ((FOX_HD ** -0.5) * LOG2E)
        aq = aq_ref[rows, :]
        for hh in range(2):
            feat = (lane >= FOX_HD * hh) & (lane < FOX_HD * (hh + 1))
            bias = (lane >= FOX_AUG * hh) & (lane < FOX_AUG * (hh + 1))
            qa_ref[i % 2, hh] = jnp.concatenate([jnp.where(feat, qn, 0.0).astype(BF16),
                                                 jnp.where(bias, aq, jnp.zeros_like(aq))], axis=1)
        m_ref[i % 2] = jnp.full(m_ref.shape[1:], NEG_BIG, F32)
        acc_ref[i % 2] = jnp.zeros(acc_ref.shape[1:], F32)

    def scores(i, j, s_ref):
        kt = ka_ref[pl.ds(pl.multiple_of(j * tq, tq), tq), :]
        for hh in range(2):
            s_ref[hh] = _dot_nt(kt, qa_ref[i % 2, hh])

    def consume(i, j, s_ref, masked):
        cols = pl.ds(pl.multiple_of(j * tq, tq), tq)
        for hh in range(2):
            s = s_ref[hh]
            if masked:
                s = jnp.where(causal, s, NEG_BIG)
            s3 = s.reshape(nb, SUB, tq)
            m_old = m_ref[i % 2, hh]
            m_new = jnp.maximum(m_old, _sublane_all(jnp.max(s3, axis=0), jnp.maximum))
            alpha = jnp.exp2(m_old - m_new)
            p = jnp.exp2(s3 - m_new[None]).reshape(tq, tq).astype(BF16)
            pv = _dot(vt_ref[hh, :, cols], p)
            acc3 = acc_ref[i % 2, hh].reshape(FOX_VROWS // SUB, SUB, tq) * alpha[None]
            acc_ref[i % 2, hh] = acc3.reshape(FOX_VROWS, tq) + pv
            m_ref[i % 2, hh] = m_new

    def finish(i):
        rows = slice(i * tq, (i + 1) * tq)
        o_t = []
        for hh in range(2):
            l = acc_ref[i % 2, hh, FOX_HD:FOX_HD + SUB, :]
            o_t.append((acc_ref[i % 2, hh, :FOX_HD, :].reshape(FOX_HD // SUB, SUB, tq) / l[None]).reshape(FOX_HD, tq))
        o = jnp.concatenate(o_t, axis=0).T
        og = og_ref[rows, :].astype(F32)
        o_ref[rows, :] = (o * _sigmoid(og)).astype(o_ref.dtype)

    cur, other = sa_ref, sb_ref
    prep(0)
    scores(0, 0, cur)
    for i in range(FOX_QTILES):
        qi = FOX_QTILES * w + i

        def pair(t, carry, i=i, cur=cur, other=other):
            j = 2 * t
            scores(i, j + 1, other)
            consume(i, j, cur, False)
            scores(i, j + 2, cur)
            consume(i, j + 1, other, False)
            return carry

        lax.fori_loop(0, qi // 2, pair, 0)
        last = i == FOX_QTILES - 1
        if i % 2 == 0:
            if not last:
                prep(i + 1)
                scores(i + 1, 0, other)
            consume(i, qi, cur, True)
            cur, other = other, cur
        else:
            scores(i, qi, other)
            consume(i, qi - 1, cur, False)
            if not last:
                prep(i + 1)
                scores(i + 1, 0, cur)
            consume(i, qi, other, True)
        finish(i)


def _fox(q, aq, k, ak, v, og, qg2, kg2, batch, seq, tq):
    tg = FOX_QTILES * tq
    ng = seq // tg
    m = batch * seq
    k3, ak3, v3 = (a.reshape(batch, seq, SELF_W) for a in (k, ak, v))
    tile_map = lambda b, p, i: (b * ng + i, p)
    seq_map = lambda b, p, i: (b, 0, p)
    const = lambda b, p, i: (0, 0)
    return pl.pallas_call(
        _fox_kernel,
        grid=(batch, FOX_PAIRS, ng),
        in_specs=[pl.BlockSpec((tg, LANE), tile_map),
                  pl.BlockSpec((tg, LANE), tile_map),
                  pl.BlockSpec((1, seq, LANE), seq_map),
                  pl.BlockSpec((1, seq, LANE), seq_map),
                  pl.BlockSpec((1, seq, LANE), seq_map),
                  pl.BlockSpec((tg, LANE), tile_map),
                  pl.BlockSpec((1, LANE), const),
                  pl.BlockSpec((1, LANE), const)],
        out_specs=pl.BlockSpec((tg, LANE), tile_map),
        out_shape=jax.ShapeDtypeStruct((m, SELF_W), BF16),
        scratch_shapes=[pltpu.VMEM((seq, 2 * LANE), BF16),
                        pltpu.VMEM((2, FOX_VROWS, seq), BF16),
                        pltpu.VMEM((2, 2, tq, 2 * LANE), BF16),
                        pltpu.VMEM((2, 2, SUB, tq), F32),
                        pltpu.VMEM((2, 2, FOX_VROWS, tq), F32),
                        pltpu.VMEM((2, tq, tq), F32),
                        pltpu.VMEM((2, tq, tq), F32)],
        compiler_params=_cparams("parallel", "parallel", "arbitrary"),
        name="fox",
    )(q, aq, k3, ak3, v3, og, qg2, kg2)


def _pad_heads(w, n_heads, d, d_pad):
    lead = w.shape[:-1]
    w = w.reshape(lead + (n_heads, d))
    w = jnp.pad(w, [(0, 0)] * len(lead) + [(0, 0), (0, d_pad - d)])
    return w.reshape(lead + (n_heads * d_pad,))


def _pad_last(w, n):
    return jnp.pad(w, [(0, 0)] * (w.ndim - 1) + [(0, n - w.shape[-1])])


def kernel(x, mem, w_in_a, w_gate_up_a, b_gate_a, gla_norm_g, w_in_b, q_norm_g, w_kv_shared, b_forget, k_norm_g, w_mem_kv, w_out, ln_mix_g, ln_mix_b, ln_ffn_g, ln_ffn_b, w_router, b_router, w_exp_gate, w_exp_up, w_exp_down):
    batch, seq, d = x.shape
    m = batch * seq
    tm = 512
    x2d = x.reshape(m, d)

    wa = w_in_a[0]
    s0, s1, s2, s3, s4 = (GLA_KDIM, 2 * GLA_KDIM, 2 * GLA_KDIM + SELF_W,
                          2 * GLA_KDIM + SELF_W + GLA_GATE_RANK, 2 * GLA_KDIM + 2 * SELF_W + GLA_GATE_RANK)
    w_a = jnp.concatenate([
        _pad_heads(wa[:, :s0], GLA_HEADS, GLA_DK, GLA_DK_PAD),
        _pad_heads(wa[:, s0:s1], GLA_HEADS, GLA_DK, GLA_DK_PAD),
        wa[:, s1:s2], wa[:, s3:s4], wa[:, s4:], _pad_last(wa[:, s2:s3], LANE)], axis=1).astype(BF16)
    wgu = _pad_heads(w_gate_up_a[0], GLA_HEADS, GLA_DK, GLA_DK_PAD)
    wgu = jnp.pad(wgu, ((0, LANE - GLA_GATE_RANK), (0, 0))).astype(BF16)
    bg = _pad_heads(b_gate_a[0], GLA_HEADS, GLA_DK, GLA_DK_PAD).reshape(1, GLA_KPAD)
    gn = jnp.tile(gla_norm_g[0], GLA_HEADS).reshape(1, SELF_W)

    wb = w_in_b[0]
    wq, wog, wmq = (wb[:, :SELF_W].astype(BF16), wb[:, SELF_W:2 * SELF_W].astype(BF16),
                    wb[:, 2 * SELF_W:].astype(BF16))
    wk, wv = w_kv_shared[:, :SELF_W].astype(BF16), w_kv_shared[:, SELF_W:2 * SELF_W].astype(BF16)
    wf = _pad_last(w_kv_shared[:, 2 * SELF_W:], LANE).astype(BF16)
    bf = _pad_last(b_forget, LANE).reshape(1, LANE)
    qg2 = jnp.tile(q_norm_g[0], 2).reshape(1, LANE)
    kg2 = jnp.tile(k_norm_g, 2).reshape(1, LANE)

    w_mkv = jnp.concatenate([w_mem_kv[l] for l in range(DEPTH)], axis=1).astype(BF16)
    wo = w_out.astype(BF16)
    wr_t = w_router.T
    wrh = wr_t.astype(BF16)
    wrl = (wr_t - wrh.astype(F32)).astype(BF16)
    br = b_router.reshape(N_EXPERTS, 1)
    row = lambda a: a.reshape(1, d)

    mkv = _mem_kv(mem.reshape(batch * N_MEM, d), w_mkv)

    tmx = 256
    n_tiles = m // tmx + N_CLASSES
    n_rows = n_tiles * tmx

    wr_pad = _pad_last(w_router, LANE).astype(BF16)

    def tail(o, mq, xin, layer):
        x1t, idx, counts = _out_block(o, mq, mkv, layer, xin, wo[layer], row(ln_mix_g[layer]), row(ln_mix_b[layer]),
                                      wrh, wrl, br, seq, tm)
        plan = _moe_plan(counts, tmx, n_tiles)
        xs = _dispatch(plan, idx, x1t, tm, tmx, n_rows)
        ys = _moe_ffn(plan, xs, layer, w_exp_gate, w_exp_up, w_exp_down, wr_pad,
                      row(ln_ffn_g[layer]), row(ln_ffn_b[layer]), tmx, n_tiles)
        return _unpermute(plan, idx, ys, m, tm)

    q, k, v, r, mq, g = _proj_a(x2d, w_a, tm)
    o = _gla(q, k, v, r, g, wgu, bg, gn, batch, seq, tm)
    xa = tail(o, mq, x2d, 0)

    qb, kb, vb, og, mqb, aq, ak = _proj_b(xa, wq, wog, wmq, wk, wv, wf, bf, batch, seq, tm)
    ob = _fox(qb, aq, kb, ak, vb, og, qg2, kg2, batch, seq, tm)
    xb = tail(ob, mqb, xa, 1)
    return xb.reshape(batch, seq, d)
```

```python
import functools
import math

import jax
import jax.numpy as jnp
import numpy as np
from jax import lax
from jax.experimental import pallas as pl
from jax.experimental.pallas import tpu as pltpu

F32 = jnp.float32
BF16 = jnp.bfloat16

D_MODEL = 1024
DEPTH = 2
CHUNK = 64
N_MEM = 256
MEM_HEADS = 4
MEM_W = D_MODEL // 4
MEM_HD = MEM_W // MEM_HEADS
SELF_W = D_MODEL - MEM_W
GLA_HEADS = 4
GLA_KDIM = SELF_W // 2
GLA_DK = GLA_KDIM // GLA_HEADS
GLA_DV = SELF_W // GLA_HEADS
GLA_GATE_RANK = 16
GLA_TAU = 16.0
FOX_HD = 64
FOX_HEADS = SELF_W // FOX_HD
N_EXPERTS = 16
N_GROUPS = 4
EXPERTS_PER_GROUP = N_EXPERTS // N_GROUPS
D_EXPERT = D_MODEL // 2
DN_ALPHA = (2.0 * DEPTH) ** 0.25
LN_EPS = 1e-5
RMS_EPS = 1e-6

LANE = 128
GLA_DK_PAD = LANE
GLA_KPAD = GLA_HEADS * GLA_DK_PAD
GLA_VBLOCKS = SELF_W // LANE
GLA_HEAD_BLOCKS = ((0, 1), (1, 2), (3, 4), (4, 5))
FOX_PAIRS = FOX_HEADS // 2
LOG2E = math.log2(math.e)
NEG_BIG = -1e30
VMEM_LIMIT_BYTES = 48 * 1024 * 1024

NT_DIMS = (((1,), (1,)), ((), ()))
TN_DIMS = (((0,), (0,)), ((), ()))


def _cparams(*sem):
    return pltpu.CompilerParams(dimension_semantics=sem, vmem_limit_bytes=VMEM_LIMIT_BYTES)


def _dot(a, b):
    return jnp.dot(a, b, preferred_element_type=F32)


def _dot_nt(a, b):
    return lax.dot_general(a, b, NT_DIMS, preferred_element_type=F32)


def _dot_tn(a, b):
    return lax.dot_general(a, b, TN_DIMS, preferred_element_type=F32)


def _log_sigmoid(z):
    return jnp.minimum(z, 0.0) - jnp.log(1.0 + jnp.exp(-jnp.abs(z)))


def _sigmoid(z):
    return 1.0 / (1.0 + jnp.exp(-z))


def _split2(x):
    hi = x.astype(BF16)
    lo = (x - hi.astype(F32)).astype(BF16)
    return hi, lo


def _split3(x):
    hi = x.astype(BF16)
    r1 = x - hi.astype(F32)
    mid = r1.astype(BF16)
    lo = (r1 - mid.astype(F32)).astype(BF16)
    return hi, mid, lo


def _layer_norm(x, g, b):
    mu = jnp.mean(x, axis=-1, keepdims=True)
    xc = x - mu
    var = jnp.mean(xc * xc, axis=-1, keepdims=True)
    return xc * lax.rsqrt(var + LN_EPS) * g + b


def _proj_a_kernel(x_ref, w_ref, q_ref, k_ref, v_ref, r_ref, mq_ref, g_ref):
    xb = x_ref[...].astype(BF16)
    c = 0
    for ref in (q_ref, k_ref, v_ref, r_ref, mq_ref, g_ref):
        n = ref.shape[1]
        ref[...] = _dot(xb, w_ref[:, c:c + n]).astype(ref.dtype)
        c += n


def _proj_a(x2d, w, tm):
    m = x2d.shape[0]
    widths = (GLA_KPAD, GLA_KPAD, SELF_W, SELF_W, MEM_W, LANE)
    assert w.shape == (D_MODEL, sum(widths))
    return pl.pallas_call(
        _proj_a_kernel,
        grid=(m // tm,),
        in_specs=[pl.BlockSpec((tm, D_MODEL), lambda i: (i, 0)),
                  pl.BlockSpec(w.shape, lambda i: (0, 0))],
        out_specs=[pl.BlockSpec((tm, n), lambda i: (i, 0)) for n in widths],
        out_shape=[jax.ShapeDtypeStruct((m, n), BF16) for n in widths],
        compiler_params=_cparams("parallel"),
        name="proj_a",
    )(x2d, w)


def _gla_kernel(q_ref, k_ref, v_ref, r_ref, g_ref, wgu_ref, bg_ref, gn_ref, o_ref, st_ref, la_ref):
    tg = q_ref.shape[0]

    @pl.when(pl.program_id(1) == 0)
    def _():
        st_ref[...] = jnp.zeros_like(st_ref)

    z = _dot(g_ref[...], wgu_ref[...]) + bg_ref[...]
    la_ref[...] = _log_sigmoid(z) * (1.0 / GLA_TAU)

    row = lax.broadcasted_iota(jnp.int32, (CHUNK, CHUNK), 0)
    col = lax.broadcasted_iota(jnp.int32, (CHUNK, CHUNK), 1)
    causal = col <= row
    tri = jnp.where(causal, 1.0, 0.0).astype(BF16)
    lo_half = lax.broadcasted_iota(jnp.int32, (CHUNK, LANE), 1) < (LANE // 2)
    scale = GLA_DK ** -0.5

    def chunk(c, carry):
        rows = pl.ds(c * CHUNK, CHUNK)
        la_hi, la_lo = _split2(la_ref[rows, :])
        b = _dot(tri, la_hi) + _dot(tri, la_lo)
        bl = b[CHUNK - 1:CHUNK, :]
        qc = q_ref[rows, :].astype(F32) * scale
        kc = k_ref[rows, :].astype(F32)
        qd = (qc * jnp.exp(b)).astype(BF16)
        ki = (kc * jnp.exp(-b)).astype(BF16)
        ke = (kc * jnp.exp(bl - b)).astype(BF16)
        dec = jnp.exp(bl)
        unit = []
        for h in range(GLA_HEADS):
            sl = slice(GLA_DK_PAD * h, GLA_DK_PAD * (h + 1))
            qh, kih, keh = qd[:, sl], ki[:, sl], ke[:, sl]
            att = jnp.where(causal, _dot_nt(qh, kih), 0.0).astype(BF16)
            for t, blk in enumerate(GLA_HEAD_BLOCKS[h]):
                u = 2 * h + t
                vb = v_ref[rows, LANE * blk:LANE * (blk + 1)]
                st = st_ref[u]
                unit.append(_dot(att, vb) + _dot_nt(qh, st.astype(BF16)))
                st_ref[u] = st * dec[:, sl] + _dot_tn(vb, keh)
        o_blk = [unit[0], jnp.where(lo_half, unit[1], unit[2]), unit[3],
                 unit[4], jnp.where(lo_half, unit[5], unit[6]), unit[7]]
        sq = [o * o for o in o_blk]
        full = [jnp.sum(s, axis=-1, keepdims=True) for s in sq]
        lo1 = jnp.sum(jnp.where(lo_half, sq[1], 0.0), axis=-1, keepdims=True)
        hi1 = jnp.sum(jnp.where(lo_half, 0.0, sq[1]), axis=-1, keepdims=True)
        lo4 = jnp.sum(jnp.where(lo_half, sq[4], 0.0), axis=-1, keepdims=True)
        hi4 = jnp.sum(jnp.where(lo_half, 0.0, sq[4]), axis=-1, keepdims=True)
        ss = [full[0] + lo1, hi1 + full[2], full[3] + lo4, hi4 + full[5]]
        inv = [lax.rsqrt(s * (1.0 / GLA_DV) + RMS_EPS) for s in ss]
        inv_blk = [inv[0], jnp.where(lo_half, inv[0], inv[1]), inv[1],
                   inv[2], jnp.where(lo_half, inv[2], inv[3]), inv[3]]
        for blk in range(GLA_VBLOCKS):
            cs = slice(LANE * blk, LANE * (blk + 1))
            rg = r_ref[rows, cs].astype(F32)
            y = o_blk[blk] * inv_blk[blk] * gn_ref[:, cs]
            o_ref[rows, cs] = (y * (rg * _sigmoid(rg))).astype(o_ref.dtype)
        return carry

    for c in range(tg // CHUNK):
        chunk(c, 0)


def _gla(q, k, v, r, g, wgu, bg, gn, batch, seq, tg):
    m = batch * seq
    nt = seq // tg
    row_map = lambda b, i: (b * nt + i, 0)
    const = lambda b, i: (0, 0)
    return pl.pallas_call(
        _gla_kernel,
        grid=(batch, nt),
        in_specs=[pl.BlockSpec((tg, GLA_KPAD), row_map),
                  pl.BlockSpec((tg, GLA_KPAD), row_map),
                  pl.BlockSpec((tg, SELF_W), row_map),
                  pl.BlockSpec((tg, SELF_W), row_map),
                  pl.BlockSpec((tg, LANE), row_map),
                  pl.BlockSpec(wgu.shape, const),
                  pl.BlockSpec(bg.shape, const),
                  pl.BlockSpec(gn.shape, const)],
        out_specs=pl.BlockSpec((tg, SELF_W), row_map),
        out_shape=jax.ShapeDtypeStruct((m, SELF_W), BF16),
        scratch_shapes=[pltpu.VMEM((2 * GLA_HEADS, LANE, GLA_DK_PAD), F32),
                        pltpu.VMEM((tg, GLA_KPAD), F32)],
        compiler_params=_cparams("parallel", "arbitrary"),
        name="gla",
    )(q, k, v, r, g, wgu, bg, gn)


def _mem_kv_kernel(m_ref, w_ref, o_ref):
    o_ref[...] = _dot(m_ref[...].astype(BF16), w_ref[...]).astype(o_ref.dtype)


def _mem_kv(mem2d, w):
    m, n = mem2d.shape[0], w.shape[1]
    tm = N_MEM
    return pl.pallas_call(
        _mem_kv_kernel,
        grid=(m // tm,),
        in_specs=[pl.BlockSpec((tm, D_MODEL), lambda i: (i, 0)),
                  pl.BlockSpec(w.shape, lambda i: (0, 0))],
        out_specs=pl.BlockSpec((tm, n), lambda i: (i, 0)),
        out_shape=jax.ShapeDtypeStruct((m, n), BF16),
        compiler_params=_cparams("parallel"),
        name="mem_kv",
    )(mem2d, w)


def _top2_sum(a, b, c, d):
    p, q = jnp.maximum(a, b), jnp.minimum(a, b)
    r, s = jnp.maximum(c, d), jnp.minimum(c, d)
    return jnp.maximum(p, r) + jnp.maximum(jnp.minimum(p, r), jnp.maximum(q, s))


PAIRS = tuple((i, j) for i in range(EXPERTS_PER_GROUP) for j in range(i + 1, EXPERTS_PER_GROUP))
N_CLASSES = N_GROUPS * len(PAIRS)
CLS_PAD = 32
SLOT = D_MODEL // LANE


def _to_slots(ref, x):
    rows = x.shape[0]
    for j in range(SLOT):
        ref[pl.ds(j, rows, stride=SLOT), :] = x[:, LANE * j:LANE * (j + 1)]


def _from_slots(ref, rows):
    return jnp.concatenate([ref[pl.ds(j, rows, stride=SLOT), :] for j in range(SLOT)], axis=1)


def _slot(ref, r):
    return ref.at[pl.ds(pl.multiple_of(r * SLOT, SLOT), SLOT)]


CLASS_EA = np.array([EXPERTS_PER_GROUP * (c // len(PAIRS)) + PAIRS[c % len(PAIRS)][0] for c in range(N_CLASSES)], np.int32)
CLASS_EB = np.array([EXPERTS_PER_GROUP * (c // len(PAIRS)) + PAIRS[c % len(PAIRS)][1] for c in range(N_CLASSES)], np.int32)


def _router(sel):
    score = [_top2_sum(*sel[EXPERTS_PER_GROUP * g:EXPERTS_PER_GROUP * (g + 1)]) for g in range(N_GROUPS)]
    picked = []
    for g in range(N_GROUPS):
        ok = None
        for i in range(N_GROUPS):
            if i == g:
                continue
            c = (score[g] > score[i]) if i < g else (score[g] >= score[i])
            ok = c if ok is None else (ok & c)
        picked.append(ok)
    chosen = []
    for g in range(N_GROUPS):
        for j in range(EXPERTS_PER_GROUP):
            ej = EXPERTS_PER_GROUP * g + j
            rank = None
            for i in range(EXPERTS_PER_GROUP):
                if i == j:
                    continue
                ei = EXPERTS_PER_GROUP * g + i
                beats = (sel[ei] >= sel[ej]) if i < j else (sel[ei] > sel[ej])
                beats = jnp.where(beats, 1.0, 0.0)
                rank = beats if rank is None else rank + beats
            chosen.append(picked[g] & (rank < 1.5))
    return [chosen[CLASS_EA[c]] & chosen[CLASS_EB[c]] for c in range(N_CLASSES)]


def _out_kernel(o_ref, mq_ref, mk_ref, mv_ref, x_ref, wo_ref, lng_ref, lnb_ref,
                wrh_ref, wrl_ref, br_ref, x1_ref, idx_ref, cnt_ref, carry_ref):
    tm = o_ref.shape[0]

    @pl.when(pl.program_id(0) == 0)
    def _():
        carry_ref[...] = jnp.zeros_like(carry_ref)

    mq = mq_ref[...]
    mk = mk_ref[...]
    mv = mv_ref[...]
    lane = lax.broadcasted_iota(jnp.int32, (N_MEM, MEM_W), 1)
    m_out = jnp.zeros((tm, MEM_W), F32)
    for h in range(MEM_HEADS):
        in_head = (lane >= MEM_HD * h) & (lane < MEM_HD * (h + 1))
        s = _dot_nt(mq, jnp.where(in_head, mk, jnp.zeros_like(mk))) * (MEM_HD ** -0.5)
        p = jnp.exp(s - jnp.max(s, axis=-1, keepdims=True))
        l = jnp.sum(p, axis=-1, keepdims=True)
        m_out = m_out + _dot(p.astype(BF16), jnp.where(in_head, mv, jnp.zeros_like(mv))) / l
    y = _dot(o_ref[...], wo_ref[:SELF_W, :]) + _dot(m_out.astype(BF16), wo_ref[SELF_W:, :])
    x1 = _layer_norm(DN_ALPHA * x_ref[...] + y, lng_ref[...], lnb_ref[...])
    _to_slots(x1_ref, x1)

    xh, xl = _split2(x1)
    wrh = wrh_ref[...]
    logits = _dot_nt(wrh, xh) + _dot_nt(wrh, xl) + _dot_nt(wrl_ref[...], xh)
    sel = _sigmoid(logits) + br_ref[...]
    onehot = _router([sel[e:e + 1, :] for e in range(N_EXPERTS)])

    csub = lax.broadcasted_iota(jnp.int32, (CLS_PAD, tm), 0)
    oh = jnp.zeros((CLS_PAD, tm), F32)
    for c in range(N_CLASSES):
        oh = jnp.where((csub == c) & onehot[c], 1.0, oh)
    row = lax.broadcasted_iota(jnp.int32, (tm, tm), 0)
    col = lax.broadcasted_iota(jnp.int32, (tm, tm), 1)
    tri_u = jnp.where(row <= col, 1.0, 0.0).astype(BF16)
    incl = _dot(oh.astype(BF16), tri_u)
    carry = carry_ref[...]
    rank = jnp.sum(oh * (carry + incl - 1.0), axis=0, keepdims=True)
    cls = jnp.sum(oh * csub.astype(F32), axis=0, keepdims=True)
    idx_ref[0, 0:1, :] = cls.astype(jnp.int32)
    idx_ref[0, 1:2, :] = rank.astype(jnp.int32)
    carry = carry + jnp.sum(oh, axis=1, keepdims=True)
    carry_ref[...] = carry
    cnt_ref[...] = carry[:, :LANE]


def _out_block(o, mq, mkv, layer, x2d, wo, lng, lnb, wrh, wrl, br, seq, tm):
    m = x2d.shape[0]
    nt = seq // tm
    row_map = lambda i: (i, 0)
    const = lambda i: (0, 0)
    return pl.pallas_call(
        _out_kernel,
        grid=(m // tm,),
        in_specs=[pl.BlockSpec((tm, SELF_W), row_map),
                  pl.BlockSpec((tm, MEM_W), row_map),
                  pl.BlockSpec((N_MEM, MEM_W), lambda i: (i // nt, 2 * layer)),
                  pl.BlockSpec((N_MEM, MEM_W), lambda i: (i // nt, 2 * layer + 1)),
                  pl.BlockSpec((tm, D_MODEL), row_map),
                  pl.BlockSpec(wo.shape, const),
                  pl.BlockSpec(lng.shape, const),
                  pl.BlockSpec(lnb.shape, const),
                  pl.BlockSpec(wrh.shape, const),
                  pl.BlockSpec(wrl.shape, const),
                  pl.BlockSpec(br.shape, const)],
        out_specs=[pl.BlockSpec((tm * SLOT, LANE), row_map),
                   pl.BlockSpec((1, 2, tm), lambda i: (i, 0, 0)),
                   pl.BlockSpec((CLS_PAD, LANE), const)],
        out_shape=[jax.ShapeDtypeStruct((m * SLOT, LANE), F32),
                   jax.ShapeDtypeStruct((m // tm, 2, tm), jnp.int32),
                   jax.ShapeDtypeStruct((CLS_PAD, LANE), F32)],
        scratch_shapes=[pltpu.VMEM((CLS_PAD, tm), F32)],
        compiler_params=_cparams("arbitrary"),
        name="out_ln_router",
    )(o, mq, mkv, mkv, x2d, wo, lng, lnb, wrh, wrl, br)


def _moe_plan(counts, tmx, n_tiles):
    cnt = counts[:N_CLASSES, 0].astype(jnp.int32)
    nt = (cnt + (tmx - 1)) // tmx
    tend = jnp.cumsum(nt)
    offs = (tend - nt) * tmx
    n_used = tend[-1:]
    i = jnp.arange(n_tiles + 1, dtype=jnp.int32)
    ieff = jnp.minimum(i, n_used - 1)
    cls = jnp.sum((ieff[:, None] >= tend[None, :]).astype(jnp.int32), axis=1)
    fresh = jnp.concatenate([jnp.ones((1,), jnp.int32), (cls[1:] != cls[:-1]).astype(jnp.int32)])
    last_tile_row = jnp.where(nt > 0, (tend - 1) * tmx, -1)
    spare = n_used + jnp.arange(N_CLASSES, dtype=jnp.int32)
    spare_row = jnp.where(spare < n_tiles, spare * tmx, -1)
    pad = lambda a: jnp.pad(a, (0, CLS_PAD - N_CLASSES))
    return dict(offs=pad(offs), zero_rows=jnp.concatenate([last_tile_row, spare_row]), row_block=ieff,
                ea=jnp.asarray(CLASS_EA)[cls], eb=jnp.asarray(CLASS_EB)[cls], fresh=fresh, n_used=n_used)


ROW_DMA_UNROLL = 16


def _tile_dest(offs_ref, idx_ref, dest_v, dest_s, sem):
    cls = idx_ref[0, 0:1, :]
    dest = idx_ref[0, 1:2, :]
    for c in range(N_CLASSES):
        dest = dest + jnp.where(cls == c, offs_ref[c], 0)
    dest_v[...] = dest
    cp = pltpu.make_async_copy(dest_v, dest_s, sem)
    cp.start()
    cp.wait()


def _issue_row_copies(n_rows, make_copy):
    def group(g, carry):
        base = pl.multiple_of(g * ROW_DMA_UNROLL, ROW_DMA_UNROLL)
        for k in range(ROW_DMA_UNROLL):
            make_copy(base + k).start(priority=k % 2)
        return carry
    lax.fori_loop(0, n_rows // ROW_DMA_UNROLL, group, 0)


def _dispatch_kernel(offs_ref, zrow_ref, idx_ref, x_ref, xs_ref, dest_v, dest_s, zero_ref, sem_i, sem_z, sem):
    tm = x_ref.shape[0] // SLOT
    zrows = zero_ref.shape[0]

    @pl.when(pl.program_id(0) == 0)
    def _():
        zero_ref[...] = jnp.zeros_like(zero_ref)

        def zcopy(c):
            start = pl.multiple_of(zrow_ref[c] * SLOT, zrows)
            return pltpu.make_async_copy(zero_ref, xs_ref.at[pl.ds(start, zrows)], sem_z)

        for c in range(2 * N_CLASSES):
            @pl.when(zrow_ref[c] >= 0)
            def _(c=c):
                zcopy(c).start()
        for c in range(2 * N_CLASSES):
            @pl.when(zrow_ref[c] >= 0)
            def _(c=c):
                zcopy(c).wait()

    _tile_dest(offs_ref, idx_ref, dest_v, dest_s, sem_i)
    _issue_row_copies(tm, lambda r: pltpu.make_async_copy(_slot(x_ref, r), _slot(xs_ref, dest_s[0, r]), sem))
    pltpu.make_async_copy(x_ref, xs_ref.at[pl.ds(0, tm * SLOT)], sem).wait()


def _dispatch(plan, idx, x1t, tm, tmx, n_rows):
    m = x1t.shape[0] // SLOT
    return pl.pallas_call(
        _dispatch_kernel,
        grid_spec=pltpu.PrefetchScalarGridSpec(
            num_scalar_prefetch=2,
            grid=(m // tm,),
            in_specs=[pl.BlockSpec((1, 2, tm), lambda i, *_: (i, 0, 0)),
                      pl.BlockSpec((tm * SLOT, LANE), lambda i, *_: (i, 0))],
            out_specs=pl.BlockSpec(memory_space=pl.ANY),
            scratch_shapes=[pltpu.VMEM((1, tm), jnp.int32),
                            pltpu.SMEM((1, tm), jnp.int32),
                            pltpu.VMEM((tmx * SLOT, LANE), F32),
                            pltpu.SemaphoreType.DMA, pltpu.SemaphoreType.DMA, pltpu.SemaphoreType.DMA]),
        out_shape=jax.ShapeDtypeStruct((n_rows * SLOT, LANE), F32),
        compiler_params=_cparams("arbitrary"),
        name="moe_dispatch",
    )(plan["offs"], plan["zero_rows"], idx, x1t)


def _moe_ffn_kernel(rb_ref, ea_ref, eb_ref, fresh_ref, nu_ref,
                    xs_ref, wga_ref, wua_ref, wgb_ref, wub_ref, wda_ref, wdb_ref, wr_ref, lng_ref, lnb_ref,
                    ys_ref, sg_ref, su_ref, sd_ref, h_ref, x_ref, g_ref):
    i = pl.program_id(0)
    t = i - 1
    n_used = nu_ref[0]
    tmx = ys_ref.shape[0] // SLOT
    slot_a = i & 1
    slot_b = 1 - slot_a

    @pl.when((i < n_used) & (fresh_ref[i] == 1))
    def _():
        for s, (g, u) in enumerate(((wga_ref, wua_ref), (wgb_ref, wub_ref))):
            sg_ref[s] = g[0, 0].astype(BF16)
            su_ref[s] = u[0, 0].astype(BF16)

    @pl.when((t >= 0) & (t < n_used) & (fresh_ref[jnp.maximum(t, 0)] == 1))
    def _():
        for s, d in enumerate((wda_ref, wdb_ref)):
            sd_ref[s] = d[0, 0].astype(BF16)

    def load_tile():
        x = _from_slots(xs_ref, tmx)
        x_ref[slot_a] = x
        aff = _sigmoid(_dot(x.astype(BF16), wr_ref[...]))
        lane = lax.broadcasted_iota(jnp.int32, aff.shape, 1)
        g = [jnp.sum(jnp.where(lane == e_ref[i], aff, 0.0), axis=-1, keepdims=True) for e_ref in (ea_ref, eb_ref)]
        for s in range(2):
            g_ref[slot_a, s] = jnp.broadcast_to(g[s] / (g[0] + g[1]), (tmx, LANE))

    def gate_up():
        xb = x_ref[slot_a].astype(BF16)
        for s in range(2):
            hg = _dot(xb, sg_ref[s])
            hu = _dot(xb, su_ref[s])
            h_ref[slot_a, s] = (hg * _sigmoid(hg) * hu).astype(BF16)

    def down_norm():
        f = None
        for s in range(2):
            fs = g_ref[slot_b, s][:, :1] * _dot(h_ref[slot_b, s], sd_ref[s])
            f = fs if f is None else f + fs
        _to_slots(ys_ref, _layer_norm(DN_ALPHA * x_ref[slot_b] + f, lng_ref[...], lnb_ref[...]))

    @pl.when(i == 0)
    def _():
        load_tile()
        gate_up()

    @pl.when((i >= 1) & (i < n_used))
    def _():
        load_tile()
        down_norm()
        gate_up()

    @pl.when((i >= n_used) & (t < n_used))
    def _():
        down_norm()

    @pl.when(t >= n_used)
    def _():
        ys_ref[...] = jnp.zeros_like(ys_ref)


def _moe_ffn(plan, xs, layer, wg, wu, wd, wr, lng, lnb, tmx, n_tiles):
    prev = lambda i: jnp.maximum(i - 1, 0)
    rows = lambda i, rb, *_: (rb[i], 0)
    wa = lambda i, rb, ea, eb, *_: (layer, ea[i], 0, 0)
    wb = lambda i, rb, ea, eb, *_: (layer, eb[i], 0, 0)
    wa_prev = lambda i, rb, ea, eb, *_: (layer, ea[prev(i)], 0, 0)
    wb_prev = lambda i, rb, ea, eb, *_: (layer, eb[prev(i)], 0, 0)
    const = lambda i, *_: (0, 0)
    gu, dn = (1, 1, D_MODEL, D_EXPERT), (1, 1, D_EXPERT, D_MODEL)
    return pl.pallas_call(
        _moe_ffn_kernel,
        grid_spec=pltpu.PrefetchScalarGridSpec(
            num_scalar_prefetch=5,
            grid=(n_tiles + 1,),
            in_specs=[pl.BlockSpec((tmx * SLOT, LANE), rows),
                      pl.BlockSpec(gu, wa), pl.BlockSpec(gu, wa), pl.BlockSpec(gu, wb), pl.BlockSpec(gu, wb),
                      pl.BlockSpec(dn, wa_prev), pl.BlockSpec(dn, wb_prev),
                      pl.BlockSpec(wr.shape, const),
                      pl.BlockSpec(lng.shape, const), pl.BlockSpec(lnb.shape, const)],
            out_specs=pl.BlockSpec((tmx * SLOT, LANE), lambda i, *_: (prev(i), 0)),
            scratch_shapes=[pltpu.VMEM((2, D_MODEL, D_EXPERT), BF16),
                            pltpu.VMEM((2, D_MODEL, D_EXPERT), BF16),
                            pltpu.VMEM((2, D_EXPERT, D_MODEL), BF16),
                            pltpu.VMEM((2, 2, tmx, D_EXPERT), BF16),
                            pltpu.VMEM((2, tmx, D_MODEL), F32),
                            pltpu.VMEM((2, 2, tmx, LANE), F32)]),
        out_shape=jax.ShapeDtypeStruct(xs.shape, F32),
        compiler_params=_cparams("arbitrary"),
        name="moe_ffn",
    )(plan["row_block"], plan["ea"], plan["eb"], plan["fresh"], plan["n_used"],
      xs, wg, wu, wg, wu, wd, wd, wr, lng, lnb)


def _unpermute_kernel(offs_ref, idx_ref, ys_ref, o_ref, dest_v, dest_s, buf_ref, sem_i, sem):
    tm = o_ref.shape[0]
    _tile_dest(offs_ref, idx_ref, dest_v, dest_s, sem_i)
    _issue_row_copies(tm, lambda r: pltpu.make_async_copy(_slot(ys_ref, dest_s[0, r]), _slot(buf_ref, r), sem))
    pltpu.make_async_copy(ys_ref.at[pl.ds(0, tm * SLOT)], buf_ref, sem).wait()
    o_ref[...] = _from_slots(buf_ref, tm)


def _unpermute(plan, idx, ys, m, tm):
    return pl.pallas_call(
        _unpermute_kernel,
        grid_spec=pltpu.PrefetchScalarGridSpec(
            num_scalar_prefetch=1,
            grid=(m // tm,),
            in_specs=[pl.BlockSpec((1, 2, tm), lambda i, *_: (i, 0, 0)),
                      pl.BlockSpec(memory_space=pl.ANY)],
            out_specs=pl.BlockSpec((tm, D_MODEL), lambda i, *_: (i, 0)),
            scratch_shapes=[pltpu.VMEM((1, tm), jnp.int32),
                            pltpu.SMEM((1, tm), jnp.int32),
                            pltpu.VMEM((tm * SLOT, LANE), F32),
                            pltpu.SemaphoreType.DMA, pltpu.SemaphoreType.DMA]),
        out_shape=jax.ShapeDtypeStruct((m, D_MODEL), F32),
        compiler_params=_cparams("arbitrary"),
        name="moe_unpermute",
    )(plan["offs"], idx, ys)


FOX_AUG = 6


def _fox_placement():
    pl_q = np.zeros((3 * LANE, SELF_W), np.float32)
    pl_k = np.zeros((3 * LANE, SELF_W), np.float32)
    ones_q = np.zeros((1, SELF_W), np.float32)
    ones_k = np.zeros((1, SELF_W), np.float32)
    for h in range(FOX_HEADS):
        base = LANE * (h // 2) + FOX_AUG * (h % 2)
        for piece in range(3):
            pl_q[piece * LANE + h, base + piece] = 1.0
            pl_k[piece * LANE + h, base + 3 + piece] = -1.0
            ones_q[0, base + 3 + piece] = 1.0
            ones_k[0, base + piece] = 1.0
    return (jnp.asarray(pl_q, BF16), jnp.asarray(pl_k, BF16), jnp.asarray(ones_q), jnp.asarray(ones_k))


def _proj_b_kernel(x_ref, wq_ref, wog_ref, wmq_ref, wk_ref, wv_ref, wf_ref, bf_ref, plq_ref, plk_ref,
                   oq_ref, ok_ref, q_ref, k_ref, v_ref, og_ref, mq_ref, aq_ref, ak_ref, carry_ref):
    tm = x_ref.shape[0]

    @pl.when(pl.program_id(1) == 0)
    def _():
        carry_ref[...] = jnp.zeros_like(carry_ref)

    xb = x_ref[...].astype(BF16)
    for w_ref, o_ref in ((wq_ref, q_ref), (wk_ref, k_ref), (wv_ref, v_ref), (wog_ref, og_ref), (wmq_ref, mq_ref)):
        o_ref[...] = _dot(xb, w_ref[...]).astype(o_ref.dtype)

    log_f = _log_sigmoid(_dot(xb, wf_ref[...]) + bf_ref[...])
    row = lax.broadcasted_iota(jnp.int32, (tm, tm), 0)
    col = lax.broadcasted_iota(jnp.int32, (tm, tm), 1)
    tri = jnp.where(col <= row, 1.0, 0.0).astype(BF16)
    f_hi, f_mid, f_lo = _split3(log_f)
    cum = carry_ref[...] + (_dot(tri, f_hi) + _dot(tri, f_mid) + _dot(tri, f_lo))
    carry_ref[...] = cum[tm - 1:tm, :]
    c3 = jnp.concatenate(_split3(cum * LOG2E), axis=1)
    aq_ref[...] = (_dot(c3, plq_ref[...]) + oq_ref[...]).astype(aq_ref.dtype)
    ak_ref[...] = (_dot(c3, plk_ref[...]) + ok_ref[...]).astype(ak_ref.dtype)


def _proj_b(x2d, wq, wog, wmq, wk, wv, wf, bf, batch, seq, tm):
    m = batch * seq
    nt = seq // tm
    row_map = lambda b, i: (b * nt + i, 0)
    const = lambda b, i: (0, 0)
    consts = (wq, wog, wmq, wk, wv, wf, bf) + _fox_placement()
    widths = (SELF_W, SELF_W, SELF_W, SELF_W, MEM_W, SELF_W, SELF_W)
    return pl.pallas_call(
        _proj_b_kernel,
        grid=(batch, nt),
        in_specs=[pl.BlockSpec((tm, D_MODEL), row_map)] + [pl.BlockSpec(a.shape, const) for a in consts],
        out_specs=[pl.BlockSpec((tm, n), row_map) for n in widths],
        out_shape=[jax.ShapeDtypeStruct((m, n), BF16) for n in widths],
        scratch_shapes=[pltpu.VMEM((1, LANE), F32)],
        compiler_params=_cparams("parallel", "arbitrary"),
        name="proj_b",
    )(x2d, *consts)


def _pair_rms_norm(t, gain):
    lo_half = lax.broadcasted_iota(jnp.int32, t.shape, 1) < FOX_HD
    sq = t * t
    ss_lo = jnp.sum(jnp.where(lo_half, sq, 0.0), axis=-1, keepdims=True)
    ss_hi = jnp.sum(jnp.where(lo_half, 0.0, sq), axis=-1, keepdims=True)
    ss = jnp.where(lo_half, ss_lo, ss_hi)
    return t * lax.rsqrt(ss * (1.0 / FOX_HD) + RMS_EPS) * gain


SUB = 8
FOX_VROWS = FOX_HD + 16
FOX_QTILES = 8


def _sublane_all(x, op):
    shift = SUB // 2
    while shift:
        x = op(x, pltpu.roll(x, shift, 0))
        shift //= 2
    return x


def _fox_kernel(q_ref, aq_ref, k_ref, ak_ref, v_ref, og_ref, qg_ref, kg_ref, o_ref,
                ka_ref, vt_ref, qa_ref, m_ref, acc_ref, sa_ref, sb_ref):
    tq = sa_ref.shape[1]
    seq = k_ref.shape[1]
    w = pl.program_id(2)

    @pl.when(w == 0)
    def _():
        def fill(c, carry):
            rows = pl.ds(pl.multiple_of(c * tq, tq), tq)
            kn = _pair_rms_norm(k_ref[0, rows, :].astype(F32), kg_ref[...])
            ka_ref[rows, :] = jnp.concatenate([kn.astype(BF16), ak_ref[0, rows, :]], axis=1)
            vt = v_ref[0, rows, :].astype(F32).T.astype(BF16)
            for hh in range(2):
                vt_ref[hh, :FOX_HD, rows] = vt[FOX_HD * hh:FOX_HD * (hh + 1), :]
                vt_ref[hh, FOX_HD:, rows] = jnp.ones((FOX_VROWS - FOX_HD, tq), BF16)
            return carry
        lax.fori_loop(0, seq // tq, fill, 0)

    lane = lax.broadcasted_iota(jnp.int32, (tq, LANE), 1)
    krow = lax.broadcasted_iota(jnp.int32, (tq, tq), 0)
    qcol = lax.broadcasted_iota(jnp.int32, (tq, tq), 1)
    causal = krow <= qcol
    nb = tq // SUB

    def prep(i):
        rows = slice(i * tq, (i + 1) * tq)
        qn = _pair_rms_norm(q_ref[rows, :].astype(F32), qg_ref[...]) * ((FOX_HD ** -0.5) * LOG2E)
        aq = aq_ref[rows, :]
        for hh in range(2):
            feat = (lane >= FOX_HD * hh) & (lane < FOX_HD * (hh + 1))
            bias = (lane >= FOX_AUG * hh) & (lane < FOX_AUG * (hh + 1))
            qa_ref[i % 2, hh] = jnp.concatenate([jnp.where(feat, qn, 0.0).astype(BF16),
                                                 jnp.where(bias, aq, jnp.zeros_like(aq))], axis=1)
        m_ref[i % 2] = jnp.full(m_ref.shape[1:], NEG_BIG, F32)
        acc_ref[i % 2] = jnp.zeros(acc_ref.shape[1:], F32)

    def scores(i, j, s_ref):
        kt = ka_ref[pl.ds(pl.multiple_of(j * tq, tq), tq), :]
        for hh in range(2):
            s_ref[hh] = _dot_nt(kt, qa_ref[i % 2, hh])

    def consume(i, j, s_ref, masked):
        cols = pl.ds(pl.multiple_of(j * tq, tq), tq)
        for hh in range(2):
            s = s_ref[hh]
            if masked:
                s = jnp.where(causal, s, NEG_BIG)
            s3 = s.reshape(nb, SUB, tq)
            m_old = m_ref[i % 2, hh]
            m_new = jnp.maximum(m_old, _sublane_all(jnp.max(s3, axis=0), jnp.maximum))
            alpha = jnp.exp2(m_old - m_new)
            p = jnp.exp2(s3 - m_new[None]).reshape(tq, tq).astype(BF16)
            pv = _dot(vt_ref[hh, :, cols], p)
            acc3 = acc_ref[i % 2, hh].reshape(FOX_VROWS // SUB, SUB, tq) * alpha[None]
            acc_ref[i % 2, hh] = acc3.reshape(FOX_VROWS, tq) + pv
            m_ref[i % 2, hh] = m_new

    def finish(i):
        rows = slice(i * tq, (i + 1) * tq)
        o_t = []
        for hh in range(2):
            l = acc_ref[i % 2, hh, FOX_HD:FOX_HD + SUB, :]
            o_t.append((acc_ref[i % 2, hh, :FOX_HD, :].reshape(FOX_HD // SUB, SUB, tq) / l[None]).reshape(FOX_HD, tq))
        o = jnp.concatenate(o_t, axis=0).T
        og = og_ref[rows, :].astype(F32)
        o_ref[rows, :] = (o * _sigmoid(og)).astype(o_ref.dtype)

    cur, other = sa_ref, sb_ref
    prep(0)
    scores(0, 0, cur)
    for i in range(FOX_QTILES):
        qi = FOX_QTILES * w + i

        def pair(t, carry, i=i, cur=cur, other=other):
            j = 2 * t
            scores(i, j + 1, other)
            consume(i, j, cur, False)
            scores(i, j + 2, cur)
            consume(i, j + 1, other, False)
            return carry

        lax.fori_loop(0, qi // 2, pair, 0)
        last = i == FOX_QTILES - 1
        if i % 2 == 0:
            if not last:
                prep(i + 1)
                scores(i + 1, 0, other)
            consume(i, qi, cur, True)
            cur, other = other, cur
        else:
            scores(i, qi, other)
            consume(i, qi - 1, cur, False)
            if not last:
                prep(i + 1)
                scores(i + 1, 0, cur)
            consume(i, qi, other, True)
        finish(i)


def _fox(q, aq, k, ak, v, og, qg2, kg2, batch, seq, tq):
    tg = FOX_QTILES * tq
    ng = seq // tg
    m = batch * seq
    k3, ak3, v3 = (a.reshape(batch, seq, SELF_W) for a in (k, ak, v))
    tile_map = lambda b, p, i: (b * ng + i, p)
    seq_map = lambda b, p, i: (b, 0, p)
    const = lambda b, p, i: (0, 0)
    return pl.pallas_call(
        _fox_kernel,
        grid=(batch, FOX_PAIRS, ng),
        in_specs=[pl.BlockSpec((tg, LANE), tile_map),
                  pl.BlockSpec((tg, LANE), tile_map),
                  pl.BlockSpec((1, seq, LANE), seq_map),
                  pl.BlockSpec((1, seq, LANE), seq_map),
                  pl.BlockSpec((1, seq, LANE), seq_map),
                  pl.BlockSpec((tg, LANE), tile_map),
                  pl.BlockSpec((1, LANE), const),
                  pl.BlockSpec((1, LANE), const)],
        out_specs=pl.BlockSpec((tg, LANE), tile_map),
        out_shape=jax.ShapeDtypeStruct((m, SELF_W), BF16),
        scratch_shapes=[pltpu.VMEM((seq, 2 * LANE), BF16),
                        pltpu.VMEM((2, FOX_VROWS, seq), BF16),
                        pltpu.VMEM((2, 2, tq, 2 * LANE), BF16),
                        pltpu.VMEM((2, 2, SUB, tq), F32),
                        pltpu.VMEM((2, 2, FOX_VROWS, tq), F32),
                        pltpu.VMEM((2, tq, tq), F32),
                        pltpu.VMEM((2, tq, tq), F32)],
        compiler_params=_cparams("parallel", "parallel", "arbitrary"),
        name="fox",
    )(q, aq, k3, ak3, v3, og, qg2, kg2)


def _pad_heads(w, n_heads, d, d_pad):
    lead = w.shape[:-1]
    w = w.reshape(lead + (n_heads, d))
    w = jnp.pad(w, [(0, 0)] * len(lead) + [(0, 0), (0, d_pad - d)])
    return w.reshape(lead + (n_heads * d_pad,))


def _pad_last(w, n):
    return jnp.pad(w, [(0, 0)] * (w.ndim - 1) + [(0, n - w.shape[-1])])


def kernel(x, mem, w_in_a, w_gate_up_a, b_gate_a, gla_norm_g, w_in_b, q_norm_g, w_kv_shared, b_forget, k_norm_g, w_mem_kv, w_out, ln_mix_g, ln_mix_b, ln_ffn_g, ln_ffn_b, w_router, b_router, w_exp_gate, w_exp_up, w_exp_down):
    batch, seq, d = x.shape
    m = batch * seq
    tm = 512
    x2d = x.reshape(m, d)

    wa = w_in_a[0]
    s0, s1, s2, s3, s4 = (GLA_KDIM, 2 * GLA_KDIM, 2 * GLA_KDIM + SELF_W,
                          2 * GLA_KDIM + SELF_W + GLA_GATE_RANK, 2 * GLA_KDIM + 2 * SELF_W + GLA_GATE_RANK)
    w_a = jnp.concatenate([
        _pad_heads(wa[:, :s0], GLA_HEADS, GLA_DK, GLA_DK_PAD),
        _pad_heads(wa[:, s0:s1], GLA_HEADS, GLA_DK, GLA_DK_PAD),
        wa[:, s1:s2], wa[:, s3:s4], wa[:, s4:], _pad_last(wa[:, s2:s3], LANE)], axis=1).astype(BF16)
    wgu = _pad_heads(w_gate_up_a[0], GLA_HEADS, GLA_DK, GLA_DK_PAD)
    wgu = jnp.pad(wgu, ((0, LANE - GLA_GATE_RANK), (0, 0))).astype(BF16)
    bg = _pad_heads(b_gate_a[0], GLA_HEADS, GLA_DK, GLA_DK_PAD).reshape(1, GLA_KPAD)
    gn = jnp.tile(gla_norm_g[0], GLA_HEADS).reshape(1, SELF_W)

    wb = w_in_b[0]
    wq, wog, wmq = (wb[:, :SELF_W].astype(BF16), wb[:, SELF_W:2 * SELF_W].astype(BF16),
                    wb[:, 2 * SELF_W:].astype(BF16))
    wk, wv = w_kv_shared[:, :SELF_W].astype(BF16), w_kv_shared[:, SELF_W:2 * SELF_W].astype(BF16)
    wf = _pad_last(w_kv_shared[:, 2 * SELF_W:], LANE).astype(BF16)
    bf = _pad_last(b_forget, LANE).reshape(1, LANE)
    qg2 = jnp.tile(q_norm_g[0], 2).reshape(1, LANE)
    kg2 = jnp.tile(k_norm_g, 2).reshape(1, LANE)

    w_mkv = jnp.concatenate([w_mem_kv[l] for l in range(DEPTH)], axis=1).astype(BF16)
    wo = w_out.astype(BF16)
    wr_t = w_router.T
    wrh = wr_t.astype(BF16)
    wrl = (wr_t - wrh.astype(F32)).astype(BF16)
    br = b_router.reshape(N_EXPERTS, 1)
    row = lambda a: a.reshape(1, d)

    mkv = _mem_kv(mem.reshape(batch * N_MEM, d), w_mkv)

    tmx = 256
    n_tiles = m // tmx + N_CLASSES
    n_rows = n_tiles * tmx

    wr_pad = _pad_last(w_router, LANE).astype(BF16)

    def tail(o, mq, xin, layer):
        x1t, idx, counts = _out_block(o, mq, mkv, layer, xin, wo[layer], row(ln_mix_g[layer]), row(ln_mix_b[layer]),
                                      wrh, wrl, br, seq, tm)
        plan = _moe_plan(counts, tmx, n_tiles)
        xs = _dispatch(plan, idx, x1t, tm, tmx, n_rows)
        ys = _moe_ffn(plan, xs, layer, w_exp_gate, w_exp_up, w_exp_down, wr_pad,
                      row(ln_ffn_g[layer]), row(ln_ffn_b[layer]), tmx, n_tiles)
        return _unpermute(plan, idx, ys, m, tm)

    q, k, v, r, mq, g = _proj_a(x2d, w_a, tm)
    o = _gla(q, k, v, r, g, wgu, bg, gn, batch, seq, tm)
    xa = tail(o, mq, x2d, 0)

    qb, kb, vb, og, mqb, aq, ak = _proj_b(xa, wq, wog, wmq, wk, wv, wf, bf, batch, seq, tm)
    ob = _fox(qb, aq, kb, ak, vb, og, qg2, kg2, batch, seq, tm)
    xb = tail(ob, mqb, xa, 1)
    return xb.reshape(batch, seq, d)
```

```python
import functools
import math

import jax
import jax.numpy as jnp
import numpy as np
from jax import lax
from jax.experimental import pallas as pl
from jax.experimental.pallas import tpu as pltpu

F32 = jnp.float32
BF16 = jnp.bfloat16

D_MODEL = 1024
DEPTH = 2
CHUNK = 64
N_MEM = 256
MEM_HEADS = 4
MEM_W = D_MODEL // 4
MEM_HD = MEM_W // MEM_HEADS
SELF_W = D_MODEL - MEM_W
GLA_HEADS = 4
GLA_KDIM = SELF_W // 2
GLA_DK = GLA_KDIM // GLA_HEADS
GLA_DV = SELF_W // GLA_HEADS
GLA_GATE_RANK = 16
GLA_TAU = 16.0
FOX_HD = 64
FOX_HEADS = SELF_W // FOX_HD
N_EXPERTS = 16
N_GROUPS = 4
EXPERTS_PER_GROUP = N_EXPERTS // N_GROUPS
D_EXPERT = D_MODEL // 2
DN_ALPHA = (2.0 * DEPTH) ** 0.25
LN_EPS = 1e-5
RMS_EPS = 1e-6

LANE = 128
GLA_DK_PAD = LANE
GLA_KPAD = GLA_HEADS * GLA_DK_PAD
GLA_VBLOCKS = SELF_W // LANE
GLA_HEAD_BLOCKS = ((0, 1), (1, 2), (3, 4), (4, 5))
FOX_PAIRS = FOX_HEADS // 2
LOG2E = math.log2(math.e)
NEG_BIG = -1e30
VMEM_LIMIT_BYTES = 48 * 1024 * 1024

NT_DIMS = (((1,), (1,)), ((), ()))
TN_DIMS = (((0,), (0,)), ((), ()))


def _cparams(*sem):
    return pltpu.CompilerParams(dimension_semantics=sem, vmem_limit_bytes=VMEM_LIMIT_BYTES)


def _dot(a, b):
    return jnp.dot(a, b, preferred_element_type=F32)


def _dot_nt(a, b):
    return lax.dot_general(a, b, NT_DIMS, preferred_element_type=F32)


def _dot_tn(a, b):
    return lax.dot_general(a, b, TN_DIMS, preferred_element_type=F32)


def _log_sigmoid(z):
    return jnp.minimum(z, 0.0) - jnp.log(1.0 + jnp.exp(-jnp.abs(z)))


def _sigmoid(z):
    return 1.0 / (1.0 + jnp.exp(-z))


def _split2(x):
    hi = x.astype(BF16)
    lo = (x - hi.astype(F32)).astype(BF16)
    return hi, lo


def _split3(x):
    hi = x.astype(BF16)
    r1 = x - hi.astype(F32)
    mid = r1.astype(BF16)
    lo = (r1 - mid.astype(F32)).astype(BF16)
    return hi, mid, lo


def _layer_norm(x, g, b):
    mu = jnp.mean(x, axis=-1, keepdims=True)
    xc = x - mu
    var = jnp.mean(xc * xc, axis=-1, keepdims=True)
    return xc * lax.rsqrt(var + LN_EPS) * g + b


def _proj_a_kernel(x_ref, w_ref, q_ref, k_ref, v_ref, r_ref, mq_ref, g_ref):
    xb = x_ref[...].astype(BF16)
    c = 0
    for ref in (q_ref, k_ref, v_ref, r_ref, mq_ref, g_ref):
        n = ref.shape[1]
        ref[...] = _dot(xb, w_ref[:, c:c + n]).astype(ref.dtype)
        c += n


def _proj_a(x2d, w, tm):
    m = x2d.shape[0]
    widths = (GLA_KPAD, GLA_KPAD, SELF_W, SELF_W, MEM_W, LANE)
    assert w.shape == (D_MODEL, sum(widths))
    return pl.pallas_call(
        _proj_a_kernel,
        grid=(m // tm,),
        in_specs=[pl.BlockSpec((tm, D_MODEL), lambda i: (i, 0)),
                  pl.BlockSpec(w.shape, lambda i: (0, 0))],
        out_specs=[pl.BlockSpec((tm, n), lambda i: (i, 0)) for n in widths],
        out_shape=[jax.ShapeDtypeStruct((m, n), BF16) for n in widths],
        compiler_params=_cparams("parallel"),
        name="proj_a",
    )(x2d, w)


def _gla_kernel(q_ref, k_ref, v_ref, r_ref, g_ref, wgu_ref, bg_ref, gn_ref, o_ref, st_ref, la_ref):
    tg = q_ref.shape[0]

    @pl.when(pl.program_id(1) == 0)
    def _():
        st_ref[...] = jnp.zeros_like(st_ref)

    z = _dot(g_ref[...], wgu_ref[...]) + bg_ref[...]
    la_ref[...] = _log_sigmoid(z) * (1.0 / GLA_TAU)

    row = lax.broadcasted_iota(jnp.int32, (CHUNK, CHUNK), 0)
    col = lax.broadcasted_iota(jnp.int32, (CHUNK, CHUNK), 1)
    causal = col <= row
    tri = jnp.where(causal, 1.0, 0.0).astype(BF16)
    lo_half = lax.broadcasted_iota(jnp.int32, (CHUNK, LANE), 1) < (LANE // 2)
    scale = GLA_DK ** -0.5

    def chunk(c, carry):
        rows = pl.ds(c * CHUNK, CHUNK)
        la_hi, la_lo = _split2(la_ref[rows, :])
        b = _dot(tri, la_hi) + _dot(tri, la_lo)
        bl = b[CHUNK - 1:CHUNK, :]
        qc = q_ref[rows, :].astype(F32) * scale
        kc = k_ref[rows, :].astype(F32)
        qd = (qc * jnp.exp(b)).astype(BF16)
        ki = (kc * jnp.exp(-b)).astype(BF16)
        ke = (kc * jnp.exp(bl - b)).astype(BF16)
        dec = jnp.exp(bl)
        unit = []
        for h in range(GLA_HEADS):
            sl = slice(GLA_DK_PAD * h, GLA_DK_PAD * (h + 1))
            qh, kih, keh = qd[:, sl], ki[:, sl], ke[:, sl]
            att = jnp.where(causal, _dot_nt(qh, kih), 0.0).astype(BF16)
            for t, blk in enumerate(GLA_HEAD_BLOCKS[h]):
                u = 2 * h + t
                vb = v_ref[rows, LANE * blk:LANE * (blk + 1)]
                st = st_ref[u]
                unit.append(_dot(att, vb) + _dot_nt(qh, st.astype(BF16)))
                st_ref[u] = st * dec[:, sl] + _dot_tn(vb, keh)
        o_blk = [unit[0], jnp.where(lo_half, unit[1], unit[2]), unit[3],
                 unit[4], jnp.where(lo_half, unit[5], unit[6]), unit[7]]
        sq = [o * o for o in o_blk]
        full = [jnp.sum(s, axis=-1, keepdims=True) for s in sq]
        lo1 = jnp.sum(jnp.where(lo_half, sq[1], 0.0), axis=-1, keepdims=True)
        hi1 = jnp.sum(jnp.where(lo_half, 0.0, sq[1]), axis=-1, keepdims=True)
        lo4 = jnp.sum(jnp.where(lo_half, sq[4], 0.0), axis=-1, keepdims=True)
        hi4 = jnp.sum(jnp.where(lo_half, 0.0, sq[4]), axis=-1, keepdims=True)
        ss = [full[0] + lo1, hi1 + full[2], full[3] + lo4, hi4 + full[5]]
        inv = [lax.rsqrt(s * (1.0 / GLA_DV) + RMS_EPS) for s in ss]
        inv_blk = [inv[0], jnp.where(lo_half, inv[0], inv[1]), inv[1],
                   inv[2], jnp.where(lo_half, inv[2], inv[3]), inv[3]]
        for blk in range(GLA_VBLOCKS):
            cs = slice(LANE * blk, LANE * (blk + 1))
            rg = r_ref[rows, cs].astype(F32)
            y = o_blk[blk] * inv_blk[blk] * gn_ref[:, cs]
            o_ref[rows, cs] = (y * (rg * _sigmoid(rg))).astype(o_ref.dtype)
        return carry

    for c in range(tg // CHUNK):
        chunk(c, 0)


def _gla(q, k, v, r, g, wgu, bg, gn, batch, seq, tg):
    m = batch * seq
    nt = seq // tg
    row_map = lambda b, i: (b * nt + i, 0)
    const = lambda b, i: (0, 0)
    return pl.pallas_call(
        _gla_kernel,
        grid=(batch, nt),
        in_specs=[pl.BlockSpec((tg, GLA_KPAD), row_map),
                  pl.BlockSpec((tg, GLA_KPAD), row_map),
                  pl.BlockSpec((tg, SELF_W), row_map),
                  pl.BlockSpec((tg, SELF_W), row_map),
                  pl.BlockSpec((tg, LANE), row_map),
                  pl.BlockSpec(wgu.shape, const),
                  pl.BlockSpec(bg.shape, const),
                  pl.BlockSpec(gn.shape, const)],
        out_specs=pl.BlockSpec((tg, SELF_W), row_map),
        out_shape=jax.ShapeDtypeStruct((m, SELF_W), BF16),
        scratch_shapes=[pltpu.VMEM((2 * GLA_HEADS, LANE, GLA_DK_PAD), F32),
                        pltpu.VMEM((tg, GLA_KPAD), F32)],
        compiler_params=_cparams("parallel", "arbitrary"),
        name="gla",
    )(q, k, v, r, g, wgu, bg, gn)


def _mem_kv_kernel(m_ref, w_ref, o_ref):
    o_ref[...] = _dot(m_ref[...].astype(BF16), w_ref[...]).astype(o_ref.dtype)


def _mem_kv(mem2d, w):
    m, n = mem2d.shape[0], w.shape[1]
    tm = N_MEM
    return pl.pallas_call(
        _mem_kv_kernel,
        grid=(m // tm,),
        in_specs=[pl.BlockSpec((tm, D_MODEL), lambda i: (i, 0)),
                  pl.BlockSpec(w.shape, lambda i: (0, 0))],
        out_specs=pl.BlockSpec((tm, n), lambda i: (i, 0)),
        out_shape=jax.ShapeDtypeStruct((m, n), BF16),
        compiler_params=_cparams("parallel"),
        name="mem_kv",
    )(mem2d, w)


def _top2_sum(a, b, c, d):
    p, q = jnp.maximum(a, b), jnp.minimum(a, b)
    r, s = jnp.maximum(c, d), jnp.minimum(c, d)
    return jnp.maximum(p, r) + jnp.maximum(jnp.minimum(p, r), jnp.maximum(q, s))


PAIRS = tuple((i, j) for i in range(EXPERTS_PER_GROUP) for j in range(i + 1, EXPERTS_PER_GROUP))
N_CLASSES = N_GROUPS * len(PAIRS)
CLS_PAD = 32
SLOT = D_MODEL // LANE


def _to_slots(ref, x):
    rows = x.shape[0]
    for j in range(SLOT):
        ref[pl.ds(j, rows, stride=SLOT), :] = x[:, LANE * j:LANE * (j + 1)]


def _from_slots(ref, rows):
    return jnp.concatenate([ref[pl.ds(j, rows, stride=SLOT), :] for j in range(SLOT)], axis=1)


def _slot(ref, r):
    return ref.at[pl.ds(pl.multiple_of(r * SLOT, SLOT), SLOT)]


CLASS_EA = np.array([EXPERTS_PER_GROUP * (c // len(PAIRS)) + PAIRS[c % len(PAIRS)][0] for c in range(N_CLASSES)], np.int32)
CLASS_EB = np.array([EXPERTS_PER_GROUP * (c // len(PAIRS)) + PAIRS[c % len(PAIRS)][1] for c in range(N_CLASSES)], np.int32)


def _router(sel):
    score = [_top2_sum(*sel[EXPERTS_PER_GROUP * g:EXPERTS_PER_GROUP * (g + 1)]) for g in range(N_GROUPS)]
    picked = []
    for g in range(N_GROUPS):
        ok = None
        for i in range(N_GROUPS):
            if i == g:
                continue
            c = (score[g] > score[i]) if i < g else (score[g] >= score[i])
            ok = c if ok is None else (ok & c)
        picked.append(ok)
    chosen = []
    for g in range(N_GROUPS):
        for j in range(EXPERTS_PER_GROUP):
            ej = EXPERTS_PER_GROUP * g + j
            rank = None
            for i in range(EXPERTS_PER_GROUP):
                if i == j:
                    continue
                ei = EXPERTS_PER_GROUP * g + i
                beats = (sel[ei] >= sel[ej]) if i < j else (sel[ei] > sel[ej])
                beats = jnp.where(beats, 1.0, 0.0)
                rank = beats if rank is None else rank + beats
            chosen.append(picked[g] & (rank < 1.5))
    return [chosen[CLASS_EA[c]] & chosen[CLASS_EB[c]] for c in range(N_CLASSES)]


def _out_kernel(o_ref, mq_ref, mk_ref, mv_ref, x_ref, wo_ref, lng_ref, lnb_ref,
                wrh_ref, wr2_ref, br_ref, x1_ref, idx_ref, cnt_ref, carry_ref):
    tm = o_ref.shape[0]

    @pl.when(pl.program_id(0) == 0)
    def _():
        carry_ref[...] = jnp.zeros_like(carry_ref)

    mq = mq_ref[...]
    mk = mk_ref[...]
    mv = mv_ref[...]
    lane = lax.broadcasted_iota(jnp.int32, (N_MEM, MEM_W), 1)
    m_out = jnp.zeros((tm, MEM_W), F32)
    for h in range(MEM_HEADS):
        in_head = (lane >= MEM_HD * h) & (lane < MEM_HD * (h + 1))
        s = _dot_nt(mq, jnp.where(in_head, mk, jnp.zeros_like(mk))) * (MEM_HD ** -0.5)
        p = jnp.exp(s - jnp.max(s, axis=-1, keepdims=True))
        l = jnp.sum(p, axis=-1, keepdims=True)
        m_out = m_out + _dot(p.astype(BF16), jnp.where(in_head, mv, jnp.zeros_like(mv))) / l
    y = _dot(o_ref[...], wo_ref[:SELF_W, :]) + _dot(m_out.astype(BF16), wo_ref[SELF_W:, :])
    x1 = _layer_norm(DN_ALPHA * x_ref[...] + y, lng_ref[...], lnb_ref[...])
    _to_slots(x1_ref, x1)

    xh, xl = _split2(x1)
    hh_lh = _dot_nt(wr2_ref[...], xh)
    logits = hh_lh[:N_EXPERTS, :] + hh_lh[N_EXPERTS:, :] + _dot_nt(wrh_ref[...], xl)
    sel = _sigmoid(logits) + br_ref[...]
    onehot = _router([sel[e:e + 1, :] for e in range(N_EXPERTS)])

    csub = lax.broadcasted_iota(jnp.int32, (CLS_PAD, tm), 0)
    oh = jnp.zeros((CLS_PAD, tm), F32)
    for c in range(N_CLASSES):
        oh = jnp.where((csub == c) & onehot[c], 1.0, oh)
    row = lax.broadcasted_iota(jnp.int32, (tm, tm), 0)
    col = lax.broadcasted_iota(jnp.int32, (tm, tm), 1)
    tri_u = jnp.where(row <= col, 1.0, 0.0).astype(BF16)
    incl = _dot(oh.astype(BF16), tri_u)
    carry = carry_ref[...]
    rank = jnp.sum(oh * (carry + incl - 1.0), axis=0, keepdims=True)
    cls = jnp.sum(oh * csub.astype(F32), axis=0, keepdims=True)
    idx_ref[0, 0:1, :] = cls.astype(jnp.int32)
    idx_ref[0, 1:2, :] = rank.astype(jnp.int32)
    carry = carry + jnp.sum(oh, axis=1, keepdims=True)
    carry_ref[...] = carry
    cnt_ref[...] = carry[:, :LANE]


def _out_block(o, mq, mkv, layer, x2d, wo, lng, lnb, wrh, wr2, br, seq, tm):
    m = x2d.shape[0]
    nt = seq // tm
    row_map = lambda i: (i, 0)
    const = lambda i: (0, 0)
    return pl.pallas_call(
        _out_kernel,
        grid=(m // tm,),
        in_specs=[pl.BlockSpec((tm, SELF_W), row_map),
                  pl.BlockSpec((tm, MEM_W), row_map),
                  pl.BlockSpec((N_MEM, MEM_W), lambda i: (i // nt, 2 * layer)),
                  pl.BlockSpec((N_MEM, MEM_W), lambda i: (i // nt, 2 * layer + 1)),
                  pl.BlockSpec((tm, D_MODEL), row_map),
                  pl.BlockSpec(wo.shape, const),
                  pl.BlockSpec(lng.shape, const),
                  pl.BlockSpec(lnb.shape, const),
                  pl.BlockSpec(wrh.shape, const),
                  pl.BlockSpec(wr2.shape, const),
                  pl.BlockSpec(br.shape, const)],
        out_specs=[pl.BlockSpec((tm * SLOT, LANE), row_map),
                   pl.BlockSpec((1, 2, tm), lambda i: (i, 0, 0)),
                   pl.BlockSpec((CLS_PAD, LANE), const)],
        out_shape=[jax.ShapeDtypeStruct((m * SLOT, LANE), F32),
                   jax.ShapeDtypeStruct((m // tm, 2, tm), jnp.int32),
                   jax.ShapeDtypeStruct((CLS_PAD, LANE), F32)],
        scratch_shapes=[pltpu.VMEM((CLS_PAD, tm), F32)],
        compiler_params=_cparams("arbitrary"),
        name="out_ln_router",
    )(o, mq, mkv, mkv, x2d, wo, lng, lnb, wrh, wr2, br)


def _moe_plan(counts, tmx, n_tiles):
    cnt = counts[:N_CLASSES, 0].astype(jnp.int32)
    nt = (cnt + (tmx - 1)) // tmx
    tend = jnp.cumsum(nt)
    offs = (tend - nt) * tmx
    n_used = tend[-1:]
    i = jnp.arange(n_tiles + 1, dtype=jnp.int32)
    ieff = jnp.minimum(i, n_used - 1)
    cls = jnp.sum((ieff[:, None] >= tend[None, :]).astype(jnp.int32), axis=1)
    fresh = jnp.concatenate([jnp.ones((1,), jnp.int32), (cls[1:] != cls[:-1]).astype(jnp.int32)])
    last_tile_row = jnp.where(nt > 0, (tend - 1) * tmx, -1)
    spare = n_used + jnp.arange(N_CLASSES, dtype=jnp.int32)
    spare_row = jnp.where(spare < n_tiles, spare * tmx, -1)
    pad = lambda a: jnp.pad(a, (0, CLS_PAD - N_CLASSES))
    return dict(offs=pad(offs), zero_rows=jnp.concatenate([last_tile_row, spare_row]), row_block=ieff,
                ea=jnp.asarray(CLASS_EA)[cls], eb=jnp.asarray(CLASS_EB)[cls], fresh=fresh, n_used=n_used)


ROW_DMA_UNROLL = 16


def _tile_dest(offs_ref, idx_ref, dest_v, dest_s, sem):
    cls = idx_ref[0, 0:1, :]
    dest = idx_ref[0, 1:2, :]
    for c in range(N_CLASSES):
        dest = dest + jnp.where(cls == c, offs_ref[c], 0)
    dest_v[...] = dest
    cp = pltpu.make_async_copy(dest_v, dest_s, sem)
    cp.start()
    cp.wait()


def _issue_row_copies(n_rows, make_copy):
    def group(g, carry):
        base = pl.multiple_of(g * ROW_DMA_UNROLL, ROW_DMA_UNROLL)
        for k in range(ROW_DMA_UNROLL):
            make_copy(base + k).start(priority=k % 2)
        return carry
    lax.fori_loop(0, n_rows // ROW_DMA_UNROLL, group, 0)


def _dispatch_kernel(offs_ref, zrow_ref, idx_ref, x_ref, xs_ref, dest_v, dest_s, zero_ref, sem_i, sem_z, sem):
    tm = x_ref.shape[0] // SLOT
    zrows = zero_ref.shape[0]

    @pl.when(pl.program_id(0) == 0)
    def _():
        zero_ref[...] = jnp.zeros_like(zero_ref)

        def zcopy(c):
            start = pl.multiple_of(zrow_ref[c] * SLOT, zrows)
            return pltpu.make_async_copy(zero_ref, xs_ref.at[pl.ds(start, zrows)], sem_z)

        for c in range(2 * N_CLASSES):
            @pl.when(zrow_ref[c] >= 0)
            def _(c=c):
                zcopy(c).start()
        for c in range(2 * N_CLASSES):
            @pl.when(zrow_ref[c] >= 0)
            def _(c=c):
                zcopy(c).wait()

    _tile_dest(offs_ref, idx_ref, dest_v, dest_s, sem_i)
    _issue_row_copies(tm, lambda r: pltpu.make_async_copy(_slot(x_ref, r), _slot(xs_ref, dest_s[0, r]), sem))
    pltpu.make_async_copy(x_ref, xs_ref.at[pl.ds(0, tm * SLOT)], sem).wait()


def _dispatch(plan, idx, x1t, tm, tmx, n_rows):
    m = x1t.shape[0] // SLOT
    return pl.pallas_call(
        _dispatch_kernel,
        grid_spec=pltpu.PrefetchScalarGridSpec(
            num_scalar_prefetch=2,
            grid=(m // tm,),
            in_specs=[pl.BlockSpec((1, 2, tm), lambda i, *_: (i, 0, 0)),
                      pl.BlockSpec((tm * SLOT, LANE), lambda i, *_: (i, 0))],
            out_specs=pl.BlockSpec(memory_space=pl.ANY),
            scratch_shapes=[pltpu.VMEM((1, tm), jnp.int32),
                            pltpu.SMEM((1, tm), jnp.int32),
                            pltpu.VMEM((tmx * SLOT, LANE), F32),
                            pltpu.SemaphoreType.DMA, pltpu.SemaphoreType.DMA, pltpu.SemaphoreType.DMA]),
        out_shape=jax.ShapeDtypeStruct((n_rows * SLOT, LANE), F32),
        compiler_params=_cparams("arbitrary"),
        name="moe_dispatch",
    )(plan["offs"], plan["zero_rows"], idx, x1t)


def _moe_ffn_kernel(rb_ref, ea_ref, eb_ref, fresh_ref, nu_ref,
                    xs_ref, wga_ref, wua_ref, wgb_ref, wub_ref, wda_ref, wdb_ref, wr_ref, lng_ref, lnb_ref,
                    ys_ref, sg_ref, su_ref, sd_ref, h_ref, x_ref, g_ref):
    i = pl.program_id(0)
    t = i - 1
    n_used = nu_ref[0]
    tmx = ys_ref.shape[0] // SLOT
    slot_a = i & 1
    slot_b = 1 - slot_a

    @pl.when((i < n_used) & (fresh_ref[i] == 1))
    def _():
        for s, (g, u) in enumerate(((wga_ref, wua_ref), (wgb_ref, wub_ref))):
            sg_ref[s] = g[0, 0].astype(BF16)
            su_ref[s] = u[0, 0].astype(BF16)

    @pl.when((t >= 0) & (t < n_used) & (fresh_ref[jnp.maximum(t, 0)] == 1))
    def _():
        for s, d in enumerate((wda_ref, wdb_ref)):
            sd_ref[s] = d[0, 0].astype(BF16)

    def load_tile():
        x = _from_slots(xs_ref, tmx)
        x_ref[slot_a] = x
        aff = _sigmoid(_dot(x.astype(BF16), wr_ref[...]))
        lane = lax.broadcasted_iota(jnp.int32, aff.shape, 1)
        g = [jnp.sum(jnp.where(lane == e_ref[i], aff, 0.0), axis=-1, keepdims=True) for e_ref in (ea_ref, eb_ref)]
        for s in range(2):
            g_ref[slot_a, s] = jnp.broadcast_to(g[s] / (g[0] + g[1]), (tmx, LANE))

    def gate_up():
        xb = x_ref[slot_a].astype(BF16)
        for s in range(2):
            hg = _dot(xb, sg_ref[s])
            hu = _dot(xb, su_ref[s])
            h_ref[slot_a, s] = (hg * _sigmoid(hg) * hu).astype(BF16)

    def down_norm():
        f = None
        for s in range(2):
            fs = g_ref[slot_b, s][:, :1] * _dot(h_ref[slot_b, s], sd_ref[s])
            f = fs if f is None else f + fs
        _to_slots(ys_ref, _layer_norm(DN_ALPHA * x_ref[slot_b] + f, lng_ref[...], lnb_ref[...]))

    @pl.when(i == 0)
    def _():
        load_tile()
        gate_up()

    @pl.when((i >= 1) & (i < n_used))
    def _():
        load_tile()
        down_norm()
        gate_up()

    @pl.when((i >= n_used) & (t < n_used))
    def _():
        down_norm()

    @pl.when(t >= n_used)
    def _():
        ys_ref[...] = jnp.zeros_like(ys_ref)


def _moe_ffn(plan, xs, layer, wg, wu, wd, wr, lng, lnb, tmx, n_tiles):
    prev = lambda i: jnp.maximum(i - 1, 0)
    rows = lambda i, rb, *_: (rb[i], 0)
    wa = lambda i, rb, ea, eb, *_: (layer, ea[i], 0, 0)
    wb = lambda i, rb, ea, eb, *_: (layer, eb[i], 0, 0)
    wa_prev = lambda i, rb, ea, eb, *_: (layer, ea[prev(i)], 0, 0)
    wb_prev = lambda i, rb, ea, eb, *_: (layer, eb[prev(i)], 0, 0)
    const = lambda i, *_: (0, 0)
    gu, dn = (1, 1, D_MODEL, D_EXPERT), (1, 1, D_EXPERT, D_MODEL)
    return pl.pallas_call(
        _moe_ffn_kernel,
        grid_spec=pltpu.PrefetchScalarGridSpec(
            num_scalar_prefetch=5,
            grid=(n_tiles + 1,),
            in_specs=[pl.BlockSpec((tmx * SLOT, LANE), rows),
                      pl.BlockSpec(gu, wa), pl.BlockSpec(gu, wa), pl.BlockSpec(gu, wb), pl.BlockSpec(gu, wb),
                      pl.BlockSpec(dn, wa_prev), pl.BlockSpec(dn, wb_prev),
                      pl.BlockSpec(wr.shape, const),
                      pl.BlockSpec(lng.shape, const), pl.BlockSpec(lnb.shape, const)],
            out_specs=pl.BlockSpec((tmx * SLOT, LANE), lambda i, *_: (prev(i), 0)),
            scratch_shapes=[pltpu.VMEM((2, D_MODEL, D_EXPERT), BF16),
                            pltpu.VMEM((2, D_MODEL, D_EXPERT), BF16),
                            pltpu.VMEM((2, D_EXPERT, D_MODEL), BF16),
                            pltpu.VMEM((2, 2, tmx, D_EXPERT), BF16),
                            pltpu.VMEM((2, tmx, D_MODEL), F32),
                            pltpu.VMEM((2, 2, tmx, LANE), F32)]),
        out_shape=jax.ShapeDtypeStruct(xs.shape, F32),
        compiler_params=_cparams("arbitrary"),
        name="moe_ffn",
    )(plan["row_block"], plan["ea"], plan["eb"], plan["fresh"], plan["n_used"],
      xs, wg, wu, wg, wu, wd, wd, wr, lng, lnb)


def _unpermute_kernel(offs_ref, idx_ref, ys_ref, o_ref, dest_v, dest_s, buf_ref, sem_i, sem):
    tm = o_ref.shape[0]
    _tile_dest(offs_ref, idx_ref, dest_v, dest_s, sem_i)
    _issue_row_copies(tm, lambda r: pltpu.make_async_copy(_slot(ys_ref, dest_s[0, r]), _slot(buf_ref, r), sem))
    pltpu.make_async_copy(ys_ref.at[pl.ds(0, tm * SLOT)], buf_ref, sem).wait()
    o_ref[...] = _from_slots(buf_ref, tm)


def _unpermute(plan, idx, ys, m, tm):
    return pl.pallas_call(
        _unpermute_kernel,
        grid_spec=pltpu.PrefetchScalarGridSpec(
            num_scalar_prefetch=1,
            grid=(m // tm,),
            in_specs=[pl.BlockSpec((1, 2, tm), lambda i, *_: (i, 0, 0)),
                      pl.BlockSpec(memory_space=pl.ANY)],
            out_specs=pl.BlockSpec((tm, D_MODEL), lambda i, *_: (i, 0)),
            scratch_shapes=[pltpu.VMEM((1, tm), jnp.int32),
                            pltpu.SMEM((1, tm), jnp.int32),
                            pltpu.VMEM((tm * SLOT, LANE), F32),
                            pltpu.SemaphoreType.DMA, pltpu.SemaphoreType.DMA]),
        out_shape=jax.ShapeDtypeStruct((m, D_MODEL), F32),
        compiler_params=_cparams("arbitrary"),
        name="moe_unpermute",
    )(plan["offs"], idx, ys)


FOX_AUG = 6


def _fox_placement():
    pl_q = np.zeros((3 * LANE, SELF_W), np.float32)
    pl_k = np.zeros((3 * LANE, SELF_W), np.float32)
    ones_q = np.zeros((1, SELF_W), np.float32)
    ones_k = np.zeros((1, SELF_W), np.float32)
    for h in range(FOX_HEADS):
        base = LANE * (h // 2) + FOX_AUG * (h % 2)
        for piece in range(3):
            pl_q[piece * LANE + h, base + piece] = 1.0
            pl_k[piece * LANE + h, base + 3 + piece] = -1.0
            ones_q[0, base + 3 + piece] = 1.0
            ones_k[0, base + piece] = 1.0
    return (jnp.asarray(pl_q, BF16), jnp.asarray(pl_k, BF16), jnp.asarray(ones_q), jnp.asarray(ones_k))


def _proj_b_kernel(x_ref, wq_ref, wog_ref, wmq_ref, wk_ref, wv_ref, wf_ref, bf_ref, plq_ref, plk_ref,
                   oq_ref, ok_ref, q_ref, k_ref, v_ref, og_ref, mq_ref, aq_ref, ak_ref, carry_ref):
    tm = x_ref.shape[0]

    @pl.when(pl.program_id(1) == 0)
    def _():
        carry_ref[...] = jnp.zeros_like(carry_ref)

    xb = x_ref[...].astype(BF16)
    for w_ref, o_ref in ((wq_ref, q_ref), (wk_ref, k_ref), (wv_ref, v_ref), (wog_ref, og_ref), (wmq_ref, mq_ref)):
        o_ref[...] = _dot(xb, w_ref[...]).astype(o_ref.dtype)

    log_f = _log_sigmoid(_dot(xb, wf_ref[...]) + bf_ref[...])
    row = lax.broadcasted_iota(jnp.int32, (tm, tm), 0)
    col = lax.broadcasted_iota(jnp.int32, (tm, tm), 1)
    tri = jnp.where(col <= row, 1.0, 0.0).astype(BF16)
    f_hi, f_mid, f_lo = _split3(log_f)
    cum = carry_ref[...] + (_dot(tri, f_hi) + _dot(tri, f_mid) + _dot(tri, f_lo))
    carry_ref[...] = cum[tm - 1:tm, :]
    c3 = jnp.concatenate(_split3(cum * LOG2E), axis=1)
    aq_ref[...] = (_dot(c3, plq_ref[...]) + oq_ref[...]).astype(aq_ref.dtype)
    ak_ref[...] = (_dot(c3, plk_ref[...]) + ok_ref[...]).astype(ak_ref.dtype)


def _proj_b(x2d, wq, wog, wmq, wk, wv, wf, bf, batch, seq, tm):
    m = batch * seq
    nt = seq // tm
    row_map = lambda b, i: (b * nt + i, 0)
    const = lambda b, i: (0, 0)
    consts = (wq, wog, wmq, wk, wv, wf, bf) + _fox_placement()
    widths = (SELF_W, SELF_W, SELF_W, SELF_W, MEM_W, SELF_W, SELF_W)
    return pl.pallas_call(
        _proj_b_kernel,
        grid=(batch, nt),
        in_specs=[pl.BlockSpec((tm, D_MODEL), row_map)] + [pl.BlockSpec(a.shape, const) for a in consts],
        out_specs=[pl.BlockSpec((tm, n), row_map) for n in widths],
        out_shape=[jax.ShapeDtypeStruct((m, n), BF16) for n in widths],
        scratch_shapes=[pltpu.VMEM((1, LANE), F32)],
        compiler_params=_cparams("parallel", "arbitrary"),
        name="proj_b",
    )(x2d, *consts)


def _pair_rms_norm(t, gain):
    lo_half = lax.broadcasted_iota(jnp.int32, t.shape, 1) < FOX_HD
    sq = t * t
    ss_lo = jnp.sum(jnp.where(lo_half, sq, 0.0), axis=-1, keepdims=True)
    ss_hi = jnp.sum(jnp.where(lo_half, 0.0, sq), axis=-1, keepdims=True)
    ss = jnp.where(lo_half, ss_lo, ss_hi)
    return t * lax.rsqrt(ss * (1.0 / FOX_HD) + RMS_EPS) * gain


SUB = 8
FOX_VROWS = FOX_HD + 16
FOX_QTILES = 8


def _sublane_all(x, op):
    shift = SUB // 2
    while shift:
        x = op(x, pltpu.roll(x, shift, 0))
        shift //= 2
    return x


def _fox_kernel(q_ref, aq_ref, k_ref, ak_ref, v_ref, og_ref, qg_ref, kg_ref, o_ref,
                ka_ref, vt_ref, qa_ref, m_ref, acc_ref, sa_ref, sb_ref):
    tq = sa_ref.shape[1]
    seq = k_ref.shape[1]
    w = pl.program_id(2)

    @pl.when(w == 0)
    def _():
        def fill(c, carry):
            rows = pl.ds(pl.multiple_of(c * tq, tq), tq)
            kn = _pair_rms_norm(k_ref[0, rows, :].astype(F32), kg_ref[...])
            ka_ref[rows, :] = jnp.concatenate([kn.astype(BF16), ak_ref[0, rows, :]], axis=1)
            vt = v_ref[0, rows, :].astype(F32).T.astype(BF16)
            for hh in range(2):
                vt_ref[hh, :FOX_HD, rows] = vt[FOX_HD * hh:FOX_HD * (hh + 1), :]
                vt_ref[hh, FOX_HD:, rows] = jnp.ones((FOX_VROWS - FOX_HD, tq), BF16)
            return carry
        lax.fori_loop(0, seq // tq, fill, 0)

    lane = lax.broadcasted_iota(jnp.int32, (tq, LANE), 1)
    krow = lax.broadcasted_iota(jnp.int32, (tq, tq), 0)
    qcol = lax.broadcasted_iota(jnp.int32, (tq, tq), 1)
    causal = krow <= qcol
    nb = tq // SUB

    def prep(i):
        rows = slice(i * tq, (i + 1) * tq)
        qn = _pair_rms_norm(q_ref[rows, :].astype(F32), qg_ref[...]) * ((FOX_HD ** -0.5) * LOG2E)
        aq = aq_ref[rows, :]
        for hh in range(2):
            feat = (lane >= FOX_HD * hh) & (lane < FOX_HD * (hh + 1))
            bias = (lane >= FOX_AUG * hh) & (lane < FOX_AUG * (hh + 1))
            qa_ref[i % 2, hh] = jnp.concatenate([jnp.where(feat, qn, 0.0).astype(BF16),
                                                 jnp.where(bias, aq, jnp.zeros_like(aq))], axis=1)
        m_ref[i % 2] = jnp.full(m_ref.shape[1:], NEG_BIG, F32)
        acc_ref[i % 2] = jnp.zeros(acc_ref.shape[1:], F32)

    def scores(i, j, s_ref):
        kt = ka_ref[pl.ds(pl.multiple_of(j * tq, tq), tq), :]
        for hh in range(2):
            s_ref[hh] = _dot_nt(kt, qa_ref[i % 2, hh])

    def consume(i, j, s_ref, masked):
        cols = pl.ds(pl.multiple_of(j * tq, tq), tq)
        for hh in range(2):
            s = s_ref[hh]
            if masked:
                s = jnp.where(causal, s, NEG_BIG)
            s3 = s.reshape(nb, SUB, tq)
            m_old = m_ref[i % 2, hh]
            m_new = jnp.maximum(m_old, _sublane_all(jnp.max(s3, axis=0), jnp.maximum))
            alpha = jnp.exp2(m_old - m_new)
            p = jnp.exp2(s3 - m_new[None]).reshape(tq, tq).astype(BF16)
            pv = _dot(vt_ref[hh, :, cols], p)
            acc3 = acc_ref[i % 2, hh].reshape(FOX_VROWS // SUB, SUB, tq) * alpha[None]
            acc_ref[i % 2, hh] = acc3.reshape(FOX_VROWS, tq) + pv
            m_ref[i % 2, hh] = m_new

    def finish(i):
        rows = slice(i * tq, (i + 1) * tq)
        o_t = []
        for hh in range(2):
            l = acc_ref[i % 2, hh, FOX_HD:FOX_HD + SUB, :]
            o_t.append((acc_ref[i % 2, hh, :FOX_HD, :].reshape(FOX_HD // SUB, SUB, tq) / l[None]).reshape(FOX_HD, tq))
        o = jnp.concatenate(o_t, axis=0).T
        og = og_ref[rows, :].astype(F32)
        o_ref[rows, :] = (o * _sigmoid(og)).astype(o_ref.dtype)

    cur, other = sa_ref, sb_ref
    prep(0)
    scores(0, 0, cur)
    for i in range(FOX_QTILES):
        qi = FOX_QTILES * w + i

        def pair(t, carry, i=i, cur=cur, other=other):
            j = 2 * t
            scores(i, j + 1, other)
            consume(i, j, cur, False)
            scores(i, j + 2, cur)
            consume(i, j + 1, other, False)
            return carry

        lax.fori_loop(0, qi // 2, pair, 0)
        last = i == FOX_QTILES - 1
        if i % 2 == 0:
            if not last:
                prep(i + 1)
                scores(i + 1, 0, other)
            consume(i, qi, cur, True)
            cur, other = other, cur
        else:
            scores(i, qi, other)
            consume(i, qi - 1, cur, False)
            if not last:
                prep(i + 1)
                scores(i + 1, 0, cur)
            consume(i, qi, other, True)
        finish(i)


def _fox(q, aq, k, ak, v, og, qg2, kg2, batch, seq, tq):
    tg = FOX_QTILES * tq
    ng = seq // tg
    m = batch * seq
    k3, ak3, v3 = (a.reshape(batch, seq, SELF_W) for a in (k, ak, v))
    tile_map = lambda b, p, i: (b * ng + i, p)
    seq_map = lambda b, p, i: (b, 0, p)
    const = lambda b, p, i: (0, 0)
    return pl.pallas_call(
        _fox_kernel,
        grid=(batch, FOX_PAIRS, ng),
        in_specs=[pl.BlockSpec((tg, LANE), tile_map),
                  pl.BlockSpec((tg, LANE), tile_map),
                  pl.BlockSpec((1, seq, LANE), seq_map),
                  pl.BlockSpec((1, seq, LANE), seq_map),
                  pl.BlockSpec((1, seq, LANE), seq_map),
                  pl.BlockSpec((tg, LANE), tile_map),
                  pl.BlockSpec((1, LANE), const),
                  pl.BlockSpec((1, LANE), const)],
        out_specs=pl.BlockSpec((tg, LANE), tile_map),
        out_shape=jax.ShapeDtypeStruct((m, SELF_W), BF16),
        scratch_shapes=[pltpu.VMEM((seq, 2 * LANE), BF16),
                        pltpu.VMEM((2, FOX_VROWS, seq), BF16),
                        pltpu.VMEM((2, 2, tq, 2 * LANE), BF16),
                        pltpu.VMEM((2, 2, SUB, tq), F32),
                        pltpu.VMEM((2, 2, FOX_VROWS, tq), F32),
                        pltpu.VMEM((2, tq, tq), F32),
                        pltpu.VMEM((2, tq, tq), F32)],
        compiler_params=_cparams("parallel", "parallel", "arbitrary"),
        name="fox",
    )(q, aq, k3, ak3, v3, og, qg2, kg2)


def _pad_heads(w, n_heads, d, d_pad):
    lead = w.shape[:-1]
    w = w.reshape(lead + (n_heads, d))
    w = jnp.pad(w, [(0, 0)] * len(lead) + [(0, 0), (0, d_pad - d)])
    return w.reshape(lead + (n_heads * d_pad,))


def _pad_last(w, n):
    return jnp.pad(w, [(0, 0)] * (w.ndim - 1) + [(0, n - w.shape[-1])])


def kernel(x, mem, w_in_a, w_gate_up_a, b_gate_a, gla_norm_g, w_in_b, q_norm_g, w_kv_shared, b_forget, k_norm_g, w_mem_kv, w_out, ln_mix_g, ln_mix_b, ln_ffn_g, ln_ffn_b, w_router, b_router, w_exp_gate, w_exp_up, w_exp_down):
    batch, seq, d = x.shape
    m = batch * seq
    tm = 512
    x2d = x.reshape(m, d)

    wa = w_in_a[0]
    s0, s1, s2, s3, s4 = (GLA_KDIM, 2 * GLA_KDIM, 2 * GLA_KDIM + SELF_W,
                          2 * GLA_KDIM + SELF_W + GLA_GATE_RANK, 2 * GLA_KDIM + 2 * SELF_W + GLA_GATE_RANK)
    w_a = jnp.concatenate([
        _pad_heads(wa[:, :s0], GLA_HEADS, GLA_DK, GLA_DK_PAD),
        _pad_heads(wa[:, s0:s1], GLA_HEADS, GLA_DK, GLA_DK_PAD),
        wa[:, s1:s2], wa[:, s3:s4], wa[:, s4:], _pad_last(wa[:, s2:s3], LANE)], axis=1).astype(BF16)
    wgu = _pad_heads(w_gate_up_a[0], GLA_HEADS, GLA_DK, GLA_DK_PAD)
    wgu = jnp.pad(wgu, ((0, LANE - GLA_GATE_RANK), (0, 0))).astype(BF16)
    bg = _pad_heads(b_gate_a[0], GLA_HEADS, GLA_DK, GLA_DK_PAD).reshape(1, GLA_KPAD)
    gn = jnp.tile(gla_norm_g[0], GLA_HEADS).reshape(1, SELF_W)

    wb = w_in_b[0]
    wq, wog, wmq = (wb[:, :SELF_W].astype(BF16), wb[:, SELF_W:2 * SELF_W].astype(BF16),
                    wb[:, 2 * SELF_W:].astype(BF16))
    wk, wv = w_kv_shared[:, :SELF_W].astype(BF16), w_kv_shared[:, SELF_W:2 * SELF_W].astype(BF16)
    wf = _pad_last(w_kv_shared[:, 2 * SELF_W:], LANE).astype(BF16)
    bf = _pad_last(b_forget, LANE).reshape(1, LANE)
    qg2 = jnp.tile(q_norm_g[0], 2).reshape(1, LANE)
    kg2 = jnp.tile(k_norm_g, 2).reshape(1, LANE)

    w_mkv = jnp.concatenate([w_mem_kv[l] for l in range(DEPTH)], axis=1).astype(BF16)
    wo = w_out.astype(BF16)
    wr_t = w_router.T
    wrh = wr_t.astype(BF16)
    wr2 = jnp.concatenate([wrh, (wr_t - wrh.astype(F32)).astype(BF16)], axis=0)
    br = b_router.reshape(N_EXPERTS, 1)
    row = lambda a: a.reshape(1, d)

    mkv = _mem_kv(mem.reshape(batch * N_MEM, d), w_mkv)

    tmx = 256
    n_tiles = m // tmx + N_CLASSES
    n_rows = n_tiles * tmx

    wr_pad = _pad_last(w_router, LANE).astype(BF16)

    def tail(o, mq, xin, layer):
        x1t, idx, counts = _out_block(o, mq, mkv, layer, xin, wo[layer], row(ln_mix_g[layer]), row(ln_mix_b[layer]),
                                      wrh, wr2, br, seq, tm)
        plan = _moe_plan(counts, tmx, n_tiles)
        xs = _dispatch(plan, idx, x1t, tm, tmx, n_rows)
        ys = _moe_ffn(plan, xs, layer, w_exp_gate, w_exp_up, w_exp_down, wr_pad,
                      row(ln_ffn_g[layer]), row(ln_ffn_b[layer]), tmx, n_tiles)
        return _unpermute(plan, idx, ys, m, tm)

    q, k, v, r, mq, g = _proj_a(x2d, w_a, tm)
    o = _gla(q, k, v, r, g, wgu, bg, gn, batch, seq, tm)
    xa = tail(o, mq, x2d, 0)

    qb, kb, vb, og, mqb, aq, ak = _proj_b(xa, wq, wog, wmq, wk, wv, wf, bf, batch, seq, tm)
    ob = _fox(qb, aq, kb, ak, vb, og, qg2, kg2, batch, seq, tm)
    xb = tail(ob, mqb, xa, 1)
    return xb.reshape(batch, seq, d)
```

```python
import functools
import math

import jax
import jax.numpy as jnp
import numpy as np
from jax import lax
from jax.experimental import pallas as pl
from jax.experimental.pallas import tpu as pltpu

F32 = jnp.float32
BF16 = jnp.bfloat16

D_MODEL = 1024
DEPTH = 2
CHUNK = 64
N_MEM = 256
MEM_HEADS = 4
MEM_W = D_MODEL // 4
MEM_HD = MEM_W // MEM_HEADS
SELF_W = D_MODEL - MEM_W
GLA_HEADS = 4
GLA_KDIM = SELF_W // 2
GLA_DK = GLA_KDIM // GLA_HEADS
GLA_DV = SELF_W // GLA_HEADS
GLA_GATE_RANK = 16
GLA_TAU = 16.0
FOX_HD = 64
FOX_HEADS = SELF_W // FOX_HD
N_EXPERTS = 16
N_GROUPS = 4
EXPERTS_PER_GROUP = N_EXPERTS // N_GROUPS
D_EXPERT = D_MODEL // 2
DN_ALPHA = (2.0 * DEPTH) ** 0.25
LN_EPS = 1e-5
RMS_EPS = 1e-6

LANE = 128
GLA_DK_PAD = LANE
GLA_KPAD = GLA_HEADS * GLA_DK_PAD
GLA_VBLOCKS = SELF_W // LANE
GLA_HEAD_BLOCKS = ((0, 1), (1, 2), (3, 4), (4, 5))
FOX_PAIRS = FOX_HEADS // 2
LOG2E = math.log2(math.e)
NEG_BIG = -1e30
VMEM_LIMIT_BYTES = 48 * 1024 * 1024

NT_DIMS = (((1,), (1,)), ((), ()))
TN_DIMS = (((0,), (0,)), ((), ()))


def _cparams(*sem):
    return pltpu.CompilerParams(dimension_semantics=sem, vmem_limit_bytes=VMEM_LIMIT_BYTES)


def _dot(a, b):
    return jnp.dot(a, b, preferred_element_type=F32)


def _dot_nt(a, b):
    return lax.dot_general(a, b, NT_DIMS, preferred_element_type=F32)


def _dot_tn(a, b):
    return lax.dot_general(a, b, TN_DIMS, preferred_element_type=F32)


def _log_sigmoid(z):
    return jnp.minimum(z, 0.0) - jnp.log(1.0 + jnp.exp(-jnp.abs(z)))


def _sigmoid(z):
    return 1.0 / (1.0 + jnp.exp(-z))


def _split2(x):
    hi = x.astype(BF16)
    lo = (x - hi.astype(F32)).astype(BF16)
    return hi, lo


def _split3(x):
    hi = x.astype(BF16)
    r1 = x - hi.astype(F32)
    mid = r1.astype(BF16)
    lo = (r1 - mid.astype(F32)).astype(BF16)
    return hi, mid, lo


def _layer_norm(x, g, b):
    mu = jnp.mean(x, axis=-1, keepdims=True)
    xc = x - mu
    var = jnp.mean(xc * xc, axis=-1, keepdims=True)
    return xc * lax.rsqrt(var + LN_EPS) * g + b


def _proj_a_kernel(x_ref, w_ref, q_ref, k_ref, v_ref, r_ref, mq_ref, g_ref):
    xb = x_ref[...].astype(BF16)
    c = 0
    for ref in (q_ref, k_ref, v_ref, r_ref, mq_ref, g_ref):
        n = ref.shape[1]
        ref[...] = _dot(xb, w_ref[:, c:c + n]).astype(ref.dtype)
        c += n


def _proj_a(x2d, w, tm):
    m = x2d.shape[0]
    widths = (GLA_KPAD, GLA_KPAD, SELF_W, SELF_W, MEM_W, LANE)
    assert w.shape == (D_MODEL, sum(widths))
    return pl.pallas_call(
        _proj_a_kernel,
        grid=(m // tm,),
        in_specs=[pl.BlockSpec((tm, D_MODEL), lambda i: (i, 0)),
                  pl.BlockSpec(w.shape, lambda i: (0, 0))],
        out_specs=[pl.BlockSpec((tm, n), lambda i: (i, 0)) for n in widths],
        out_shape=[jax.ShapeDtypeStruct((m, n), BF16) for n in widths],
        compiler_params=_cparams("parallel"),
        name="proj_a",
    )(x2d, w)


def _gla_kernel(q_ref, k_ref, v_ref, r_ref, g_ref, wgu_ref, bg_ref, gn_ref, o_ref, st_ref, la_ref):
    tg = q_ref.shape[0]

    @pl.when(pl.program_id(1) == 0)
    def _():
        st_ref[...] = jnp.zeros_like(st_ref)

    z = _dot(g_ref[...], wgu_ref[...]) + bg_ref[...]
    la_ref[...] = _log_sigmoid(z) * (1.0 / GLA_TAU)

    row = lax.broadcasted_iota(jnp.int32, (CHUNK, CHUNK), 0)
    col = lax.broadcasted_iota(jnp.int32, (CHUNK, CHUNK), 1)
    causal = col <= row
    tri = jnp.where(causal, 1.0, 0.0).astype(BF16)
    lo_half = lax.broadcasted_iota(jnp.int32, (CHUNK, LANE), 1) < (LANE // 2)
    scale = GLA_DK ** -0.5

    def chunk(c, carry):
        rows = pl.ds(c * CHUNK, CHUNK)
        la_hi, la_lo = _split2(la_ref[rows, :])
        b = _dot(tri, la_hi) + _dot(tri, la_lo)
        bl = b[CHUNK - 1:CHUNK, :]
        qc = q_ref[rows, :].astype(F32) * scale
        kc = k_ref[rows, :].astype(F32)
        qd = (qc * jnp.exp(b)).astype(BF16)
        ki = (kc * jnp.exp(-b)).astype(BF16)
        ke = (kc * jnp.exp(bl - b)).astype(BF16)
        dec = jnp.exp(bl)
        unit = []
        for h in range(GLA_HEADS):
            sl = slice(GLA_DK_PAD * h, GLA_DK_PAD * (h + 1))
            qh, kih, keh = qd[:, sl], ki[:, sl], ke[:, sl]
            att = jnp.where(causal, _dot_nt(qh, kih), 0.0).astype(BF16)
            for t, blk in enumerate(GLA_HEAD_BLOCKS[h]):
                u = 2 * h + t
                vb = v_ref[rows, LANE * blk:LANE * (blk + 1)]
                st = st_ref[u]
                unit.append(_dot(att, vb) + _dot_nt(qh, st.astype(BF16)))
                st_ref[u] = st * dec[:, sl] + _dot_tn(vb, keh)
        o_blk = [unit[0], jnp.where(lo_half, unit[1], unit[2]), unit[3],
                 unit[4], jnp.where(lo_half, unit[5], unit[6]), unit[7]]
        sq = [o * o for o in o_blk]
        full = [jnp.sum(s, axis=-1, keepdims=True) for s in sq]
        lo1 = jnp.sum(jnp.where(lo_half, sq[1], 0.0), axis=-1, keepdims=True)
        hi1 = jnp.sum(jnp.where(lo_half, 0.0, sq[1]), axis=-1, keepdims=True)
        lo4 = jnp.sum(jnp.where(lo_half, sq[4], 0.0), axis=-1, keepdims=True)
        hi4 = jnp.sum(jnp.where(lo_half, 0.0, sq[4]), axis=-1, keepdims=True)
        ss = [full[0] + lo1, hi1 + full[2], full[3] + lo4, hi4 + full[5]]
        inv = [lax.rsqrt(s * (1.0 / GLA_DV) + RMS_EPS) for s in ss]
        inv_blk = [inv[0], jnp.where(lo_half, inv[0], inv[1]), inv[1],
                   inv[2], jnp.where(lo_half, inv[2], inv[3]), inv[3]]
        for blk in range(GLA_VBLOCKS):
            cs = slice(LANE * blk, LANE * (blk + 1))
            rg = r_ref[rows, cs].astype(F32)
            y = o_blk[blk] * inv_blk[blk] * gn_ref[:, cs]
            o_ref[rows, cs] = (y * (rg * _sigmoid(rg))).astype(o_ref.dtype)
        return carry

    for c in range(tg // CHUNK):
        chunk(c, 0)


def _gla(q, k, v, r, g, wgu, bg, gn, batch, seq, tg):
    m = batch * seq
    nt = seq // tg
    row_map = lambda b, i: (b * nt + i, 0)
    const = lambda b, i: (0, 0)
    return pl.pallas_call(
        _gla_kernel,
        grid=(batch, nt),
        in_specs=[pl.BlockSpec((tg, GLA_KPAD), row_map),
                  pl.BlockSpec((tg, GLA_KPAD), row_map),
                  pl.BlockSpec((tg, SELF_W), row_map),
                  pl.BlockSpec((tg, SELF_W), row_map),
                  pl.BlockSpec((tg, LANE), row_map),
                  pl.BlockSpec(wgu.shape, const),
                  pl.BlockSpec(bg.shape, const),
                  pl.BlockSpec(gn.shape, const)],
        out_specs=pl.BlockSpec((tg, SELF_W), row_map),
        out_shape=jax.ShapeDtypeStruct((m, SELF_W), BF16),
        scratch_shapes=[pltpu.VMEM((2 * GLA_HEADS, LANE, GLA_DK_PAD), F32),
                        pltpu.VMEM((tg, GLA_KPAD), F32)],
        compiler_params=_cparams("parallel", "arbitrary"),
        name="gla",
    )(q, k, v, r, g, wgu, bg, gn)


def _mem_kv_kernel(m_ref, w_ref, o_ref):
    o_ref[...] = _dot(m_ref[...].astype(BF16), w_ref[...]).astype(o_ref.dtype)


def _mem_kv(mem2d, w):
    m, n = mem2d.shape[0], w.shape[1]
    tm = N_MEM
    return pl.pallas_call(
        _mem_kv_kernel,
        grid=(m // tm,),
        in_specs=[pl.BlockSpec((tm, D_MODEL), lambda i: (i, 0)),
                  pl.BlockSpec(w.shape, lambda i: (0, 0))],
        out_specs=pl.BlockSpec((tm, n), lambda i: (i, 0)),
        out_shape=jax.ShapeDtypeStruct((m, n), BF16),
        compiler_params=_cparams("parallel"),
        name="mem_kv",
    )(mem2d, w)


def _top2_sum(a, b, c, d):
    p, q = jnp.maximum(a, b), jnp.minimum(a, b)
    r, s = jnp.maximum(c, d), jnp.minimum(c, d)
    return jnp.maximum(p, r) + jnp.maximum(jnp.minimum(p, r), jnp.maximum(q, s))


PAIRS = tuple((i, j) for i in range(EXPERTS_PER_GROUP) for j in range(i + 1, EXPERTS_PER_GROUP))
N_CLASSES = N_GROUPS * len(PAIRS)
CLS_PAD = 32
SLOT = D_MODEL // LANE


def _to_slots(ref, x):
    rows = x.shape[0]
    for j in range(SLOT):
        ref[pl.ds(j, rows, stride=SLOT), :] = x[:, LANE * j:LANE * (j + 1)]


def _from_slots(ref, rows):
    return jnp.concatenate([ref[pl.ds(j, rows, stride=SLOT), :] for j in range(SLOT)], axis=1)


def _slot(ref, r):
    return ref.at[pl.ds(pl.multiple_of(r * SLOT, SLOT), SLOT)]


CLASS_EA = np.array([EXPERTS_PER_GROUP * (c // len(PAIRS)) + PAIRS[c % len(PAIRS)][0] for c in range(N_CLASSES)], np.int32)
CLASS_EB = np.array([EXPERTS_PER_GROUP * (c // len(PAIRS)) + PAIRS[c % len(PAIRS)][1] for c in range(N_CLASSES)], np.int32)


def _router(sel):
    score = [_top2_sum(*sel[EXPERTS_PER_GROUP * g:EXPERTS_PER_GROUP * (g + 1)]) for g in range(N_GROUPS)]
    picked = []
    for g in range(N_GROUPS):
        ok = None
        for i in range(N_GROUPS):
            if i == g:
                continue
            c = (score[g] > score[i]) if i < g else (score[g] >= score[i])
            ok = c if ok is None else (ok & c)
        picked.append(ok)
    chosen = []
    for g in range(N_GROUPS):
        for j in range(EXPERTS_PER_GROUP):
            ej = EXPERTS_PER_GROUP * g + j
            rank = None
            for i in range(EXPERTS_PER_GROUP):
                if i == j:
                    continue
                ei = EXPERTS_PER_GROUP * g + i
                beats = (sel[ei] >= sel[ej]) if i < j else (sel[ei] > sel[ej])
                beats = jnp.where(beats, 1.0, 0.0)
                rank = beats if rank is None else rank + beats
            chosen.append(picked[g] & (rank < 1.5))
    return [chosen[CLASS_EA[c]] & chosen[CLASS_EB[c]] for c in range(N_CLASSES)]


def _out_kernel(o_ref, mq_ref, mk_ref, mv_ref, x_ref, wo_ref, lng_ref, lnb_ref,
                wrh_ref, wr2_ref, br_ref, x1_ref, idx_ref, cnt_ref, carry_ref):
    tm = o_ref.shape[0]

    @pl.when(pl.program_id(0) == 0)
    def _():
        carry_ref[...] = jnp.zeros_like(carry_ref)

    mq = mq_ref[...]
    mk = mk_ref[...]
    mv = mv_ref[...]
    lane = lax.broadcasted_iota(jnp.int32, (N_MEM, MEM_W), 1)
    m_out = jnp.zeros((tm, MEM_W), F32)
    for h in range(MEM_HEADS):
        in_head = (lane >= MEM_HD * h) & (lane < MEM_HD * (h + 1))
        s = _dot_nt(mq, jnp.where(in_head, mk, jnp.zeros_like(mk))) * (MEM_HD ** -0.5)
        p = jnp.exp(s - jnp.max(s, axis=-1, keepdims=True))
        l = jnp.sum(p, axis=-1, keepdims=True)
        m_out = m_out + _dot(p.astype(BF16), jnp.where(in_head, mv, jnp.zeros_like(mv))) / l
    y = _dot(o_ref[...], wo_ref[:SELF_W, :]) + _dot(m_out.astype(BF16), wo_ref[SELF_W:, :])
    x1 = _layer_norm(DN_ALPHA * x_ref[...] + y, lng_ref[...], lnb_ref[...])
    _to_slots(x1_ref, x1)

    xh, xl = _split2(x1)
    hh_lh = _dot_nt(wr2_ref[...], xh)
    logits = hh_lh[:N_EXPERTS, :] + hh_lh[N_EXPERTS:, :] + _dot_nt(wrh_ref[...], xl)
    sel = _sigmoid(logits) + br_ref[...]
    onehot = _router([sel[e:e + 1, :] for e in range(N_EXPERTS)])

    csub = lax.broadcasted_iota(jnp.int32, (CLS_PAD, tm), 0)
    oh = jnp.zeros((CLS_PAD, tm), F32)
    for c in range(N_CLASSES):
        oh = jnp.where((csub == c) & onehot[c], 1.0, oh)
    row = lax.broadcasted_iota(jnp.int32, (tm, tm), 0)
    col = lax.broadcasted_iota(jnp.int32, (tm, tm), 1)
    tri_u = jnp.where(row <= col, 1.0, 0.0).astype(BF16)
    incl = _dot(oh.astype(BF16), tri_u)
    carry = carry_ref[...]
    rank = jnp.sum(oh * (carry + incl - 1.0), axis=0, keepdims=True)
    cls = jnp.sum(oh * csub.astype(F32), axis=0, keepdims=True)
    idx_ref[0, 0:1, :] = cls.astype(jnp.int32)
    idx_ref[0, 1:2, :] = rank.astype(jnp.int32)
    carry = carry + jnp.sum(oh, axis=1, keepdims=True)
    carry_ref[...] = carry
    cnt_ref[...] = carry[:, :LANE]


def _out_block(o, mq, mkv, layer, x2d, wo, lng, lnb, wrh, wr2, br, seq, tm):
    m = x2d.shape[0]
    nt = seq // tm
    row_map = lambda i: (i, 0)
    const = lambda i: (0, 0)
    return pl.pallas_call(
        _out_kernel,
        grid=(m // tm,),
        in_specs=[pl.BlockSpec((tm, SELF_W), row_map),
                  pl.BlockSpec((tm, MEM_W), row_map),
                  pl.BlockSpec((N_MEM, MEM_W), lambda i: (i // nt, 2 * layer)),
                  pl.BlockSpec((N_MEM, MEM_W), lambda i: (i // nt, 2 * layer + 1)),
                  pl.BlockSpec((tm, D_MODEL), row_map),
                  pl.BlockSpec(wo.shape, const),
                  pl.BlockSpec(lng.shape, const),
                  pl.BlockSpec(lnb.shape, const),
                  pl.BlockSpec(wrh.shape, const),
                  pl.BlockSpec(wr2.shape, const),
                  pl.BlockSpec(br.shape, const)],
        out_specs=[pl.BlockSpec((tm * SLOT, LANE), row_map),
                   pl.BlockSpec((1, 2, tm), lambda i: (i, 0, 0)),
                   pl.BlockSpec((CLS_PAD, LANE), const)],
        out_shape=[jax.ShapeDtypeStruct((m * SLOT, LANE), F32),
                   jax.ShapeDtypeStruct((m // tm, 2, tm), jnp.int32),
                   jax.ShapeDtypeStruct((CLS_PAD, LANE), F32)],
        scratch_shapes=[pltpu.VMEM((CLS_PAD, tm), F32)],
        compiler_params=_cparams("arbitrary"),
        name="out_ln_router",
    )(o, mq, mkv, mkv, x2d, wo, lng, lnb, wrh, wr2, br)


def _moe_plan(counts, tmx, n_tiles):
    cnt = counts[:N_CLASSES, 0].astype(jnp.int32)
    nt = (cnt + (tmx - 1)) // tmx
    tend = jnp.cumsum(nt)
    offs = (tend - nt) * tmx
    n_used = tend[-1:]
    i = jnp.arange(n_tiles + 1, dtype=jnp.int32)
    ieff = jnp.minimum(i, n_used - 1)
    cls = jnp.sum((ieff[:, None] >= tend[None, :]).astype(jnp.int32), axis=1)
    fresh = jnp.concatenate([jnp.ones((1,), jnp.int32), (cls[1:] != cls[:-1]).astype(jnp.int32)])
    last_tile_row = jnp.where(nt > 0, (tend - 1) * tmx, -1)
    spare = n_used + jnp.arange(N_CLASSES, dtype=jnp.int32)
    spare_row = jnp.where(spare < n_tiles, spare * tmx, -1)
    pad = lambda a: jnp.pad(a, (0, CLS_PAD - N_CLASSES))
    return dict(offs=pad(offs), zero_rows=jnp.concatenate([last_tile_row, spare_row]), row_block=ieff,
                ea=jnp.asarray(CLASS_EA)[cls], eb=jnp.asarray(CLASS_EB)[cls], fresh=fresh, n_used=n_used)


ROW_DMA_UNROLL = 16


def _tile_dest(offs_ref, idx_ref, dest_v, dest_s, sem):
    cls = idx_ref[0, 0:1, :]
    dest = idx_ref[0, 1:2, :]
    for c in range(N_CLASSES):
        dest = dest + jnp.where(cls == c, offs_ref[c], 0)
    dest_v[...] = dest
    cp = pltpu.make_async_copy(dest_v, dest_s, sem)
    cp.start()
    cp.wait()


def _issue_row_copies(n_rows, make_copy):
    def group(g, carry):
        base = pl.multiple_of(g * ROW_DMA_UNROLL, ROW_DMA_UNROLL)
        for k in range(ROW_DMA_UNROLL):
            make_copy(base + k).start(priority=k % 2)
        return carry
    lax.fori_loop(0, n_rows // ROW_DMA_UNROLL, group, 0)


def _dispatch_kernel(offs_ref, zrow_ref, idx_ref, x_ref, xs_ref, dest_v, dest_s, zero_ref, sem_i, sem_z, sem):
    tm = x_ref.shape[0] // SLOT
    zrows = zero_ref.shape[0]

    @pl.when(pl.program_id(0) == 0)
    def _():
        zero_ref[...] = jnp.zeros_like(zero_ref)

        def zcopy(c):
            start = pl.multiple_of(zrow_ref[c] * SLOT, zrows)
            return pltpu.make_async_copy(zero_ref, xs_ref.at[pl.ds(start, zrows)], sem_z)

        for c in range(2 * N_CLASSES):
            @pl.when(zrow_ref[c] >= 0)
            def _(c=c):
                zcopy(c).start()
        for c in range(2 * N_CLASSES):
            @pl.when(zrow_ref[c] >= 0)
            def _(c=c):
                zcopy(c).wait()

    _tile_dest(offs_ref, idx_ref, dest_v, dest_s, sem_i)
    _issue_row_copies(tm, lambda r: pltpu.make_async_copy(_slot(x_ref, r), _slot(xs_ref, dest_s[0, r]), sem))
    pltpu.make_async_copy(x_ref, xs_ref.at[pl.ds(0, tm * SLOT)], sem).wait()


def _dispatch(plan, idx, x1t, tm, tmx, n_rows):
    m = x1t.shape[0] // SLOT
    return pl.pallas_call(
        _dispatch_kernel,
        grid_spec=pltpu.PrefetchScalarGridSpec(
            num_scalar_prefetch=2,
            grid=(m // tm,),
            in_specs=[pl.BlockSpec((1, 2, tm), lambda i, *_: (i, 0, 0)),
                      pl.BlockSpec((tm * SLOT, LANE), lambda i, *_: (i, 0))],
            out_specs=pl.BlockSpec(memory_space=pl.ANY),
            scratch_shapes=[pltpu.VMEM((1, tm), jnp.int32),
                            pltpu.SMEM((1, tm), jnp.int32),
                            pltpu.VMEM((tmx * SLOT, LANE), F32),
                            pltpu.SemaphoreType.DMA, pltpu.SemaphoreType.DMA, pltpu.SemaphoreType.DMA]),
        out_shape=jax.ShapeDtypeStruct((n_rows * SLOT, LANE), F32),
        compiler_params=_cparams("arbitrary"),
        name="moe_dispatch",
    )(plan["offs"], plan["zero_rows"], idx, x1t)


def _moe_ffn_kernel(rb_ref, ea_ref, eb_ref, fresh_ref, nu_ref,
                    xs_ref, wga_ref, wua_ref, wgb_ref, wub_ref, wda_ref, wdb_ref, wr_ref, lng_ref, lnb_ref,
                    ys_ref, h_ref, x_ref, g_ref):
    i = pl.program_id(0)
    t = i - 1
    n_used = nu_ref[0]
    tmx = ys_ref.shape[0] // SLOT
    slot_a = i & 1
    slot_b = 1 - slot_a
    sg_ref, su_ref, sd_ref = (wga_ref, wgb_ref), (wua_ref, wub_ref), (wda_ref, wdb_ref)

    def load_tile():
        x = _from_slots(xs_ref, tmx)
        x_ref[slot_a] = x
        aff = _sigmoid(_dot(x.astype(BF16), wr_ref[...]))
        lane = lax.broadcasted_iota(jnp.int32, aff.shape, 1)
        g = [jnp.sum(jnp.where(lane == e_ref[i], aff, 0.0), axis=-1, keepdims=True) for e_ref in (ea_ref, eb_ref)]
        for s in range(2):
            g_ref[slot_a, s] = jnp.broadcast_to(g[s] / (g[0] + g[1]), (tmx, LANE))

    def gate_up():
        xb = x_ref[slot_a].astype(BF16)
        for s in range(2):
            hg = _dot(xb, sg_ref[s][0, 0])
            hu = _dot(xb, su_ref[s][0, 0])
            h_ref[slot_a, s] = (hg * _sigmoid(hg) * hu).astype(BF16)

    def down_norm():
        f = None
        for s in range(2):
            fs = g_ref[slot_b, s][:, :1] * _dot(h_ref[slot_b, s], sd_ref[s][0, 0])
            f = fs if f is None else f + fs
        _to_slots(ys_ref, _layer_norm(DN_ALPHA * x_ref[slot_b] + f, lng_ref[...], lnb_ref[...]))

    @pl.when(i == 0)
    def _():
        load_tile()
        gate_up()

    @pl.when((i >= 1) & (i < n_used))
    def _():
        load_tile()
        down_norm()
        gate_up()

    @pl.when((i >= n_used) & (t < n_used))
    def _():
        down_norm()

    @pl.when(t >= n_used)
    def _():
        ys_ref[...] = jnp.zeros_like(ys_ref)


def _moe_ffn(plan, xs, layer, wg, wu, wd, wr, lng, lnb, tmx, n_tiles):
    prev = lambda i: jnp.maximum(i - 1, 0)
    rows = lambda i, rb, *_: (rb[i], 0)
    wa = lambda i, rb, ea, eb, *_: (layer, ea[i], 0, 0)
    wb = lambda i, rb, ea, eb, *_: (layer, eb[i], 0, 0)
    wa_prev = lambda i, rb, ea, eb, *_: (layer, ea[prev(i)], 0, 0)
    wb_prev = lambda i, rb, ea, eb, *_: (layer, eb[prev(i)], 0, 0)
    const = lambda i, *_: (0, 0)
    gu, dn = (1, 1, D_MODEL, D_EXPERT), (1, 1, D_EXPERT, D_MODEL)
    return pl.pallas_call(
        _moe_ffn_kernel,
        grid_spec=pltpu.PrefetchScalarGridSpec(
            num_scalar_prefetch=5,
            grid=(n_tiles + 1,),
            in_specs=[pl.BlockSpec((tmx * SLOT, LANE), rows),
                      pl.BlockSpec(gu, wa), pl.BlockSpec(gu, wa), pl.BlockSpec(gu, wb), pl.BlockSpec(gu, wb),
                      pl.BlockSpec(dn, wa_prev), pl.BlockSpec(dn, wb_prev),
                      pl.BlockSpec(wr.shape, const),
                      pl.BlockSpec(lng.shape, const), pl.BlockSpec(lnb.shape, const)],
            out_specs=pl.BlockSpec((tmx * SLOT, LANE), lambda i, *_: (prev(i), 0)),
            scratch_shapes=[pltpu.VMEM((2, 2, tmx, D_EXPERT), BF16),
                            pltpu.VMEM((2, tmx, D_MODEL), F32),
                            pltpu.VMEM((2, 2, tmx, LANE), F32)]),
        out_shape=jax.ShapeDtypeStruct(xs.shape, F32),
        compiler_params=_cparams("arbitrary"),
        name="moe_ffn",
    )(plan["row_block"], plan["ea"], plan["eb"], plan["fresh"], plan["n_used"],
      xs, wg, wu, wg, wu, wd, wd, wr, lng, lnb)


def _unpermute_kernel(offs_ref, idx_ref, ys_ref, o_ref, dest_v, dest_s, buf_ref, sem_i, sem):
    tm = o_ref.shape[0]
    _tile_dest(offs_ref, idx_ref, dest_v, dest_s, sem_i)
    _issue_row_copies(tm, lambda r: pltpu.make_async_copy(_slot(ys_ref, dest_s[0, r]), _slot(buf_ref, r), sem))
    pltpu.make_async_copy(ys_ref.at[pl.ds(0, tm * SLOT)], buf_ref, sem).wait()
    o_ref[...] = _from_slots(buf_ref, tm)


def _unpermute(plan, idx, ys, m, tm):
    return pl.pallas_call(
        _unpermute_kernel,
        grid_spec=pltpu.PrefetchScalarGridSpec(
            num_scalar_prefetch=1,
            grid=(m // tm,),
            in_specs=[pl.BlockSpec((1, 2, tm), lambda i, *_: (i, 0, 0)),
                      pl.BlockSpec(memory_space=pl.ANY)],
            out_specs=pl.BlockSpec((tm, D_MODEL), lambda i, *_: (i, 0)),
            scratch_shapes=[pltpu.VMEM((1, tm), jnp.int32),
                            pltpu.SMEM((1, tm), jnp.int32),
                            pltpu.VMEM((tm * SLOT, LANE), F32),
                            pltpu.SemaphoreType.DMA, pltpu.SemaphoreType.DMA]),
        out_shape=jax.ShapeDtypeStruct((m, D_MODEL), F32),
        compiler_params=_cparams("arbitrary"),
        name="moe_unpermute",
    )(plan["offs"], idx, ys)


FOX_AUG = 6


def _fox_placement():
    pl_q = np.zeros((3 * LANE, SELF_W), np.float32)
    pl_k = np.zeros((3 * LANE, SELF_W), np.float32)
    ones_q = np.zeros((1, SELF_W), np.float32)
    ones_k = np.zeros((1, SELF_W), np.float32)
    for h in range(FOX_HEADS):
        base = LANE * (h // 2) + FOX_AUG * (h % 2)
        for piece in range(3):
            pl_q[piece * LANE + h, base + piece] = 1.0
            pl_k[piece * LANE + h, base + 3 + piece] = -1.0
            ones_q[0, base + 3 + piece] = 1.0
            ones_k[0, base + piece] = 1.0
    return (jnp.asarray(pl_q, BF16), jnp.asarray(pl_k, BF16), jnp.asarray(ones_q), jnp.asarray(ones_k))


def _proj_b_kernel(x_ref, wq_ref, wog_ref, wmq_ref, wk_ref, wv_ref, wf_ref, bf_ref, plq_ref, plk_ref,
                   oq_ref, ok_ref, q_ref, k_ref, v_ref, og_ref, mq_ref, aq_ref, ak_ref, carry_ref):
    tm = x_ref.shape[0]

    @pl.when(pl.program_id(1) == 0)
    def _():
        carry_ref[...] = jnp.zeros_like(carry_ref)

    xb = x_ref[...].astype(BF16)
    for w_ref, o_ref in ((wq_ref, q_ref), (wk_ref, k_ref), (wv_ref, v_ref), (wog_ref, og_ref), (wmq_ref, mq_ref)):
        o_ref[...] = _dot(xb, w_ref[...]).astype(o_ref.dtype)

    log_f = _log_sigmoid(_dot(xb, wf_ref[...]) + bf_ref[...])
    row = lax.broadcasted_iota(jnp.int32, (tm, tm), 0)
    col = lax.broadcasted_iota(jnp.int32, (tm, tm), 1)
    tri = jnp.where(col <= row, 1.0, 0.0).astype(BF16)
    f_hi, f_mid, f_lo = _split3(log_f)
    cum = carry_ref[...] + (_dot(tri, f_hi) + _dot(tri, f_mid) + _dot(tri, f_lo))
    carry_ref[...] = cum[tm - 1:tm, :]
    c3 = jnp.concatenate(_split3(cum * LOG2E), axis=1)
    aq_ref[...] = (_dot(c3, plq_ref[...]) + oq_ref[...]).astype(aq_ref.dtype)
    ak_ref[...] = (_dot(c3, plk_ref[...]) + ok_ref[...]).astype(ak_ref.dtype)


def _proj_b(x2d, wq, wog, wmq, wk, wv, wf, bf, batch, seq, tm):
    m = batch * seq
    nt = seq // tm
    row_map = lambda b, i: (b * nt + i, 0)
    const = lambda b, i: (0, 0)
    consts = (wq, wog, wmq, wk, wv, wf, bf) + _fox_placement()
    widths = (SELF_W, SELF_W, SELF_W, SELF_W, MEM_W, SELF_W, SELF_W)
    return pl.pallas_call(
        _proj_b_kernel,
        grid=(batch, nt),
        in_specs=[pl.BlockSpec((tm, D_MODEL), row_map)] + [pl.BlockSpec(a.shape, const) for a in consts],
        out_specs=[pl.BlockSpec((tm, n), row_map) for n in widths],
        out_shape=[jax.ShapeDtypeStruct((m, n), BF16) for n in widths],
        scratch_shapes=[pltpu.VMEM((1, LANE), F32)],
        compiler_params=_cparams("parallel", "arbitrary"),
        name="proj_b",
    )(x2d, *consts)


def _pair_rms_norm(t, gain):
    lo_half = lax.broadcasted_iota(jnp.int32, t.shape, 1) < FOX_HD
    sq = t * t
    ss_lo = jnp.sum(jnp.where(lo_half, sq, 0.0), axis=-1, keepdims=True)
    ss_hi = jnp.sum(jnp.where(lo_half, 0.0, sq), axis=-1, keepdims=True)
    ss = jnp.where(lo_half, ss_lo, ss_hi)
    return t * lax.rsqrt(ss * (1.0 / FOX_HD) + RMS_EPS) * gain


SUB = 8
FOX_VROWS = FOX_HD + 16
FOX_QTILES = 8


def _sublane_all(x, op):
    shift = SUB // 2
    while shift:
        x = op(x, pltpu.roll(x, shift, 0))
        shift //= 2
    return x


def _fox_kernel(q_ref, aq_ref, k_ref, ak_ref, v_ref, og_ref, qg_ref, kg_ref, o_ref,
                ka_ref, vt_ref, qa_ref, m_ref, acc_ref, sa_ref, sb_ref):
    tq = sa_ref.shape[1]
    seq = k_ref.shape[1]
    w = pl.program_id(2)

    @pl.when(w == 0)
    def _():
        def fill(c, carry):
            rows = pl.ds(pl.multiple_of(c * tq, tq), tq)
            kn = _pair_rms_norm(k_ref[0, rows, :].astype(F32), kg_ref[...])
            ka_ref[rows, :] = jnp.concatenate([kn.astype(BF16), ak_ref[0, rows, :]], axis=1)
            vt = v_ref[0, rows, :].astype(F32).T.astype(BF16)
            for hh in range(2):
                vt_ref[hh, :FOX_HD, rows] = vt[FOX_HD * hh:FOX_HD * (hh + 1), :]
                vt_ref[hh, FOX_HD:, rows] = jnp.ones((FOX_VROWS - FOX_HD, tq), BF16)
            return carry
        lax.fori_loop(0, seq // tq, fill, 0)

    lane = lax.broadcasted_iota(jnp.int32, (tq, LANE), 1)
    krow = lax.broadcasted_iota(jnp.int32, (tq, tq), 0)
    qcol = lax.broadcasted_iota(jnp.int32, (tq, tq), 1)
    causal = krow <= qcol
    nb = tq // SUB

    def prep(i):
        rows = slice(i * tq, (i + 1) * tq)
        qn = _pair_rms_norm(q_ref[rows, :].astype(F32), qg_ref[...]) * ((FOX_HD ** -0.5) * LOG2E)
        aq = aq_ref[rows, :]
        for hh in range(2):
            feat = (lane >= FOX_HD * hh) & (lane < FOX_HD * (hh + 1))
            bias = (lane >= FOX_AUG * hh) & (lane < FOX_AUG * (hh + 1))
            qa_ref[i % 2, hh] = jnp.concatenate([jnp.where(feat, qn, 0.0).astype(BF16),
                                                 jnp.where(bias, aq, jnp.zeros_like(aq))], axis=1)
        m_ref[i % 2] = jnp.full(m_ref.shape[1:], NEG_BIG, F32)
        acc_ref[i % 2] = jnp.zeros(acc_ref.shape[1:], F32)

    def scores(i, j, s_ref):
        kt = ka_ref[pl.ds(pl.multiple_of(j * tq, tq), tq), :]
        for hh in range(2):
            s_ref[hh] = _dot_nt(kt, qa_ref[i % 2, hh])

    def consume(i, j, s_ref, masked):
        cols = pl.ds(pl.multiple_of(j * tq, tq), tq)
        for hh in range(2):
            s = s_ref[hh]
            if masked:
                s = jnp.where(causal, s, NEG_BIG)
            s3 = s.reshape(nb, SUB, tq)
            m_old = m_ref[i % 2, hh]
            m_new = jnp.maximum(m_old, _sublane_all(jnp.max(s3, axis=0), jnp.maximum))
            alpha = jnp.exp2(m_old - m_new)
            p = jnp.exp2(s3 - m_new[None]).reshape(tq, tq).astype(BF16)
            pv = _dot(vt_ref[hh, :, cols], p)
            acc3 = acc_ref[i % 2, hh].reshape(FOX_VROWS // SUB, SUB, tq) * alpha[None]
            acc_ref[i % 2, hh] = acc3.reshape(FOX_VROWS, tq) + pv
            m_ref[i % 2, hh] = m_new

    def finish(i):
        rows = slice(i * tq, (i + 1) * tq)
        o_t = []
        for hh in range(2):
            l = acc_ref[i % 2, hh, FOX_HD:FOX_HD + SUB, :]
            o_t.append((acc_ref[i % 2, hh, :FOX_HD, :].reshape(FOX_HD // SUB, SUB, tq) / l[None]).reshape(FOX_HD, tq))
        o = jnp.concatenate(o_t, axis=0).T
        og = og_ref[rows, :].astype(F32)
        o_ref[rows, :] = (o * _sigmoid(og)).astype(o_ref.dtype)

    cur, other = sa_ref, sb_ref
    prep(0)
    scores(0, 0, cur)
    for i in range(FOX_QTILES):
        qi = FOX_QTILES * w + i

        def pair(t, carry, i=i, cur=cur, other=other):
            j = 2 * t
            scores(i, j + 1, other)
            consume(i, j, cur, False)
            scores(i, j + 2, cur)
            consume(i, j + 1, other, False)
            return carry

        lax.fori_loop(0, qi // 2, pair, 0)
        last = i == FOX_QTILES - 1
        if i % 2 == 0:
            if not last:
                prep(i + 1)
                scores(i + 1, 0, other)
            consume(i, qi, cur, True)
            cur, other = other, cur
        else:
            scores(i, qi, other)
            consume(i, qi - 1, cur, False)
            if not last:
                prep(i + 1)
                scores(i + 1, 0, cur)
            consume(i, qi, other, True)
        finish(i)


def _fox(q, aq, k, ak, v, og, qg2, kg2, batch, seq, tq):
    tg = FOX_QTILES * tq
    ng = seq // tg
    m = batch * seq
    k3, ak3, v3 = (a.reshape(batch, seq, SELF_W) for a in (k, ak, v))
    tile_map = lambda b, p, i: (b * ng + i, p)
    seq_map = lambda b, p, i: (b, 0, p)
    const = lambda b, p, i: (0, 0)
    return pl.pallas_call(
        _fox_kernel,
        grid=(batch, FOX_PAIRS, ng),
        in_specs=[pl.BlockSpec((tg, LANE), tile_map),
                  pl.BlockSpec((tg, LANE), tile_map),
                  pl.BlockSpec((1, seq, LANE), seq_map),
                  pl.BlockSpec((1, seq, LANE), seq_map),
                  pl.BlockSpec((1, seq, LANE), seq_map),
                  pl.BlockSpec((tg, LANE), tile_map),
                  pl.BlockSpec((1, LANE), const),
                  pl.BlockSpec((1, LANE), const)],
        out_specs=pl.BlockSpec((tg, LANE), tile_map),
        out_shape=jax.ShapeDtypeStruct((m, SELF_W), BF16),
        scratch_shapes=[pltpu.VMEM((seq, 2 * LANE), BF16),
                        pltpu.VMEM((2, FOX_VROWS, seq), BF16),
                        pltpu.VMEM((2, 2, tq, 2 * LANE), BF16),
                        pltpu.VMEM((2, 2, SUB, tq), F32),
                        pltpu.VMEM((2, 2, FOX_VROWS, tq), F32),
                        pltpu.VMEM((2, tq, tq), F32),
                        pltpu.VMEM((2, tq, tq), F32)],
        compiler_params=_cparams("parallel", "parallel", "arbitrary"),
        name="fox",
    )(q, aq, k3, ak3, v3, og, qg2, kg2)


def _pad_heads(w, n_heads, d, d_pad):
    lead = w.shape[:-1]
    w = w.reshape(lead + (n_heads, d))
    w = jnp.pad(w, [(0, 0)] * len(lead) + [(0, 0), (0, d_pad - d)])
    return w.reshape(lead + (n_heads * d_pad,))


def _pad_last(w, n):
    return jnp.pad(w, [(0, 0)] * (w.ndim - 1) + [(0, n - w.shape[-1])])


def kernel(x, mem, w_in_a, w_gate_up_a, b_gate_a, gla_norm_g, w_in_b, q_norm_g, w_kv_shared, b_forget, k_norm_g, w_mem_kv, w_out, ln_mix_g, ln_mix_b, ln_ffn_g, ln_ffn_b, w_router, b_router, w_exp_gate, w_exp_up, w_exp_down):
    batch, seq, d = x.shape
    m = batch * seq
    tm = 512
    x2d = x.reshape(m, d)

    wa = w_in_a[0]
    s0, s1, s2, s3, s4 = (GLA_KDIM, 2 * GLA_KDIM, 2 * GLA_KDIM + SELF_W,
                          2 * GLA_KDIM + SELF_W + GLA_GATE_RANK, 2 * GLA_KDIM + 2 * SELF_W + GLA_GATE_RANK)
    w_a = jnp.concatenate([
        _pad_heads(wa[:, :s0], GLA_HEADS, GLA_DK, GLA_DK_PAD),
        _pad_heads(wa[:, s0:s1], GLA_HEADS, GLA_DK, GLA_DK_PAD),
        wa[:, s1:s2], wa[:, s3:s4], wa[:, s4:], _pad_last(wa[:, s2:s3], LANE)], axis=1).astype(BF16)
    wgu = _pad_heads(w_gate_up_a[0], GLA_HEADS, GLA_DK, GLA_DK_PAD)
    wgu = jnp.pad(wgu, ((0, LANE - GLA_GATE_RANK), (0, 0))).astype(BF16)
    bg = _pad_heads(b_gate_a[0], GLA_HEADS, GLA_DK, GLA_DK_PAD).reshape(1, GLA_KPAD)
    gn = jnp.tile(gla_norm_g[0], GLA_HEADS).reshape(1, SELF_W)

    wb = w_in_b[0]
    wq, wog, wmq = (wb[:, :SELF_W].astype(BF16), wb[:, SELF_W:2 * SELF_W].astype(BF16),
                    wb[:, 2 * SELF_W:].astype(BF16))
    wk, wv = w_kv_shared[:, :SELF_W].astype(BF16), w_kv_shared[:, SELF_W:2 * SELF_W].astype(BF16)
    wf = _pad_last(w_kv_shared[:, 2 * SELF_W:], LANE).astype(BF16)
    bf = _pad_last(b_forget, LANE).reshape(1, LANE)
    qg2 = jnp.tile(q_norm_g[0], 2).reshape(1, LANE)
    kg2 = jnp.tile(k_norm_g, 2).reshape(1, LANE)

    w_mkv = jnp.concatenate([w_mem_kv[l] for l in range(DEPTH)], axis=1).astype(BF16)
    wo = w_out.astype(BF16)
    wr_t = w_router.T
    wrh = wr_t.astype(BF16)
    wr2 = jnp.concatenate([wrh, (wr_t - wrh.astype(F32)).astype(BF16)], axis=0)
    br = b_router.reshape(N_EXPERTS, 1)
    row = lambda a: a.reshape(1, d)

    mkv = _mem_kv(mem.reshape(batch * N_MEM, d), w_mkv)

    tmx = 256
    n_tiles = m // tmx + N_CLASSES
    n_rows = n_tiles * tmx

    wr_pad = _pad_last(w_router, LANE).astype(BF16)
    weg, weu, wed = w_exp_gate.astype(BF16), w_exp_up.astype(BF16), w_exp_down.astype(BF16)

    def tail(o, mq, xin, layer):
        x1t, idx, counts = _out_block(o, mq, mkv, layer, xin, wo[layer], row(ln_mix_g[layer]), row(ln_mix_b[layer]),
                                      wrh, wr2, br, seq, tm)
        plan = _moe_plan(counts, tmx, n_tiles)
        xs = _dispatch(plan, idx, x1t, tm, tmx, n_rows)
        ys = _moe_ffn(plan, xs, layer, weg, weu, wed, wr_pad,
                      row(ln_ffn_g[layer]), row(ln_ffn_b[layer]), tmx, n_tiles)
        return _unpermute(plan, idx, ys, m, tm)

    q, k, v, r, mq, g = _proj_a(x2d, w_a, tm)
    o = _gla(q, k, v, r, g, wgu, bg, gn, batch, seq, tm)
    xa = tail(o, mq, x2d, 0)

    qb, kb, vb, og, mqb, aq, ak = _proj_b(xa, wq, wog, wmq, wk, wv, wf, bf, batch, seq, tm)
    ob = _fox(qb, aq, kb, ak, vb, og, qg2, kg2, batch, seq, tm)
    xb = tail(ob, mqb, xa, 1)
    return xb.reshape(batch, seq, d)
```

```python
import functools
import math

import jax
import jax.numpy as jnp
import numpy as np
from jax import lax
from jax.experimental import pallas as pl
from jax.experimental.pallas import tpu as pltpu

F32 = jnp.float32
BF16 = jnp.bfloat16

D_MODEL = 1024
DEPTH = 2
CHUNK = 64
N_MEM = 256
MEM_HEADS = 4
MEM_W = D_MODEL // 4
MEM_HD = MEM_W // MEM_HEADS
SELF_W = D_MODEL - MEM_W
GLA_HEADS = 4
GLA_KDIM = SELF_W // 2
GLA_DK = GLA_KDIM // GLA_HEADS
GLA_DV = SELF_W // GLA_HEADS
GLA_GATE_RANK = 16
GLA_TAU = 16.0
FOX_HD = 64
FOX_HEADS = SELF_W // FOX_HD
N_EXPERTS = 16
N_GROUPS = 4
EXPERTS_PER_GROUP = N_EXPERTS // N_GROUPS
D_EXPERT = D_MODEL // 2
DN_ALPHA = (2.0 * DEPTH) ** 0.25
LN_EPS = 1e-5
RMS_EPS = 1e-6

LANE = 128
GLA_DK_PAD = LANE
GLA_KPAD = GLA_HEADS * GLA_DK_PAD
GLA_VBLOCKS = SELF_W // LANE
GLA_HEAD_BLOCKS = ((0, 1), (1, 2), (3, 4), (4, 5))
FOX_PAIRS = FOX_HEADS // 2
LOG2E = math.log2(math.e)
NEG_BIG = -1e30
VMEM_LIMIT_BYTES = 48 * 1024 * 1024
TOKEN_TILE = 512
MOE_TILE = 256

NT_DIMS = (((1,), (1,)), ((), ()))
TN_DIMS = (((0,), (0,)), ((), ()))


def _cparams(*sem):
    return pltpu.CompilerParams(dimension_semantics=sem, vmem_limit_bytes=VMEM_LIMIT_BYTES)


def _dot(a, b):
    return jnp.dot(a, b, preferred_element_type=F32)


def _dot_nt(a, b):
    return lax.dot_general(a, b, NT_DIMS, preferred_element_type=F32)


def _dot_tn(a, b):
    return lax.dot_general(a, b, TN_DIMS, preferred_element_type=F32)


def _log_sigmoid(z):
    return jnp.minimum(z, 0.0) - jnp.log(1.0 + jnp.exp(-jnp.abs(z)))


def _sigmoid(z):
    return 1.0 / (1.0 + jnp.exp(-z))


def _split2(x):
    hi = x.astype(BF16)
    lo = (x - hi.astype(F32)).astype(BF16)
    return hi, lo


def _split3(x):
    hi = x.astype(BF16)
    r1 = x - hi.astype(F32)
    mid = r1.astype(BF16)
    lo = (r1 - mid.astype(F32)).astype(BF16)
    return hi, mid, lo


def _layer_norm(x, g, b):
    mu = jnp.mean(x, axis=-1, keepdims=True)
    xc = x - mu
    var = jnp.mean(xc * xc, axis=-1, keepdims=True)
    return xc * lax.rsqrt(var + LN_EPS) * g + b


def _proj_a_kernel(x_ref, w_ref, q_ref, k_ref, v_ref, r_ref, mq_ref, g_ref):
    xb = x_ref[...].astype(BF16)
    c = 0
    for ref in (q_ref, k_ref, v_ref, r_ref, mq_ref, g_ref):
        n = ref.shape[1]
        ref[...] = _dot(xb, w_ref[:, c:c + n]).astype(ref.dtype)
        c += n


def _proj_a(x2d, w, tm):
    m = x2d.shape[0]
    widths = (GLA_KPAD, GLA_KPAD, SELF_W, SELF_W, MEM_W, LANE)
    assert w.shape == (D_MODEL, sum(widths))
    return pl.pallas_call(
        _proj_a_kernel,
        grid=(m // tm,),
        in_specs=[pl.BlockSpec((tm, D_MODEL), lambda i: (i, 0)),
                  pl.BlockSpec(w.shape, lambda i: (0, 0))],
        out_specs=[pl.BlockSpec((tm, n), lambda i: (i, 0)) for n in widths],
        out_shape=[jax.ShapeDtypeStruct((m, n), BF16) for n in widths],
        compiler_params=_cparams("parallel"),
        name="proj_a",
    )(x2d, w)


def _gla_kernel(q_ref, k_ref, v_ref, r_ref, g_ref, wgu_ref, bg_ref, gn_ref, o_ref, st_ref, la_ref):
    tg = q_ref.shape[0]

    @pl.when(pl.program_id(1) == 0)
    def _():
        st_ref[...] = jnp.zeros_like(st_ref)

    z = _dot(g_ref[...], wgu_ref[...]) + bg_ref[...]
    la_ref[...] = _log_sigmoid(z) * (1.0 / GLA_TAU)

    row = lax.broadcasted_iota(jnp.int32, (CHUNK, CHUNK), 0)
    col = lax.broadcasted_iota(jnp.int32, (CHUNK, CHUNK), 1)
    causal = col <= row
    tri = jnp.where(causal, 1.0, 0.0).astype(BF16)
    lo_half = lax.broadcasted_iota(jnp.int32, (CHUNK, LANE), 1) < (LANE // 2)
    scale = GLA_DK ** -0.5

    def chunk(c, carry):
        rows = pl.ds(c * CHUNK, CHUNK)
        la_hi, la_lo = _split2(la_ref[rows, :])
        b = _dot(tri, la_hi) + _dot(tri, la_lo)
        bl = b[CHUNK - 1:CHUNK, :]
        qc = q_ref[rows, :].astype(F32) * scale
        kc = k_ref[rows, :].astype(F32)
        qd = (qc * jnp.exp(b)).astype(BF16)
        ki = (kc * jnp.exp(-b)).astype(BF16)
        ke = (kc * jnp.exp(bl - b)).astype(BF16)
        dec = jnp.exp(bl)
        unit = []
        for h in range(GLA_HEADS):
            sl = slice(GLA_DK_PAD * h, GLA_DK_PAD * (h + 1))
            qh, kih, keh = qd[:, sl], ki[:, sl], ke[:, sl]
            att = jnp.where(causal, _dot_nt(qh, kih), 0.0).astype(BF16)
            for t, blk in enumerate(GLA_HEAD_BLOCKS[h]):
                u = 2 * h + t
                vb = v_ref[rows, LANE * blk:LANE * (blk + 1)]
                st = st_ref[u]
                unit.append(_dot(att, vb) + _dot_nt(qh, st.astype(BF16)))
                st_ref[u] = st * dec[:, sl] + _dot_tn(vb, keh)
        o_blk = [unit[0], jnp.where(lo_half, unit[1], unit[2]), unit[3],
                 unit[4], jnp.where(lo_half, unit[5], unit[6]), unit[7]]
        sq = [o * o for o in o_blk]
        full = [jnp.sum(s, axis=-1, keepdims=True) for s in sq]
        lo1 = jnp.sum(jnp.where(lo_half, sq[1], 0.0), axis=-1, keepdims=True)
        hi1 = jnp.sum(jnp.where(lo_half, 0.0, sq[1]), axis=-1, keepdims=True)
        lo4 = jnp.sum(jnp.where(lo_half, sq[4], 0.0), axis=-1, keepdims=True)
        hi4 = jnp.sum(jnp.where(lo_half, 0.0, sq[4]), axis=-1, keepdims=True)
        ss = [full[0] + lo1, hi1 + full[2], full[3] + lo4, hi4 + full[5]]
        inv = [lax.rsqrt(s * (1.0 / GLA_DV) + RMS_EPS) for s in ss]
        inv_blk = [inv[0], jnp.where(lo_half, inv[0], inv[1]), inv[1],
                   inv[2], jnp.where(lo_half, inv[2], inv[3]), inv[3]]
        for blk in range(GLA_VBLOCKS):
            cs = slice(LANE * blk, LANE * (blk + 1))
            rg = r_ref[rows, cs].astype(F32)
            y = o_blk[blk] * inv_blk[blk] * gn_ref[:, cs]
            o_ref[rows, cs] = (y * (rg * _sigmoid(rg))).astype(o_ref.dtype)
        return carry

    for c in range(tg // CHUNK):
        chunk(c, 0)


def _gla(q, k, v, r, g, wgu, bg, gn, batch, seq, tg):
    m = batch * seq
    nt = seq // tg
    row_map = lambda b, i: (b * nt + i, 0)
    const = lambda b, i: (0, 0)
    return pl.pallas_call(
        _gla_kernel,
        grid=(batch, nt),
        in_specs=[pl.BlockSpec((tg, GLA_KPAD), row_map),
                  pl.BlockSpec((tg, GLA_KPAD), row_map),
                  pl.BlockSpec((tg, SELF_W), row_map),
                  pl.BlockSpec((tg, SELF_W), row_map),
                  pl.BlockSpec((tg, LANE), row_map),
                  pl.BlockSpec(wgu.shape, const),
                  pl.BlockSpec(bg.shape, const),
                  pl.BlockSpec(gn.shape, const)],
        out_specs=pl.BlockSpec((tg, SELF_W), row_map),
        out_shape=jax.ShapeDtypeStruct((m, SELF_W), BF16),
        scratch_shapes=[pltpu.VMEM((2 * GLA_HEADS, LANE, GLA_DK_PAD), F32),
                        pltpu.VMEM((tg, GLA_KPAD), F32)],
        compiler_params=_cparams("parallel", "arbitrary"),
        name="gla",
    )(q, k, v, r, g, wgu, bg, gn)


def _mem_kv_kernel(m_ref, w_ref, o_ref):
    o_ref[...] = _dot(m_ref[...].astype(BF16), w_ref[...]).astype(o_ref.dtype)


def _mem_kv(mem2d, w):
    m, n = mem2d.shape[0], w.shape[1]
    tm = N_MEM
    return pl.pallas_call(
        _mem_kv_kernel,
        grid=(m // tm,),
        in_specs=[pl.BlockSpec((tm, D_MODEL), lambda i: (i, 0)),
                  pl.BlockSpec(w.shape, lambda i: (0, 0))],
        out_specs=pl.BlockSpec((tm, n), lambda i: (i, 0)),
        out_shape=jax.ShapeDtypeStruct((m, n), BF16),
        compiler_params=_cparams("parallel"),
        name="mem_kv",
    )(mem2d, w)


def _top2_sum(a, b, c, d):
    p, q = jnp.maximum(a, b), jnp.minimum(a, b)
    r, s = jnp.maximum(c, d), jnp.minimum(c, d)
    return jnp.maximum(p, r) + jnp.maximum(jnp.minimum(p, r), jnp.maximum(q, s))


PAIRS = tuple((i, j) for i in range(EXPERTS_PER_GROUP) for j in range(i + 1, EXPERTS_PER_GROUP))
N_CLASSES = N_GROUPS * len(PAIRS)
CLS_PAD = 32
SLOT = D_MODEL // LANE


def _to_slots(ref, x):
    rows = x.shape[0]
    for j in range(SLOT):
        ref[pl.ds(j, rows, stride=SLOT), :] = x[:, LANE * j:LANE * (j + 1)]


def _from_slots(ref, rows):
    return jnp.concatenate([ref[pl.ds(j, rows, stride=SLOT), :] for j in range(SLOT)], axis=1)


def _slot(ref, r):
    return ref.at[pl.ds(pl.multiple_of(r * SLOT, SLOT), SLOT)]


CLASS_EA = np.array([EXPERTS_PER_GROUP * (c // len(PAIRS)) + PAIRS[c % len(PAIRS)][0] for c in range(N_CLASSES)], np.int32)
CLASS_EB = np.array([EXPERTS_PER_GROUP * (c // len(PAIRS)) + PAIRS[c % len(PAIRS)][1] for c in range(N_CLASSES)], np.int32)


def _router(sel):
    score = [_top2_sum(*sel[EXPERTS_PER_GROUP * g:EXPERTS_PER_GROUP * (g + 1)]) for g in range(N_GROUPS)]
    picked = []
    for g in range(N_GROUPS):
        ok = None
        for i in range(N_GROUPS):
            if i == g:
                continue
            c = (score[g] > score[i]) if i < g else (score[g] >= score[i])
            ok = c if ok is None else (ok & c)
        picked.append(ok)
    chosen = []
    for g in range(N_GROUPS):
        for j in range(EXPERTS_PER_GROUP):
            ej = EXPERTS_PER_GROUP * g + j
            rank = None
            for i in range(EXPERTS_PER_GROUP):
                if i == j:
                    continue
                ei = EXPERTS_PER_GROUP * g + i
                beats = (sel[ei] >= sel[ej]) if i < j else (sel[ei] > sel[ej])
                beats = jnp.where(beats, 1.0, 0.0)
                rank = beats if rank is None else rank + beats
            chosen.append(picked[g] & (rank < 1.5))
    return [chosen[CLASS_EA[c]] & chosen[CLASS_EB[c]] for c in range(N_CLASSES)]


def _out_kernel(o_ref, mq_ref, mk_ref, mv_ref, x_ref, wo_ref, lng_ref, lnb_ref,
                wrh_ref, wr2_ref, br_ref, x1_ref, idx_ref, cnt_ref, carry_ref):
    tm = o_ref.shape[0]

    @pl.when(pl.program_id(0) == 0)
    def _():
        carry_ref[...] = jnp.zeros_like(carry_ref)

    mq = mq_ref[...]
    mk = mk_ref[...]
    mv = mv_ref[...]
    lane = lax.broadcasted_iota(jnp.int32, (N_MEM, MEM_W), 1)
    m_out = jnp.zeros((tm, MEM_W), F32)
    for h in range(MEM_HEADS):
        in_head = (lane >= MEM_HD * h) & (lane < MEM_HD * (h + 1))
        s = _dot_nt(mq, jnp.where(in_head, mk, jnp.zeros_like(mk))) * (MEM_HD ** -0.5)
        p = jnp.exp(s - jnp.max(s, axis=-1, keepdims=True))
        l = jnp.sum(p, axis=-1, keepdims=True)
        m_out = m_out + _dot(p.astype(BF16), jnp.where(in_head, mv, jnp.zeros_like(mv))) / l
    y = _dot(o_ref[...], wo_ref[:SELF_W, :]) + _dot(m_out.astype(BF16), wo_ref[SELF_W:, :])
    x1 = _layer_norm(DN_ALPHA * x_ref[...] + y, lng_ref[...], lnb_ref[...])
    _to_slots(x1_ref, x1)

    xh, xl = _split2(x1)
    hh_lh = _dot_nt(wr2_ref[...], xh)
    logits = hh_lh[:N_EXPERTS, :] + hh_lh[N_EXPERTS:, :] + _dot_nt(wrh_ref[...], xl)
    sel = _sigmoid(logits) + br_ref[...]
    onehot = _router([sel[e:e + 1, :] for e in range(N_EXPERTS)])

    csub = lax.broadcasted_iota(jnp.int32, (CLS_PAD, tm), 0)
    oh = jnp.zeros((CLS_PAD, tm), F32)
    for c in range(N_CLASSES):
        oh = jnp.where((csub == c) & onehot[c], 1.0, oh)
    row = lax.broadcasted_iota(jnp.int32, (tm, tm), 0)
    col = lax.broadcasted_iota(jnp.int32, (tm, tm), 1)
    tri_u = jnp.where(row <= col, 1.0, 0.0).astype(BF16)
    incl = _dot(oh.astype(BF16), tri_u)
    carry = carry_ref[...]
    rank = jnp.sum(oh * (carry + incl - 1.0), axis=0, keepdims=True)
    cls = jnp.sum(oh * csub.astype(F32), axis=0, keepdims=True)
    idx_ref[0, 0:1, :] = cls.astype(jnp.int32)
    idx_ref[0, 1:2, :] = rank.astype(jnp.int32)
    carry = carry + jnp.sum(oh, axis=1, keepdims=True)
    carry_ref[...] = carry
    cnt_ref[...] = carry[:, :LANE]


def _out_block(o, mq, mkv, layer, x2d, wo, lng, lnb, wrh, wr2, br, seq, tm):
    m = x2d.shape[0]
    nt = seq // tm
    row_map = lambda i: (i, 0)
    const = lambda i: (0, 0)
    return pl.pallas_call(
        _out_kernel,
        grid=(m // tm,),
        in_specs=[pl.BlockSpec((tm, SELF_W), row_map),
                  pl.BlockSpec((tm, MEM_W), row_map),
                  pl.BlockSpec((N_MEM, MEM_W), lambda i: (i // nt, 2 * layer)),
                  pl.BlockSpec((N_MEM, MEM_W), lambda i: (i // nt, 2 * layer + 1)),
                  pl.BlockSpec((tm, D_MODEL), row_map),
                  pl.BlockSpec(wo.shape, const),
                  pl.BlockSpec(lng.shape, const),
                  pl.BlockSpec(lnb.shape, const),
                  pl.BlockSpec(wrh.shape, const),
                  pl.BlockSpec(wr2.shape, const),
                  pl.BlockSpec(br.shape, const)],
        out_specs=[pl.BlockSpec((tm * SLOT, LANE), row_map),
                   pl.BlockSpec((1, 2, tm), lambda i: (i, 0, 0)),
                   pl.BlockSpec((CLS_PAD, LANE), const)],
        out_shape=[jax.ShapeDtypeStruct((m * SLOT, LANE), F32),
                   jax.ShapeDtypeStruct((m // tm, 2, tm), jnp.int32),
                   jax.ShapeDtypeStruct((CLS_PAD, LANE), F32)],
        scratch_shapes=[pltpu.VMEM((CLS_PAD, tm), F32)],
        compiler_params=_cparams("arbitrary"),
        name="out_ln_router",
    )(o, mq, mkv, mkv, x2d, wo, lng, lnb, wrh, wr2, br)


def _moe_plan(counts, tmx, n_tiles):
    cnt = counts[:N_CLASSES, 0].astype(jnp.int32)
    nt = (cnt + (tmx - 1)) // tmx
    tend = jnp.cumsum(nt)
    offs = (tend - nt) * tmx
    n_used = tend[-1:]
    i = jnp.arange(n_tiles + 1, dtype=jnp.int32)
    ieff = jnp.minimum(i, n_used - 1)
    cls = jnp.sum((ieff[:, None] >= tend[None, :]).astype(jnp.int32), axis=1)
    fresh = jnp.concatenate([jnp.ones((1,), jnp.int32), (cls[1:] != cls[:-1]).astype(jnp.int32)])
    last_tile_row = jnp.where(nt > 0, (tend - 1) * tmx, -1)
    spare = n_used + jnp.arange(N_CLASSES, dtype=jnp.int32)
    spare_row = jnp.where(spare < n_tiles, spare * tmx, -1)
    pad = lambda a: jnp.pad(a, (0, CLS_PAD - N_CLASSES))
    return dict(offs=pad(offs), zero_rows=jnp.concatenate([last_tile_row, spare_row]), row_block=ieff,
                ea=jnp.asarray(CLASS_EA)[cls], eb=jnp.asarray(CLASS_EB)[cls], fresh=fresh, n_used=n_used)


ROW_DMA_UNROLL = 16


def _tile_dest(offs_ref, idx_ref, dest_v, dest_s, sem):
    cls = idx_ref[0, 0:1, :]
    dest = idx_ref[0, 1:2, :]
    for c in range(N_CLASSES):
        dest = dest + jnp.where(cls == c, offs_ref[c], 0)
    dest_v[...] = dest
    cp = pltpu.make_async_copy(dest_v, dest_s, sem)
    cp.start()
    cp.wait()


def _issue_row_copies(n_rows, make_copy):
    def group(g, carry):
        base = pl.multiple_of(g * ROW_DMA_UNROLL, ROW_DMA_UNROLL)
        for k in range(ROW_DMA_UNROLL):
            make_copy(base + k).start(priority=k % 2)
        return carry
    lax.fori_loop(0, n_rows // ROW_DMA_UNROLL, group, 0)


def _dispatch_kernel(offs_ref, zrow_ref, idx_ref, x_ref, xs_ref, dest_v, dest_s, zero_ref, sem_i, sem_z, sem):
    tm = x_ref.shape[0] // SLOT
    zrows = zero_ref.shape[0]

    @pl.when(pl.program_id(0) == 0)
    def _():
        zero_ref[...] = jnp.zeros_like(zero_ref)

        def zcopy(c):
            start = pl.multiple_of(zrow_ref[c] * SLOT, zrows)
            return pltpu.make_async_copy(zero_ref, xs_ref.at[pl.ds(start, zrows)], sem_z)

        for c in range(2 * N_CLASSES):
            @pl.when(zrow_ref[c] >= 0)
            def _(c=c):
                zcopy(c).start()
        for c in range(2 * N_CLASSES):
            @pl.when(zrow_ref[c] >= 0)
            def _(c=c):
                zcopy(c).wait()

    _tile_dest(offs_ref, idx_ref, dest_v, dest_s, sem_i)
    _issue_row_copies(tm, lambda r: pltpu.make_async_copy(_slot(x_ref, r), _slot(xs_ref, dest_s[0, r]), sem))
    pltpu.make_async_copy(x_ref, xs_ref.at[pl.ds(0, tm * SLOT)], sem).wait()


def _dispatch(plan, idx, x1t, tm, tmx, n_rows):
    m = x1t.shape[0] // SLOT
    return pl.pallas_call(
        _dispatch_kernel,
        grid_spec=pltpu.PrefetchScalarGridSpec(
            num_scalar_prefetch=2,
            grid=(m // tm,),
            in_specs=[pl.BlockSpec((1, 2, tm), lambda i, *_: (i, 0, 0)),
                      pl.BlockSpec((tm * SLOT, LANE), lambda i, *_: (i, 0))],
            out_specs=pl.BlockSpec(memory_space=pl.ANY),
            scratch_shapes=[pltpu.VMEM((1, tm), jnp.int32),
                            pltpu.SMEM((1, tm), jnp.int32),
                            pltpu.VMEM((tmx * SLOT, LANE), F32),
                            pltpu.SemaphoreType.DMA, pltpu.SemaphoreType.DMA, pltpu.SemaphoreType.DMA]),
        out_shape=jax.ShapeDtypeStruct((n_rows * SLOT, LANE), F32),
        compiler_params=_cparams("arbitrary"),
        name="moe_dispatch",
    )(plan["offs"], plan["zero_rows"], idx, x1t)


def _moe_ffn_kernel(rb_ref, ea_ref, eb_ref, fresh_ref, nu_ref,
                    xs_ref, wga_ref, wua_ref, wgb_ref, wub_ref, wda_ref, wdb_ref, wr_ref, lng_ref, lnb_ref,
                    ys_ref, sg_ref, su_ref, sd_ref, h_ref, x_ref, g_ref):
    i = pl.program_id(0)
    t = i - 1
    n_used = nu_ref[0]
    tmx = ys_ref.shape[0] // SLOT
    slot_a = i & 1
    slot_b = 1 - slot_a

    @pl.when((i < n_used) & (fresh_ref[i] == 1))
    def _():
        for s, (g, u) in enumerate(((wga_ref, wua_ref), (wgb_ref, wub_ref))):
            sg_ref[s] = g[0, 0].astype(BF16)
            su_ref[s] = u[0, 0].astype(BF16)

    @pl.when((t >= 0) & (t < n_used) & (fresh_ref[jnp.maximum(t, 0)] == 1))
    def _():
        for s, d in enumerate((wda_ref, wdb_ref)):
            sd_ref[s] = d[0, 0].astype(BF16)

    def load_tile():
        x = _from_slots(xs_ref, tmx)
        x_ref[slot_a] = x
        aff = _sigmoid(_dot(x.astype(BF16), wr_ref[...]))
        lane = lax.broadcasted_iota(jnp.int32, aff.shape, 1)
        g = [jnp.sum(jnp.where(lane == e_ref[i], aff, 0.0), axis=-1, keepdims=True) for e_ref in (ea_ref, eb_ref)]
        for s in range(2):
            g_ref[slot_a, s] = jnp.broadcast_to(g[s] / (g[0] + g[1]), (tmx, LANE))

    def gate_up():
        xb = x_ref[slot_a].astype(BF16)
        for s in range(2):
            hg = _dot(xb, sg_ref[s])
            hu = _dot(xb, su_ref[s])
            h_ref[slot_a, s] = (hg * _sigmoid(hg) * hu).astype(BF16)

    def down_norm():
        f = None
        for s in range(2):
            fs = g_ref[slot_b, s][:, :1] * _dot(h_ref[slot_b, s], sd_ref[s])
            f = fs if f is None else f + fs
        _to_slots(ys_ref, _layer_norm(DN_ALPHA * x_ref[slot_b] + f, lng_ref[...], lnb_ref[...]))

    @pl.when(i == 0)
    def _():
        load_tile()
        gate_up()

    @pl.when((i >= 1) & (i < n_used))
    def _():
        load_tile()
        down_norm()
        gate_up()

    @pl.when((i >= n_used) & (t < n_used))
    def _():
        down_norm()

    @pl.when(t >= n_used)
    def _():
        ys_ref[...] = jnp.zeros_like(ys_ref)


def _moe_ffn(plan, xs, layer, wg, wu, wd, wr, lng, lnb, tmx, n_tiles):
    prev = lambda i: jnp.maximum(i - 1, 0)
    rows = lambda i, rb, *_: (rb[i], 0)
    wa = lambda i, rb, ea, eb, *_: (layer, ea[i], 0, 0)
    wb = lambda i, rb, ea, eb, *_: (layer, eb[i], 0, 0)
    wa_prev = lambda i, rb, ea, eb, *_: (layer, ea[prev(i)], 0, 0)
    wb_prev = lambda i, rb, ea, eb, *_: (layer, eb[prev(i)], 0, 0)
    const = lambda i, *_: (0, 0)
    gu, dn = (1, 1, D_MODEL, D_EXPERT), (1, 1, D_EXPERT, D_MODEL)
    return pl.pallas_call(
        _moe_ffn_kernel,
        grid_spec=pltpu.PrefetchScalarGridSpec(
            num_scalar_prefetch=5,
            grid=(n_tiles + 1,),
            in_specs=[pl.BlockSpec((tmx * SLOT, LANE), rows),
                      pl.BlockSpec(gu, wa), pl.BlockSpec(gu, wa), pl.BlockSpec(gu, wb), pl.BlockSpec(gu, wb),
                      pl.BlockSpec(dn, wa_prev), pl.BlockSpec(dn, wb_prev),
                      pl.BlockSpec(wr.shape, const),
                      pl.BlockSpec(lng.shape, const), pl.BlockSpec(lnb.shape, const)],
            out_specs=pl.BlockSpec((tmx * SLOT, LANE), lambda i, *_: (prev(i), 0)),
            scratch_shapes=[pltpu.VMEM((2, D_MODEL, D_EXPERT), BF16),
                            pltpu.VMEM((2, D_MODEL, D_EXPERT), BF16),
                            pltpu.VMEM((2, D_EXPERT, D_MODEL), BF16),
                            pltpu.VMEM((2, 2, tmx, D_EXPERT), BF16),
                            pltpu.VMEM((2, tmx, D_MODEL), F32),
                            pltpu.VMEM((2, 2, tmx, LANE), F32)]),
        out_shape=jax.ShapeDtypeStruct(xs.shape, F32),
        compiler_params=_cparams("arbitrary"),
        name="moe_ffn",
    )(plan["row_block"], plan["ea"], plan["eb"], plan["fresh"], plan["n_used"],
      xs, wg, wu, wg, wu, wd, wd, wr, lng, lnb)


def _unpermute_kernel(offs_ref, idx_ref, ys_ref, o_ref, dest_v, dest_s, buf_ref, sem_i, sem):
    tm = o_ref.shape[0]
    _tile_dest(offs_ref, idx_ref, dest_v, dest_s, sem_i)
    _issue_row_copies(tm, lambda r: pltpu.make_async_copy(_slot(ys_ref, dest_s[0, r]), _slot(buf_ref, r), sem))
    pltpu.make_async_copy(ys_ref.at[pl.ds(0, tm * SLOT)], buf_ref, sem).wait()
    o_ref[...] = _from_slots(buf_ref, tm)


def _unpermute(plan, idx, ys, m, tm):
    return pl.pallas_call(
        _unpermute_kernel,
        grid_spec=pltpu.PrefetchScalarGridSpec(
            num_scalar_prefetch=1,
            grid=(m // tm,),
            in_specs=[pl.BlockSpec((1, 2, tm), lambda i, *_: (i, 0, 0)),
                      pl.BlockSpec(memory_space=pl.ANY)],
            out_specs=pl.BlockSpec((tm, D_MODEL), lambda i, *_: (i, 0)),
            scratch_shapes=[pltpu.VMEM((1, tm), jnp.int32),
                            pltpu.SMEM((1, tm), jnp.int32),
                            pltpu.VMEM((tm * SLOT, LANE), F32),
                            pltpu.SemaphoreType.DMA, pltpu.SemaphoreType.DMA]),
        out_shape=jax.ShapeDtypeStruct((m, D_MODEL), F32),
        compiler_params=_cparams("arbitrary"),
        name="moe_unpermute",
    )(plan["offs"], idx, ys)


FOX_AUG = 6


def _fox_placement():
    pl_q = np.zeros((3 * LANE, SELF_W), np.float32)
    pl_k = np.zeros((3 * LANE, SELF_W), np.float32)
    ones_q = np.zeros((1, SELF_W), np.float32)
    ones_k = np.zeros((1, SELF_W), np.float32)
    for h in range(FOX_HEADS):
        base = LANE * (h // 2) + FOX_AUG * (h % 2)
        for piece in range(3):
            pl_q[piece * LANE + h, base + piece] = 1.0
            pl_k[piece * LANE + h, base + 3 + piece] = -1.0
            ones_q[0, base + 3 + piece] = 1.0
            ones_k[0, base + piece] = 1.0
    return (jnp.asarray(pl_q, BF16), jnp.asarray(pl_k, BF16), jnp.asarray(ones_q), jnp.asarray(ones_k))


def _proj_b_kernel(x_ref, wq_ref, wog_ref, wmq_ref, wk_ref, wv_ref, wf_ref, bf_ref, plq_ref, plk_ref,
                   oq_ref, ok_ref, q_ref, k_ref, v_ref, og_ref, mq_ref, aq_ref, ak_ref, carry_ref):
    tm = x_ref.shape[0]

    @pl.when(pl.program_id(1) == 0)
    def _():
        carry_ref[...] = jnp.zeros_like(carry_ref)

    xb = x_ref[...].astype(BF16)
    for w_ref, o_ref in ((wq_ref, q_ref), (wk_ref, k_ref), (wv_ref, v_ref), (wog_ref, og_ref), (wmq_ref, mq_ref)):
        o_ref[...] = _dot(xb, w_ref[...]).astype(o_ref.dtype)

    log_f = _log_sigmoid(_dot(xb, wf_ref[...]) + bf_ref[...])
    row = lax.broadcasted_iota(jnp.int32, (tm, tm), 0)
    col = lax.broadcasted_iota(jnp.int32, (tm, tm), 1)
    tri = jnp.where(col <= row, 1.0, 0.0).astype(BF16)
    f_hi, f_mid, f_lo = _split3(log_f)
    cum = carry_ref[...] + (_dot(tri, f_hi) + _dot(tri, f_mid) + _dot(tri, f_lo))
    carry_ref[...] = cum[tm - 1:tm, :]
    c3 = jnp.concatenate(_split3(cum * LOG2E), axis=1)
    aq_ref[...] = (_dot(c3, plq_ref[...]) + oq_ref[...]).astype(aq_ref.dtype)
    ak_ref[...] = (_dot(c3, plk_ref[...]) + ok_ref[...]).astype(ak_ref.dtype)


def _proj_b(x2d, wq, wog, wmq, wk, wv, wf, bf, batch, seq, tm):
    m = batch * seq
    nt = seq // tm
    row_map = lambda b, i: (b * nt + i, 0)
    const = lambda b, i: (0, 0)
    consts = (wq, wog, wmq, wk, wv, wf, bf) + _fox_placement()
    widths = (SELF_W, SELF_W, SELF_W, SELF_W, MEM_W, SELF_W, SELF_W)
    return pl.pallas_call(
        _proj_b_kernel,
        grid=(batch, nt),
        in_specs=[pl.BlockSpec((tm, D_MODEL), row_map)] + [pl.BlockSpec(a.shape, const) for a in consts],
        out_specs=[pl.BlockSpec((tm, n), row_map) for n in widths],
        out_shape=[jax.ShapeDtypeStruct((m, n), BF16) for n in widths],
        scratch_shapes=[pltpu.VMEM((1, LANE), F32)],
        compiler_params=_cparams("parallel", "arbitrary"),
        name="proj_b",
    )(x2d, *consts)


def _pair_rms_norm(t, gain):
    lo_half = lax.broadcasted_iota(jnp.int32, t.shape, 1) < FOX_HD
    sq = t * t
    ss_lo = jnp.sum(jnp.where(lo_half, sq, 0.0), axis=-1, keepdims=True)
    ss_hi = jnp.sum(jnp.where(lo_half, 0.0, sq), axis=-1, keepdims=True)
    ss = jnp.where(lo_half, ss_lo, ss_hi)
    return t * lax.rsqrt(ss * (1.0 / FOX_HD) + RMS_EPS) * gain


SUB = 8
FOX_VROWS = FOX_HD + 16
FOX_QTILES = 8


def _sublane_all(x, op):
    shift = SUB // 2
    while shift:
        x = op(x, pltpu.roll(x, shift, 0))
        shift //= 2
    return x


def _fox_kernel(q_ref, aq_ref, k_ref, ak_ref, v_ref, og_ref, qg_ref, kg_ref, o_ref,
                ka_ref, vt_ref, qa_ref, m_ref, acc_ref, sa_ref, sb_ref):
    tq = sa_ref.shape[1]
    seq = k_ref.shape[1]
    w = pl.program_id(2)

    @pl.when(w == 0)
    def _():
        def fill(c, carry):
            rows = pl.ds(pl.multiple_of(c * tq, tq), tq)
            kn = _pair_rms_norm(k_ref[0, rows, :].astype(F32), kg_ref[...])
            ka_ref[rows, :] = jnp.concatenate([kn.astype(BF16), ak_ref[0, rows, :]], axis=1)
            vt = v_ref[0, rows, :].astype(F32).T.astype(BF16)
            for hh in range(2):
                vt_ref[hh, :FOX_HD, rows] = vt[FOX_HD * hh:FOX_HD * (hh + 1), :]
                vt_ref[hh, FOX_HD:, rows] = jnp.ones((FOX_VROWS - FOX_HD, tq), BF16)
            return carry
        lax.fori_loop(0, seq // tq, fill, 0)

    lane = lax.broadcasted_iota(jnp.int32, (tq, LANE), 1)
    krow = lax.broadcasted_iota(jnp.int32, (tq, tq), 0)
    qcol = lax.broadcasted_iota(jnp.int32, (tq, tq), 1)
    causal = krow <= qcol
    nb = tq // SUB

    def prep(i):
        rows = slice(i * tq, (i + 1) * tq)
        qn = _pair_rms_norm(q_ref[rows, :].astype(F32), qg_ref[...]) * ((FOX_HD ** -0.5) * LOG2E)
        aq = aq_ref[rows, :]
        for hh in range(2):
            feat = (lane >= FOX_HD * hh) & (lane < FOX_HD * (hh + 1))
            bias = (lane >= FOX_AUG * hh) & (lane < FOX_AUG * (hh + 1))
            qa_ref[i % 2, hh] = jnp.concatenate([jnp.where(feat, qn, 0.0).astype(BF16),
                                                 jnp.where(bias, aq, jnp.zeros_like(aq))], axis=1)
        m_ref[i % 2] = jnp.full(m_ref.shape[1:], NEG_BIG, F32)
        acc_ref[i % 2] = jnp.zeros(acc_ref.shape[1:], F32)

    def scores(i, j, s_ref):
        kt = ka_ref[pl.ds(pl.multiple_of(j * tq, tq), tq), :]
        for hh in range(2):
            s_ref[hh] = _dot_nt(kt, qa_ref[i % 2, hh])

    def consume(i, j, s_ref, masked):
        cols = pl.ds(pl.multiple_of(j * tq, tq), tq)
        for hh in range(2):
            s = s_ref[hh]
            if masked:
                s = jnp.where(causal, s, NEG_BIG)
            s3 = s.reshape(nb, SUB, tq)
            m_old = m_ref[i % 2, hh]
            m_new = jnp.maximum(m_old, _sublane_all(jnp.max(s3, axis=0), jnp.maximum))
            alpha = jnp.exp2(m_old - m_new)
            p = jnp.exp2(s3 - m_new[None]).reshape(tq, tq).astype(BF16)
            pv = _dot(vt_ref[hh, :, cols], p)
            acc3 = acc_ref[i % 2, hh].reshape(FOX_VROWS // SUB, SUB, tq) * alpha[None]
            acc_ref[i % 2, hh] = acc3.reshape(FOX_VROWS, tq) + pv
            m_ref[i % 2, hh] = m_new

    def finish(i):
        rows = slice(i * tq, (i + 1) * tq)
        o_t = []
        for hh in range(2):
            l = acc_ref[i % 2, hh, FOX_HD:FOX_HD + SUB, :]
            o_t.append((acc_ref[i % 2, hh, :FOX_HD, :].reshape(FOX_HD // SUB, SUB, tq) / l[None]).reshape(FOX_HD, tq))
        o = jnp.concatenate(o_t, axis=0).T
        og = og_ref[rows, :].astype(F32)
        o_ref[rows, :] = (o * _sigmoid(og)).astype(o_ref.dtype)

    cur, other = sa_ref, sb_ref
    prep(0)
    scores(0, 0, cur)
    for i in range(FOX_QTILES):
        qi = FOX_QTILES * w + i

        def pair(t, carry, i=i, cur=cur, other=other):
            j = 2 * t
            scores(i, j + 1, other)
            consume(i, j, cur, False)
            scores(i, j + 2, cur)
            consume(i, j + 1, other, False)
            return carry

        lax.fori_loop(0, qi // 2, pair, 0)
        last = i == FOX_QTILES - 1
        if i % 2 == 0:
            if not last:
                prep(i + 1)
                scores(i + 1, 0, other)
            consume(i, qi, cur, True)
            cur, other = other, cur
        else:
            scores(i, qi, other)
            consume(i, qi - 1, cur, False)
            if not last:
                prep(i + 1)
                scores(i + 1, 0, cur)
            consume(i, qi, other, True)
        finish(i)


def _fox(q, aq, k, ak, v, og, qg2, kg2, batch, seq, tq):
    tg = FOX_QTILES * tq
    ng = seq // tg
    m = batch * seq
    k3, ak3, v3 = (a.reshape(batch, seq, SELF_W) for a in (k, ak, v))
    tile_map = lambda b, p, i: (b * ng + i, p)
    seq_map = lambda b, p, i: (b, 0, p)
    const = lambda b, p, i: (0, 0)
    return pl.pallas_call(
        _fox_kernel,
        grid=(batch, FOX_PAIRS, ng),
        in_specs=[pl.BlockSpec((tg, LANE), tile_map),
                  pl.BlockSpec((tg, LANE), tile_map),
                  pl.BlockSpec((1, seq, LANE), seq_map),
                  pl.BlockSpec((1, seq, LANE), seq_map),
                  pl.BlockSpec((1, seq, LANE), seq_map),
                  pl.BlockSpec((tg, LANE), tile_map),
                  pl.BlockSpec((1, LANE), const),
                  pl.BlockSpec((1, LANE), const)],
        out_specs=pl.BlockSpec((tg, LANE), tile_map),
        out_shape=jax.ShapeDtypeStruct((m, SELF_W), BF16),
        scratch_shapes=[pltpu.VMEM((seq, 2 * LANE), BF16),
                        pltpu.VMEM((2, FOX_VROWS, seq), BF16),
                        pltpu.VMEM((2, 2, tq, 2 * LANE), BF16),
                        pltpu.VMEM((2, 2, SUB, tq), F32),
                        pltpu.VMEM((2, 2, FOX_VROWS, tq), F32),
                        pltpu.VMEM((2, tq, tq), F32),
                        pltpu.VMEM((2, tq, tq), F32)],
        compiler_params=_cparams("parallel", "parallel", "arbitrary"),
        name="fox",
    )(q, aq, k3, ak3, v3, og, qg2, kg2)


def _pad_heads(w, n_heads, d, d_pad):
    lead = w.shape[:-1]
    w = w.reshape(lead + (n_heads, d))
    w = jnp.pad(w, [(0, 0)] * len(lead) + [(0, 0), (0, d_pad - d)])
    return w.reshape(lead + (n_heads * d_pad,))


def _pad_last(w, n):
    return jnp.pad(w, [(0, 0)] * (w.ndim - 1) + [(0, n - w.shape[-1])])


def kernel(x, mem, w_in_a, w_gate_up_a, b_gate_a, gla_norm_g, w_in_b, q_norm_g, w_kv_shared, b_forget, k_norm_g, w_mem_kv, w_out, ln_mix_g, ln_mix_b, ln_ffn_g, ln_ffn_b, w_router, b_router, w_exp_gate, w_exp_up, w_exp_down):
    batch, seq, d = x.shape
    m = batch * seq
    tm, tmx = TOKEN_TILE, MOE_TILE
    assert d == D_MODEL and seq % (FOX_QTILES * tm) == 0 and m % tmx == 0
    x2d = x.reshape(m, d)

    wa = w_in_a[0]
    s0, s1, s2, s3, s4 = (GLA_KDIM, 2 * GLA_KDIM, 2 * GLA_KDIM + SELF_W,
                          2 * GLA_KDIM + SELF_W + GLA_GATE_RANK, 2 * GLA_KDIM + 2 * SELF_W + GLA_GATE_RANK)
    w_a = jnp.concatenate([
        _pad_heads(wa[:, :s0], GLA_HEADS, GLA_DK, GLA_DK_PAD),
        _pad_heads(wa[:, s0:s1], GLA_HEADS, GLA_DK, GLA_DK_PAD),
        wa[:, s1:s2], wa[:, s3:s4], wa[:, s4:], _pad_last(wa[:, s2:s3], LANE)], axis=1).astype(BF16)
    wgu = _pad_heads(w_gate_up_a[0], GLA_HEADS, GLA_DK, GLA_DK_PAD)
    wgu = jnp.pad(wgu, ((0, LANE - GLA_GATE_RANK), (0, 0))).astype(BF16)
    bg = _pad_heads(b_gate_a[0], GLA_HEADS, GLA_DK, GLA_DK_PAD).reshape(1, GLA_KPAD)
    gn = jnp.tile(gla_norm_g[0], GLA_HEADS).reshape(1, SELF_W)

    wb = w_in_b[0]
    wq, wog, wmq = (wb[:, :SELF_W].astype(BF16), wb[:, SELF_W:2 * SELF_W].astype(BF16),
                    wb[:, 2 * SELF_W:].astype(BF16))
    wk, wv = w_kv_shared[:, :SELF_W].astype(BF16), w_kv_shared[:, SELF_W:2 * SELF_W].astype(BF16)
    wf = _pad_last(w_kv_shared[:, 2 * SELF_W:], LANE).astype(BF16)
    bf = _pad_last(b_forget, LANE).reshape(1, LANE)
    qg2 = jnp.tile(q_norm_g[0], 2).reshape(1, LANE)
    kg2 = jnp.tile(k_norm_g, 2).reshape(1, LANE)

    w_mkv = jnp.concatenate([w_mem_kv[l] for l in range(DEPTH)], axis=1).astype(BF16)
    wo = w_out.astype(BF16)
    wr_t = w_router.T
    wrh = wr_t.astype(BF16)
    wr2 = jnp.concatenate([wrh, (wr_t - wrh.astype(F32)).astype(BF16)], axis=0)
    br = b_router.reshape(N_EXPERTS, 1)
    row = lambda a: a.reshape(1, d)

    mkv = _mem_kv(mem.reshape(batch * N_MEM, d), w_mkv)

    n_tiles = m // tmx + N_CLASSES
    n_rows = n_tiles * tmx

    wr_pad = _pad_last(w_router, LANE).astype(BF16)

    def tail(o, mq, xin, layer):
        x1t, idx, counts = _out_block(o, mq, mkv, layer, xin, wo[layer], row(ln_mix_g[layer]), row(ln_mix_b[layer]),
                                      wrh, wr2, br, seq, tm)
        plan = _moe_plan(counts, tmx, n_tiles)
        xs = _dispatch(plan, idx, x1t, tm, tmx, n_rows)
        ys = _moe_ffn(plan, xs, layer, w_exp_gate, w_exp_up, w_exp_down, wr_pad,
                      row(ln_ffn_g[layer]), row(ln_ffn_b[layer]), tmx, n_tiles)
        return _unpermute(plan, idx, ys, m, tm)

    q, k, v, r, mq, g = _proj_a(x2d, w_a, tm)
    o = _gla(q, k, v, r, g, wgu, bg, gn, batch, seq, tm)
    xa = tail(o, mq, x2d, 0)

    qb, kb, vb, og, mqb, aq, ak = _proj_b(xa, wq, wog, wmq, wk, wv, wf, bf, batch, seq, tm)
    ob = _fox(qb, aq, kb, ak, vb, og, qg2, kg2, batch, seq, tm)
    xb = tail(ob, mqb, xa, 1)
    return xb.reshape(batch, seq, d)
```

```python
import functools
import math

import jax
import jax.numpy as jnp
import numpy as np
from jax import lax
from jax.experimental import pallas as pl
from jax.experimental.pallas import tpu as pltpu

F32 = jnp.float32
BF16 = jnp.bfloat16

D_MODEL = 1024
DEPTH = 2
CHUNK = 64
N_MEM = 256
MEM_HEADS = 4
MEM_W = D_MODEL // 4
MEM_HD = MEM_W // MEM_HEADS
SELF_W = D_MODEL - MEM_W
GLA_HEADS = 4
GLA_KDIM = SELF_W // 2
GLA_DK = GLA_KDIM // GLA_HEADS
GLA_DV = SELF_W // GLA_HEADS
GLA_GATE_RANK = 16
GLA_TAU = 16.0
FOX_HD = 64
FOX_HEADS = SELF_W // FOX_HD
N_EXPERTS = 16
N_GROUPS = 4
EXPERTS_PER_GROUP = N_EXPERTS // N_GROUPS
D_EXPERT = D_MODEL // 2
DN_ALPHA = (2.0 * DEPTH) ** 0.25
LN_EPS = 1e-5
RMS_EPS = 1e-6

LANE = 128
GLA_DK_PAD = LANE
GLA_KPAD = GLA_HEADS * GLA_DK_PAD
GLA_VBLOCKS = SELF_W // LANE
GLA_HEAD_BLOCKS = ((0, 1), (1, 2), (3, 4), (4, 5))
FOX_PAIRS = FOX_HEADS // 2
LOG2E = math.log2(math.e)
NEG_BIG = -1e30
VMEM_LIMIT_BYTES = 48 * 1024 * 1024
TOKEN_TILE = 512
MOE_TILE = 256

NT_DIMS = (((1,), (1,)), ((), ()))
TN_DIMS = (((0,), (0,)), ((), ()))


def _cparams(*sem):
    return pltpu.CompilerParams(dimension_semantics=sem, vmem_limit_bytes=VMEM_LIMIT_BYTES)


def _dot(a, b):
    return jnp.dot(a, b, preferred_element_type=F32)


def _dot_nt(a, b):
    return lax.dot_general(a, b, NT_DIMS, preferred_element_type=F32)


def _dot_tn(a, b):
    return lax.dot_general(a, b, TN_DIMS, preferred_element_type=F32)


def _log_sigmoid(z):
    return jnp.minimum(z, 0.0) - jnp.log(1.0 + jnp.exp(-jnp.abs(z)))


def _sigmoid(z):
    return 1.0 / (1.0 + jnp.exp(-z))


def _split2(x):
    hi = x.astype(BF16)
    lo = (x - hi.astype(F32)).astype(BF16)
    return hi, lo


def _split3(x):
    hi = x.astype(BF16)
    r1 = x - hi.astype(F32)
    mid = r1.astype(BF16)
    lo = (r1 - mid.astype(F32)).astype(BF16)
    return hi, mid, lo


def _layer_norm(x, g, b):
    mu = jnp.mean(x, axis=-1, keepdims=True)
    xc = x - mu
    var = jnp.mean(xc * xc, axis=-1, keepdims=True)
    return xc * lax.rsqrt(var + LN_EPS) * g + b


def _proj_a_kernel(x_ref, w_ref, q_ref, k_ref, v_ref, r_ref, mq_ref, g_ref):
    xb = x_ref[...].astype(BF16)
    c = 0
    for ref in (q_ref, k_ref, v_ref, r_ref, mq_ref, g_ref):
        n = ref.shape[1]
        ref[...] = _dot(xb, w_ref[:, c:c + n]).astype(ref.dtype)
        c += n


def _proj_a(x2d, w, tm):
    m = x2d.shape[0]
    widths = (GLA_KPAD, GLA_KPAD, SELF_W, SELF_W, MEM_W, LANE)
    assert w.shape == (D_MODEL, sum(widths))
    return pl.pallas_call(
        _proj_a_kernel,
        grid=(m // tm,),
        in_specs=[pl.BlockSpec((tm, D_MODEL), lambda i: (i, 0)),
                  pl.BlockSpec(w.shape, lambda i: (0, 0))],
        out_specs=[pl.BlockSpec((tm, n), lambda i: (i, 0)) for n in widths],
        out_shape=[jax.ShapeDtypeStruct((m, n), BF16) for n in widths],
        compiler_params=_cparams("parallel"),
        name="proj_a",
    )(x2d, w)


def _gla_kernel(q_ref, k_ref, v_ref, r_ref, g_ref, wgu_ref, bg_ref, gn_ref, o_ref, st_ref, la_ref):
    tg = q_ref.shape[0]

    @pl.when(pl.program_id(1) == 0)
    def _():
        st_ref[...] = jnp.zeros_like(st_ref)

    z = _dot(g_ref[...], wgu_ref[...]) + bg_ref[...]
    la_ref[...] = _log_sigmoid(z) * (1.0 / GLA_TAU)

    row = lax.broadcasted_iota(jnp.int32, (CHUNK, CHUNK), 0)
    col = lax.broadcasted_iota(jnp.int32, (CHUNK, CHUNK), 1)
    causal = col <= row
    tri = jnp.where(causal, 1.0, 0.0).astype(BF16)
    lo_half = lax.broadcasted_iota(jnp.int32, (CHUNK, LANE), 1) < (LANE // 2)
    scale = GLA_DK ** -0.5

    def chunk(c, carry):
        rows = pl.ds(c * CHUNK, CHUNK)
        la_hi, la_lo = _split2(la_ref[rows, :])
        b = _dot(tri, la_hi) + _dot(tri, la_lo)
        bl = b[CHUNK - 1:CHUNK, :]
        qc = q_ref[rows, :].astype(F32) * scale
        kc = k_ref[rows, :].astype(F32)
        qd = (qc * jnp.exp(b)).astype(BF16)
        ki = (kc * jnp.exp(-b)).astype(BF16)
        ke = (kc * jnp.exp(bl - b)).astype(BF16)
        dec = jnp.exp(bl)
        unit = []
        for h in range(GLA_HEADS):
            sl = slice(GLA_DK_PAD * h, GLA_DK_PAD * (h + 1))
            qh, kih, keh = qd[:, sl], ki[:, sl], ke[:, sl]
            att = jnp.where(causal, _dot_nt(qh, kih), 0.0).astype(BF16)
            for t, blk in enumerate(GLA_HEAD_BLOCKS[h]):
                u = 2 * h + t
                vb = v_ref[rows, LANE * blk:LANE * (blk + 1)]
                st = st_ref[u]
                unit.append(_dot(att, vb) + _dot_nt(qh, st.astype(BF16)))
                st_ref[u] = st * dec[:, sl] + _dot_tn(vb, keh)
        o_blk = [unit[0], jnp.where(lo_half, unit[1], unit[2]), unit[3],
                 unit[4], jnp.where(lo_half, unit[5], unit[6]), unit[7]]
        sq = [o * o for o in o_blk]
        full = [jnp.sum(s, axis=-1, keepdims=True) for s in sq]
        lo1 = jnp.sum(jnp.where(lo_half, sq[1], 0.0), axis=-1, keepdims=True)
        hi1 = jnp.sum(jnp.where(lo_half, 0.0, sq[1]), axis=-1, keepdims=True)
        lo4 = jnp.sum(jnp.where(lo_half, sq[4], 0.0), axis=-1, keepdims=True)
        hi4 = jnp.sum(jnp.where(lo_half, 0.0, sq[4]), axis=-1, keepdims=True)
        ss = [full[0] + lo1, hi1 + full[2], full[3] + lo4, hi4 + full[5]]
        inv = [lax.rsqrt(s * (1.0 / GLA_DV) + RMS_EPS) for s in ss]
        inv_blk = [inv[0], jnp.where(lo_half, inv[0], inv[1]), inv[1],
                   inv[2], jnp.where(lo_half, inv[2], inv[3]), inv[3]]
        for blk in range(GLA_VBLOCKS):
            cs = slice(LANE * blk, LANE * (blk + 1))
            rg = r_ref[rows, cs].astype(F32)
            y = o_blk[blk] * inv_blk[blk] * gn_ref[:, cs]
            o_ref[rows, cs] = (y * (rg * _sigmoid(rg))).astype(o_ref.dtype)
        return carry

    for c in range(tg // CHUNK):
        chunk(c, 0)


def _gla(q, k, v, r, g, wgu, bg, gn, batch, seq, tg):
    m = batch * seq
    nt = seq // tg
    row_map = lambda b, i: (b * nt + i, 0)
    const = lambda b, i: (0, 0)
    return pl.pallas_call(
        _gla_kernel,
        grid=(batch, nt),
        in_specs=[pl.BlockSpec((tg, GLA_KPAD), row_map),
                  pl.BlockSpec((tg, GLA_KPAD), row_map),
                  pl.BlockSpec((tg, SELF_W), row_map),
                  pl.BlockSpec((tg, SELF_W), row_map),
                  pl.BlockSpec((tg, LANE), row_map),
                  pl.BlockSpec(wgu.shape, const),
                  pl.BlockSpec(bg.shape, const),
                  pl.BlockSpec(gn.shape, const)],
        out_specs=pl.BlockSpec((tg, SELF_W), row_map),
        out_shape=jax.ShapeDtypeStruct((m, SELF_W), BF16),
        scratch_shapes=[pltpu.VMEM((2 * GLA_HEADS, LANE, GLA_DK_PAD), F32),
                        pltpu.VMEM((tg, GLA_KPAD), F32)],
        compiler_params=_cparams("parallel", "arbitrary"),
        name="gla",
    )(q, k, v, r, g, wgu, bg, gn)


def _mem_kv_kernel(m_ref, w_ref, o_ref):
    o_ref[...] = _dot(m_ref[...].astype(BF16), w_ref[...]).astype(o_ref.dtype)


def _mem_kv(mem2d, w):
    m, n = mem2d.shape[0], w.shape[1]
    tm = N_MEM
    return pl.pallas_call(
        _mem_kv_kernel,
        grid=(m // tm,),
        in_specs=[pl.BlockSpec((tm, D_MODEL), lambda i: (i, 0)),
                  pl.BlockSpec(w.shape, lambda i: (0, 0))],
        out_specs=pl.BlockSpec((tm, n), lambda i: (i, 0)),
        out_shape=jax.ShapeDtypeStruct((m, n), BF16),
        compiler_params=_cparams("parallel"),
        name="mem_kv",
    )(mem2d, w)


def _top2_sum(a, b, c, d):
    p, q = jnp.maximum(a, b), jnp.minimum(a, b)
    r, s = jnp.maximum(c, d), jnp.minimum(c, d)
    return jnp.maximum(p, r) + jnp.maximum(jnp.minimum(p, r), jnp.maximum(q, s))


PAIRS = tuple((i, j) for i in range(EXPERTS_PER_GROUP) for j in range(i + 1, EXPERTS_PER_GROUP))
N_CLASSES = N_GROUPS * len(PAIRS)
CLS_PAD = 32
SLOT = D_MODEL // LANE


def _to_slots(ref, x):
    rows = x.shape[0]
    for j in range(SLOT):
        ref[pl.ds(j, rows, stride=SLOT), :] = x[:, LANE * j:LANE * (j + 1)]


def _from_slots(ref, rows):
    return jnp.concatenate([ref[pl.ds(j, rows, stride=SLOT), :] for j in range(SLOT)], axis=1)


def _slot(ref, r):
    return ref.at[pl.ds(pl.multiple_of(r * SLOT, SLOT), SLOT)]


CLASS_EA = np.array([EXPERTS_PER_GROUP * (c // len(PAIRS)) + PAIRS[c % len(PAIRS)][0] for c in range(N_CLASSES)], np.int32)
CLASS_EB = np.array([EXPERTS_PER_GROUP * (c // len(PAIRS)) + PAIRS[c % len(PAIRS)][1] for c in range(N_CLASSES)], np.int32)


def _router(sel):
    score = [_top2_sum(*sel[EXPERTS_PER_GROUP * g:EXPERTS_PER_GROUP * (g + 1)]) for g in range(N_GROUPS)]
    picked = []
    for g in range(N_GROUPS):
        ok = None
        for i in range(N_GROUPS):
            if i == g:
                continue
            c = (score[g] > score[i]) if i < g else (score[g] >= score[i])
            ok = c if ok is None else (ok & c)
        picked.append(ok)
    chosen = []
    for g in range(N_GROUPS):
        for j in range(EXPERTS_PER_GROUP):
            ej = EXPERTS_PER_GROUP * g + j
            rank = None
            for i in range(EXPERTS_PER_GROUP):
                if i == j:
                    continue
                ei = EXPERTS_PER_GROUP * g + i
                beats = (sel[ei] >= sel[ej]) if i < j else (sel[ei] > sel[ej])
                beats = jnp.where(beats, 1.0, 0.0)
                rank = beats if rank is None else rank + beats
            chosen.append(picked[g] & (rank < 1.5))
    return [chosen[CLASS_EA[c]] & chosen[CLASS_EB[c]] for c in range(N_CLASSES)]


def _out_kernel(o_ref, mq_ref, mk_ref, mv_ref, x_ref, wo_ref, lng_ref, lnb_ref,
                wrh_ref, wr2_ref, br_ref, x1_ref, idx_ref, cnt_ref, carry_ref):
    tm = o_ref.shape[0]

    @pl.when(pl.program_id(0) == 0)
    def _():
        carry_ref[...] = jnp.zeros_like(carry_ref)

    mq = mq_ref[...]
    mk = mk_ref[...]
    mv = mv_ref[...]
    lane = lax.broadcasted_iota(jnp.int32, (N_MEM, MEM_W), 1)
    m_out = jnp.zeros((tm, MEM_W), F32)
    for h in range(MEM_HEADS):
        in_head = (lane >= MEM_HD * h) & (lane < MEM_HD * (h + 1))
        s = _dot_nt(mq, jnp.where(in_head, mk, jnp.zeros_like(mk))) * (MEM_HD ** -0.5)
        p = jnp.exp(s - jnp.max(s, axis=-1, keepdims=True))
        l = jnp.sum(p, axis=-1, keepdims=True)
        m_out = m_out + _dot(p.astype(BF16), jnp.where(in_head, mv, jnp.zeros_like(mv))) / l
    y = _dot(o_ref[...], wo_ref[:SELF_W, :]) + _dot(m_out.astype(BF16), wo_ref[SELF_W:, :])
    x1 = _layer_norm(DN_ALPHA * x_ref[...] + y, lng_ref[...], lnb_ref[...])
    _to_slots(x1_ref, x1)

    xh, xl = _split2(x1)
    hh_lh = _dot_nt(wr2_ref[...], xh)
    logits = hh_lh[:N_EXPERTS, :] + hh_lh[N_EXPERTS:, :] + _dot_nt(wrh_ref[...], xl)
    sel = _sigmoid(logits) + br_ref[...]
    onehot = _router([sel[e:e + 1, :] for e in range(N_EXPERTS)])

    csub = lax.broadcasted_iota(jnp.int32, (CLS_PAD, tm), 0)
    oh = jnp.zeros((CLS_PAD, tm), F32)
    for c in range(N_CLASSES):
        oh = jnp.where((csub == c) & onehot[c], 1.0, oh)
    row = lax.broadcasted_iota(jnp.int32, (tm, tm), 0)
    col = lax.broadcasted_iota(jnp.int32, (tm, tm), 1)
    tri_u = jnp.where(row <= col, 1.0, 0.0).astype(BF16)
    incl = _dot(oh.astype(BF16), tri_u)
    carry = carry_ref[...]
    rank = jnp.sum(oh * (carry + incl - 1.0), axis=0, keepdims=True)
    cls = jnp.sum(oh * csub.astype(F32), axis=0, keepdims=True)
    idx_ref[0, 0:1, :] = cls.astype(jnp.int32)
    idx_ref[0, 1:2, :] = rank.astype(jnp.int32)
    carry = carry + jnp.sum(oh, axis=1, keepdims=True)
    carry_ref[...] = carry
    cnt_ref[...] = carry[:, :LANE]


def _out_block(o, mq, mkv, layer, x2d, wo, lng, lnb, wrh, wr2, br, seq, tm):
    m = x2d.shape[0]
    nt = seq // tm
    row_map = lambda i: (i, 0)
    const = lambda i: (0, 0)
    return pl.pallas_call(
        _out_kernel,
        grid=(m // tm,),
        in_specs=[pl.BlockSpec((tm, SELF_W), row_map),
                  pl.BlockSpec((tm, MEM_W), row_map),
                  pl.BlockSpec((N_MEM, MEM_W), lambda i: (i // nt, 2 * layer)),
                  pl.BlockSpec((N_MEM, MEM_W), lambda i: (i // nt, 2 * layer + 1)),
                  pl.BlockSpec((tm, D_MODEL), row_map),
                  pl.BlockSpec(wo.shape, const),
                  pl.BlockSpec(lng.shape, const),
                  pl.BlockSpec(lnb.shape, const),
                  pl.BlockSpec(wrh.shape, const),
                  pl.BlockSpec(wr2.shape, const),
                  pl.BlockSpec(br.shape, const)],
        out_specs=[pl.BlockSpec((tm * SLOT, LANE), row_map),
                   pl.BlockSpec((1, 2, tm), lambda i: (i, 0, 0)),
                   pl.BlockSpec((CLS_PAD, LANE), const)],
        out_shape=[jax.ShapeDtypeStruct((m * SLOT, LANE), F32),
                   jax.ShapeDtypeStruct((m // tm, 2, tm), jnp.int32),
                   jax.ShapeDtypeStruct((CLS_PAD, LANE), F32)],
        scratch_shapes=[pltpu.VMEM((CLS_PAD, tm), F32)],
        compiler_params=_cparams("arbitrary"),
        name="out_ln_router",
    )(o, mq, mkv, mkv, x2d, wo, lng, lnb, wrh, wr2, br)


def _moe_plan(counts, tmx, n_tiles):
    cnt = counts[:N_CLASSES, 0].astype(jnp.int32)
    nt = (cnt + (tmx - 1)) // tmx
    tend = jnp.cumsum(nt)
    offs = (tend - nt) * tmx
    n_used = tend[-1:]
    i = jnp.arange(n_tiles + 1, dtype=jnp.int32)
    ieff = jnp.minimum(i, n_used - 1)
    cls = jnp.sum((ieff[:, None] >= tend[None, :]).astype(jnp.int32), axis=1)
    fresh = jnp.concatenate([jnp.ones((1,), jnp.int32), (cls[1:] != cls[:-1]).astype(jnp.int32)])
    last_tile_row = jnp.where(nt > 0, (tend - 1) * tmx, -1)
    spare = n_used + jnp.arange(N_CLASSES, dtype=jnp.int32)
    spare_row = jnp.where(spare < n_tiles, spare * tmx, -1)
    pad = lambda a: jnp.pad(a, (0, CLS_PAD - N_CLASSES))
    return dict(offs=pad(offs), zero_rows=jnp.concatenate([last_tile_row, spare_row]), row_block=ieff,
                ea=jnp.asarray(CLASS_EA)[cls], eb=jnp.asarray(CLASS_EB)[cls], fresh=fresh, n_used=n_used)


ROW_DMA_UNROLL = 16


def _dest_copy(dest_v, dest_s, sem, slot):
    return pltpu.make_async_copy(dest_v.at[slot], dest_s.at[slot], sem.at[slot])


def _tile_dest(offs_ref, idx_ref, dest_v, dest_s, sem):
    i = pl.program_id(0)
    slot = i & 1

    def start(t, s):
        cls = idx_ref[t, 0:1, :]
        dest = idx_ref[t, 1:2, :]
        for c in range(N_CLASSES):
            dest = dest + jnp.where(cls == c, offs_ref[c], 0)
        dest_v[s] = dest
        _dest_copy(dest_v, dest_s, sem, s).start()

    @pl.when(i == 0)
    def _():
        start(0, 0)

    @pl.when(i + 1 < pl.num_programs(0))
    def _():
        start(i + 1, 1 - slot)

    _dest_copy(dest_v, dest_s, sem, slot).wait()
    return slot


def _issue_row_copies(n_rows, make_copy):
    def group(g, carry):
        base = pl.multiple_of(g * ROW_DMA_UNROLL, ROW_DMA_UNROLL)
        for k in range(ROW_DMA_UNROLL):
            make_copy(base + k).start(priority=k % 2)
        return carry
    lax.fori_loop(0, n_rows // ROW_DMA_UNROLL, group, 0)


def _dispatch_kernel(offs_ref, zrow_ref, idx_ref, x_ref, xs_ref, dest_v, dest_s, zero_ref, sem_i, sem_z, sem):
    tm = x_ref.shape[0] // SLOT
    zrows = zero_ref.shape[0]

    @pl.when(pl.program_id(0) == 0)
    def _():
        zero_ref[...] = jnp.zeros_like(zero_ref)

        def zcopy(c):
            start = pl.multiple_of(zrow_ref[c] * SLOT, zrows)
            return pltpu.make_async_copy(zero_ref, xs_ref.at[pl.ds(start, zrows)], sem_z)

        for c in range(2 * N_CLASSES):
            @pl.when(zrow_ref[c] >= 0)
            def _(c=c):
                zcopy(c).start()
        for c in range(2 * N_CLASSES):
            @pl.when(zrow_ref[c] >= 0)
            def _(c=c):
                zcopy(c).wait()

    ds = _tile_dest(offs_ref, idx_ref, dest_v, dest_s, sem_i)
    _issue_row_copies(tm, lambda r: pltpu.make_async_copy(_slot(x_ref, r), _slot(xs_ref, dest_s[ds, 0, r]), sem))
    pltpu.make_async_copy(x_ref, xs_ref.at[pl.ds(0, tm * SLOT)], sem).wait()


def _dispatch(plan, idx, x1t, tm, tmx, n_rows):
    m = x1t.shape[0] // SLOT
    return pl.pallas_call(
        _dispatch_kernel,
        grid_spec=pltpu.PrefetchScalarGridSpec(
            num_scalar_prefetch=2,
            grid=(m // tm,),
            in_specs=[pl.BlockSpec(idx.shape, lambda i, *_: (0, 0, 0)),
                      pl.BlockSpec((tm * SLOT, LANE), lambda i, *_: (i, 0))],
            out_specs=pl.BlockSpec(memory_space=pl.ANY),
            scratch_shapes=[pltpu.VMEM((2, 1, tm), jnp.int32),
                            pltpu.SMEM((2, 1, tm), jnp.int32),
                            pltpu.VMEM((tmx * SLOT, LANE), F32),
                            pltpu.SemaphoreType.DMA((2,)), pltpu.SemaphoreType.DMA, pltpu.SemaphoreType.DMA]),
        out_shape=jax.ShapeDtypeStruct((n_rows * SLOT, LANE), F32),
        compiler_params=_cparams("arbitrary"),
        name="moe_dispatch",
    )(plan["offs"], plan["zero_rows"], idx, x1t)


def _moe_ffn_kernel(rb_ref, ea_ref, eb_ref, fresh_ref, nu_ref,
                    xs_ref, wga_ref, wua_ref, wgb_ref, wub_ref, wda_ref, wdb_ref, wr_ref, lng_ref, lnb_ref,
                    ys_ref, sg_ref, su_ref, sd_ref, h_ref, x_ref, g_ref):
    i = pl.program_id(0)
    t = i - 1
    n_used = nu_ref[0]
    tmx = ys_ref.shape[0] // SLOT
    slot_a = i & 1
    slot_b = 1 - slot_a

    @pl.when((i < n_used) & (fresh_ref[i] == 1))
    def _():
        for s, (g, u) in enumerate(((wga_ref, wua_ref), (wgb_ref, wub_ref))):
            sg_ref[s] = g[0, 0].astype(BF16)
            su_ref[s] = u[0, 0].astype(BF16)

    @pl.when((t >= 0) & (t < n_used) & (fresh_ref[jnp.maximum(t, 0)] == 1))
    def _():
        for s, d in enumerate((wda_ref, wdb_ref)):
            sd_ref[s] = d[0, 0].astype(BF16)

    def load_tile():
        x = _from_slots(xs_ref, tmx)
        x_ref[slot_a] = x
        aff = _sigmoid(_dot(x.astype(BF16), wr_ref[...]))
        lane = lax.broadcasted_iota(jnp.int32, aff.shape, 1)
        g = [jnp.sum(jnp.where(lane == e_ref[i], aff, 0.0), axis=-1, keepdims=True) for e_ref in (ea_ref, eb_ref)]
        for s in range(2):
            g_ref[slot_a, s] = jnp.broadcast_to(g[s] / (g[0] + g[1]), (tmx, LANE))

    def gate_up():
        xb = x_ref[slot_a].astype(BF16)
        for s in range(2):
            hg = _dot(xb, sg_ref[s])
            hu = _dot(xb, su_ref[s])
            h_ref[slot_a, s] = (hg * _sigmoid(hg) * hu).astype(BF16)

    def down_norm():
        f = None
        for s in range(2):
            fs = g_ref[slot_b, s][:, :1] * _dot(h_ref[slot_b, s], sd_ref[s])
            f = fs if f is None else f + fs
        _to_slots(ys_ref, _layer_norm(DN_ALPHA * x_ref[slot_b] + f, lng_ref[...], lnb_ref[...]))

    @pl.when(i == 0)
    def _():
        load_tile()
        gate_up()

    @pl.when((i >= 1) & (i < n_used))
    def _():
        load_tile()
        down_norm()
        gate_up()

    @pl.when((i >= n_used) & (t < n_used))
    def _():
        down_norm()

    @pl.when(t >= n_used)
    def _():
        ys_ref[...] = jnp.zeros_like(ys_ref)


def _moe_ffn(plan, xs, layer, wg, wu, wd, wr, lng, lnb, tmx, n_tiles):
    prev = lambda i: jnp.maximum(i - 1, 0)
    rows = lambda i, rb, *_: (rb[i], 0)
    wa = lambda i, rb, ea, eb, *_: (layer, ea[i], 0, 0)
    wb = lambda i, rb, ea, eb, *_: (layer, eb[i], 0, 0)
    wa_prev = lambda i, rb, ea, eb, *_: (layer, ea[prev(i)], 0, 0)
    wb_prev = lambda i, rb, ea, eb, *_: (layer, eb[prev(i)], 0, 0)
    const = lambda i, *_: (0, 0)
    gu, dn = (1, 1, D_MODEL, D_EXPERT), (1, 1, D_EXPERT, D_MODEL)
    return pl.pallas_call(
        _moe_ffn_kernel,
        grid_spec=pltpu.PrefetchScalarGridSpec(
            num_scalar_prefetch=5,
            grid=(n_tiles + 1,),
            in_specs=[pl.BlockSpec((tmx * SLOT, LANE), rows),
                      pl.BlockSpec(gu, wa), pl.BlockSpec(gu, wa), pl.BlockSpec(gu, wb), pl.BlockSpec(gu, wb),
                      pl.BlockSpec(dn, wa_prev), pl.BlockSpec(dn, wb_prev),
                      pl.BlockSpec(wr.shape, const),
                      pl.BlockSpec(lng.shape, const), pl.BlockSpec(lnb.shape, const)],
            out_specs=pl.BlockSpec((tmx * SLOT, LANE), lambda i, *_: (prev(i), 0)),
            scratch_shapes=[pltpu.VMEM((2, D_MODEL, D_EXPERT), BF16),
                            pltpu.VMEM((2, D_MODEL, D_EXPERT), BF16),
                            pltpu.VMEM((2, D_EXPERT, D_MODEL), BF16),
                            pltpu.VMEM((2, 2, tmx, D_EXPERT), BF16),
                            pltpu.VMEM((2, tmx, D_MODEL), F32),
                            pltpu.VMEM((2, 2, tmx, LANE), F32)]),
        out_shape=jax.ShapeDtypeStruct(xs.shape, F32),
        compiler_params=_cparams("arbitrary"),
        name="moe_ffn",
    )(plan["row_block"], plan["ea"], plan["eb"], plan["fresh"], plan["n_used"],
      xs, wg, wu, wg, wu, wd, wd, wr, lng, lnb)


def _unpermute_kernel(offs_ref, idx_ref, ys_ref, o_ref, dest_v, dest_s, buf_ref, sem_i, sem):
    tm = o_ref.shape[0]
    ds = _tile_dest(offs_ref, idx_ref, dest_v, dest_s, sem_i)
    _issue_row_copies(tm, lambda r: pltpu.make_async_copy(_slot(ys_ref, dest_s[ds, 0, r]), _slot(buf_ref, r), sem))
    pltpu.make_async_copy(ys_ref.at[pl.ds(0, tm * SLOT)], buf_ref, sem).wait()
    o_ref[...] = _from_slots(buf_ref, tm)


def _unpermute(plan, idx, ys, m, tm):
    return pl.pallas_call(
        _unpermute_kernel,
        grid_spec=pltpu.PrefetchScalarGridSpec(
            num_scalar_prefetch=1,
            grid=(m // tm,),
            in_specs=[pl.BlockSpec(idx.shape, lambda i, *_: (0, 0, 0)),
                      pl.BlockSpec(memory_space=pl.ANY)],
            out_specs=pl.BlockSpec((tm, D_MODEL), lambda i, *_: (i, 0)),
            scratch_shapes=[pltpu.VMEM((2, 1, tm), jnp.int32),
                            pltpu.SMEM((2, 1, tm), jnp.int32),
                            pltpu.VMEM((tm * SLOT, LANE), F32),
                            pltpu.SemaphoreType.DMA((2,)), pltpu.SemaphoreType.DMA]),
        out_shape=jax.ShapeDtypeStruct((m, D_MODEL), F32),
        compiler_params=_cparams("arbitrary"),
        name="moe_unpermute",
    )(plan["offs"], idx, ys)


FOX_AUG = 6


def _fox_placement():
    pl_q = np.zeros((3 * LANE, SELF_W), np.float32)
    pl_k = np.zeros((3 * LANE, SELF_W), np.float32)
    ones_q = np.zeros((1, SELF_W), np.float32)
    ones_k = np.zeros((1, SELF_W), np.float32)
    for h in range(FOX_HEADS):
        base = LANE * (h // 2) + FOX_AUG * (h % 2)
        for piece in range(3):
            pl_q[piece * LANE + h, base + piece] = 1.0
            pl_k[piece * LANE + h, base + 3 + piece] = -1.0
            ones_q[0, base + 3 + piece] = 1.0
            ones_k[0, base + piece] = 1.0
    return (jnp.asarray(pl_q, BF16), jnp.asarray(pl_k, BF16), jnp.asarray(ones_q), jnp.asarray(ones_k))


def _proj_b_kernel(x_ref, wq_ref, wog_ref, wmq_ref, wk_ref, wv_ref, wf_ref, bf_ref, plq_ref, plk_ref,
                   oq_ref, ok_ref, q_ref, k_ref, v_ref, og_ref, mq_ref, aq_ref, ak_ref, carry_ref):
    tm = x_ref.shape[0]

    @pl.when(pl.program_id(1) == 0)
    def _():
        carry_ref[...] = jnp.zeros_like(carry_ref)

    xb = x_ref[...].astype(BF16)
    for w_ref, o_ref in ((wq_ref, q_ref), (wk_ref, k_ref), (wv_ref, v_ref), (wog_ref, og_ref), (wmq_ref, mq_ref)):
        o_ref[...] = _dot(xb, w_ref[...]).astype(o_ref.dtype)

    log_f = _log_sigmoid(_dot(xb, wf_ref[...]) + bf_ref[...])
    row = lax.broadcasted_iota(jnp.int32, (tm, tm), 0)
    col = lax.broadcasted_iota(jnp.int32, (tm, tm), 1)
    tri = jnp.where(col <= row, 1.0, 0.0).astype(BF16)
    f_hi, f_mid, f_lo = _split3(log_f)
    cum = carry_ref[...] + (_dot(tri, f_hi) + _dot(tri, f_mid) + _dot(tri, f_lo))
    carry_ref[...] = cum[tm - 1:tm, :]
    c3 = jnp.concatenate(_split3(cum * LOG2E), axis=1)
    aq_ref[...] = (_dot(c3, plq_ref[...]) + oq_ref[...]).astype(aq_ref.dtype)
    ak_ref[...] = (_dot(c3, plk_ref[...]) + ok_ref[...]).astype(ak_ref.dtype)


def _proj_b(x2d, wq, wog, wmq, wk, wv, wf, bf, batch, seq, tm):
    m = batch * seq
    nt = seq // tm
    row_map = lambda b, i: (b * nt + i, 0)
    const = lambda b, i: (0, 0)
    consts = (wq, wog, wmq, wk, wv, wf, bf) + _fox_placement()
    widths = (SELF_W, SELF_W, SELF_W, SELF_W, MEM_W, SELF_W, SELF_W)
    return pl.pallas_call(
        _proj_b_kernel,
        grid=(batch, nt),
        in_specs=[pl.BlockSpec((tm, D_MODEL), row_map)] + [pl.BlockSpec(a.shape, const) for a in consts],
        out_specs=[pl.BlockSpec((tm, n), row_map) for n in widths],
        out_shape=[jax.ShapeDtypeStruct((m, n), BF16) for n in widths],
        scratch_shapes=[pltpu.VMEM((1, LANE), F32)],
        compiler_params=_cparams("parallel", "arbitrary"),
        name="proj_b",
    )(x2d, *consts)


def _pair_rms_norm(t, gain):
    lo_half = lax.broadcasted_iota(jnp.int32, t.shape, 1) < FOX_HD
    sq = t * t
    ss_lo = jnp.sum(jnp.where(lo_half, sq, 0.0), axis=-1, keepdims=True)
    ss_hi = jnp.sum(jnp.where(lo_half, 0.0, sq), axis=-1, keepdims=True)
    ss = jnp.where(lo_half, ss_lo, ss_hi)
    return t * lax.rsqrt(ss * (1.0 / FOX_HD) + RMS_EPS) * gain


SUB = 8
FOX_VROWS = FOX_HD + 16
FOX_QTILES = 8


def _sublane_all(x, op):
    shift = SUB // 2
    while shift:
        x = op(x, pltpu.roll(x, shift, 0))
        shift //= 2
    return x


def _fox_kernel(q_ref, aq_ref, k_ref, ak_ref, v_ref, og_ref, qg_ref, kg_ref, o_ref,
                ka_ref, vt_ref, qa_ref, m_ref, acc_ref, sa_ref, sb_ref):
    tq = sa_ref.shape[1]
    seq = k_ref.shape[1]
    w = pl.program_id(2)

    @pl.when(w == 0)
    def _():
        def fill(c, carry):
            rows = pl.ds(pl.multiple_of(c * tq, tq), tq)
            kn = _pair_rms_norm(k_ref[0, rows, :].astype(F32), kg_ref[...])
            ka_ref[rows, :] = jnp.concatenate([kn.astype(BF16), ak_ref[0, rows, :]], axis=1)
            vt = v_ref[0, rows, :].astype(F32).T.astype(BF16)
            for hh in range(2):
                vt_ref[hh, :FOX_HD, rows] = vt[FOX_HD * hh:FOX_HD * (hh + 1), :]
                vt_ref[hh, FOX_HD:, rows] = jnp.ones((FOX_VROWS - FOX_HD, tq), BF16)
            return carry
        lax.fori_loop(0, seq // tq, fill, 0)

    lane = lax.broadcasted_iota(jnp.int32, (tq, LANE), 1)
    krow = lax.broadcasted_iota(jnp.int32, (tq, tq), 0)
    qcol = lax.broadcasted_iota(jnp.int32, (tq, tq), 1)
    causal = krow <= qcol
    nb = tq // SUB

    def prep(i):
        rows = slice(i * tq, (i + 1) * tq)
        qn = _pair_rms_norm(q_ref[rows, :].astype(F32), qg_ref[...]) * ((FOX_HD ** -0.5) * LOG2E)
        aq = aq_ref[rows, :]
        for hh in range(2):
            feat = (lane >= FOX_HD * hh) & (lane < FOX_HD * (hh + 1))
            bias = (lane >= FOX_AUG * hh) & (lane < FOX_AUG * (hh + 1))
            qa_ref[i % 2, hh] = jnp.concatenate([jnp.where(feat, qn, 0.0).astype(BF16),
                                                 jnp.where(bias, aq, jnp.zeros_like(aq))], axis=1)
        m_ref[i % 2] = jnp.full(m_ref.shape[1:], NEG_BIG, F32)
        acc_ref[i % 2] = jnp.zeros(acc_ref.shape[1:], F32)

    def scores(i, j, s_ref):
        kt = ka_ref[pl.ds(pl.multiple_of(j * tq, tq), tq), :]
        for hh in range(2):
            s_ref[hh] = _dot_nt(kt, qa_ref[i % 2, hh])

    def consume(i, j, s_ref, masked):
        cols = pl.ds(pl.multiple_of(j * tq, tq), tq)
        for hh in range(2):
            s = s_ref[hh]
            if masked:
                s = jnp.where(causal, s, NEG_BIG)
            s3 = s.reshape(nb, SUB, tq)
            m_old = m_ref[i % 2, hh]
            m_new = jnp.maximum(m_old, _sublane_all(jnp.max(s3, axis=0), jnp.maximum))
            alpha = jnp.exp2(m_old - m_new)
            p = jnp.exp2(s3 - m_new[None]).reshape(tq, tq).astype(BF16)
            pv = _dot(vt_ref[hh, :, cols], p)
            acc3 = acc_ref[i % 2, hh].reshape(FOX_VROWS // SUB, SUB, tq) * alpha[None]
            acc_ref[i % 2, hh] = acc3.reshape(FOX_VROWS, tq) + pv
            m_ref[i % 2, hh] = m_new

    def finish(i):
        rows = slice(i * tq, (i + 1) * tq)
        o_t = []
        for hh in range(2):
            l = acc_ref[i % 2, hh, FOX_HD:FOX_HD + SUB, :]
            o_t.append((acc_ref[i % 2, hh, :FOX_HD, :].reshape(FOX_HD // SUB, SUB, tq) / l[None]).reshape(FOX_HD, tq))
        o = jnp.concatenate(o_t, axis=0).T
        og = og_ref[rows, :].astype(F32)
        o_ref[rows, :] = (o * _sigmoid(og)).astype(o_ref.dtype)

    cur, other = sa_ref, sb_ref
    prep(0)
    scores(0, 0, cur)
    for i in range(FOX_QTILES):
        qi = FOX_QTILES * w + i

        def pair(t, carry, i=i, cur=cur, other=other):
            j = 2 * t
            scores(i, j + 1, other)
            consume(i, j, cur, False)
            scores(i, j + 2, cur)
            consume(i, j + 1, other, False)
            return carry

        lax.fori_loop(0, qi // 2, pair, 0)
        last = i == FOX_QTILES - 1
        if i % 2 == 0:
            if not last:
                prep(i + 1)
                scores(i + 1, 0, other)
            consume(i, qi, cur, True)
            cur, other = other, cur
        else:
            scores(i, qi, other)
            consume(i, qi - 1, cur, False)
            if not last:
                prep(i + 1)
                scores(i + 1, 0, cur)
            consume(i, qi, other, True)
        finish(i)


def _fox(q, aq, k, ak, v, og, qg2, kg2, batch, seq, tq):
    tg = FOX_QTILES * tq
    ng = seq // tg
    m = batch * seq
    k3, ak3, v3 = (a.reshape(batch, seq, SELF_W) for a in (k, ak, v))
    tile_map = lambda b, p, i: (b * ng + i, p)
    seq_map = lambda b, p, i: (b, 0, p)
    const = lambda b, p, i: (0, 0)
    return pl.pallas_call(
        _fox_kernel,
        grid=(batch, FOX_PAIRS, ng),
        in_specs=[pl.BlockSpec((tg, LANE), tile_map),
                  pl.BlockSpec((tg, LANE), tile_map),
                  pl.BlockSpec((1, seq, LANE), seq_map),
                  pl.BlockSpec((1, seq, LANE), seq_map),
                  pl.BlockSpec((1, seq, LANE), seq_map),
                  pl.BlockSpec((tg, LANE), tile_map),
                  pl.BlockSpec((1, LANE), const),
                  pl.BlockSpec((1, LANE), const)],
        out_specs=pl.BlockSpec((tg, LANE), tile_map),
        out_shape=jax.ShapeDtypeStruct((m, SELF_W), BF16),
        scratch_shapes=[pltpu.VMEM((seq, 2 * LANE), BF16),
                        pltpu.VMEM((2, FOX_VROWS, seq), BF16),
                        pltpu.VMEM((2, 2, tq, 2 * LANE), BF16),
                        pltpu.VMEM((2, 2, SUB, tq), F32),
                        pltpu.VMEM((2, 2, FOX_VROWS, tq), F32),
                        pltpu.VMEM((2, tq, tq), F32),
                        pltpu.VMEM((2, tq, tq), F32)],
        compiler_params=_cparams("parallel", "parallel", "arbitrary"),
        name="fox",
    )(q, aq, k3, ak3, v3, og, qg2, kg2)


def _pad_heads(w, n_heads, d, d_pad):
    lead = w.shape[:-1]
    w = w.reshape(lead + (n_heads, d))
    w = jnp.pad(w, [(0, 0)] * len(lead) + [(0, 0), (0, d_pad - d)])
    return w.reshape(lead + (n_heads * d_pad,))


def _pad_last(w, n):
    return jnp.pad(w, [(0, 0)] * (w.ndim - 1) + [(0, n - w.shape[-1])])


def kernel(x, mem, w_in_a, w_gate_up_a, b_gate_a, gla_norm_g, w_in_b, q_norm_g, w_kv_shared, b_forget, k_norm_g, w_mem_kv, w_out, ln_mix_g, ln_mix_b, ln_ffn_g, ln_ffn_b, w_router, b_router, w_exp_gate, w_exp_up, w_exp_down):
    batch, seq, d = x.shape
    m = batch * seq
    tm, tmx = TOKEN_TILE, MOE_TILE
    assert d == D_MODEL and seq % (FOX_QTILES * tm) == 0 and m % tmx == 0
    x2d = x.reshape(m, d)

    wa = w_in_a[0]
    s0, s1, s2, s3, s4 = (GLA_KDIM, 2 * GLA_KDIM, 2 * GLA_KDIM + SELF_W,
                          2 * GLA_KDIM + SELF_W + GLA_GATE_RANK, 2 * GLA_KDIM + 2 * SELF_W + GLA_GATE_RANK)
    w_a = jnp.concatenate([
        _pad_heads(wa[:, :s0], GLA_HEADS, GLA_DK, GLA_DK_PAD),
        _pad_heads(wa[:, s0:s1], GLA_HEADS, GLA_DK, GLA_DK_PAD),
        wa[:, s1:s2], wa[:, s3:s4], wa[:, s4:], _pad_last(wa[:, s2:s3], LANE)], axis=1).astype(BF16)
    wgu = _pad_heads(w_gate_up_a[0], GLA_HEADS, GLA_DK, GLA_DK_PAD)
    wgu = jnp.pad(wgu, ((0, LANE - GLA_GATE_RANK), (0, 0))).astype(BF16)
    bg = _pad_heads(b_gate_a[0], GLA_HEADS, GLA_DK, GLA_DK_PAD).reshape(1, GLA_KPAD)
    gn = jnp.tile(gla_norm_g[0], GLA_HEADS).reshape(1, SELF_W)

    wb = w_in_b[0]
    wq, wog, wmq = (wb[:, :SELF_W].astype(BF16), wb[:, SELF_W:2 * SELF_W].astype(BF16),
                    wb[:, 2 * SELF_W:].astype(BF16))
    wk, wv = w_kv_shared[:, :SELF_W].astype(BF16), w_kv_shared[:, SELF_W:2 * SELF_W].astype(BF16)
    wf = _pad_last(w_kv_shared[:, 2 * SELF_W:], LANE).astype(BF16)
    bf = _pad_last(b_forget, LANE).reshape(1, LANE)
    qg2 = jnp.tile(q_norm_g[0], 2).reshape(1, LANE)
    kg2 = jnp.tile(k_norm_g, 2).reshape(1, LANE)

    w_mkv = jnp.concatenate([w_mem_kv[l] for l in range(DEPTH)], axis=1).astype(BF16)
    wo = w_out.astype(BF16)
    wr_t = w_router.T
    wrh = wr_t.astype(BF16)
    wr2 = jnp.concatenate([wrh, (wr_t - wrh.astype(F32)).astype(BF16)], axis=0)
    br = b_router.reshape(N_EXPERTS, 1)
    row = lambda a: a.reshape(1, d)

    mkv = _mem_kv(mem.reshape(batch * N_MEM, d), w_mkv)

    n_tiles = m // tmx + N_CLASSES
    n_rows = n_tiles * tmx

    wr_pad = _pad_last(w_router, LANE).astype(BF16)

    def tail(o, mq, xin, layer):
        x1t, idx, counts = _out_block(o, mq, mkv, layer, xin, wo[layer], row(ln_mix_g[layer]), row(ln_mix_b[layer]),
                                      wrh, wr2, br, seq, tm)
        plan = _moe_plan(counts, tmx, n_tiles)
        xs = _dispatch(plan, idx, x1t, tm, tmx, n_rows)
        ys = _moe_ffn(plan, xs, layer, w_exp_gate, w_exp_up, w_exp_down, wr_pad,
                      row(ln_ffn_g[layer]), row(ln_ffn_b[layer]), tmx, n_tiles)
        return _unpermute(plan, idx, ys, m, tm)

    q, k, v, r, mq, g = _proj_a(x2d, w_a, tm)
    o = _gla(q, k, v, r, g, wgu, bg, gn, batch, seq, tm)
    xa = tail(o, mq, x2d, 0)

    qb, kb, vb, og, mqb, aq, ak = _proj_b(xa, wq, wog, wmq, wk, wv, wf, bf, batch, seq, tm)
    ob = _fox(qb, aq, kb, ak, vb, og, qg2, kg2, batch, seq, tm)
    xb = tail(ob, mqb, xa, 1)
    return xb.reshape(batch, seq, d)
```

```python
import functools
import math

import jax
import jax.numpy as jnp
import numpy as np
from jax import lax
from jax.experimental import pallas as pl
from jax.experimental.pallas import tpu as pltpu

F32 = jnp.float32
BF16 = jnp.bfloat16

D_MODEL = 1024
DEPTH = 2
CHUNK = 64
N_MEM = 256
MEM_HEADS = 4
MEM_W = D_MODEL // 4
MEM_HD = MEM_W // MEM_HEADS
SELF_W = D_MODEL - MEM_W
GLA_HEADS = 4
GLA_KDIM = SELF_W // 2
GLA_DK = GLA_KDIM // GLA_HEADS
GLA_DV = SELF_W // GLA_HEADS
GLA_GATE_RANK = 16
GLA_TAU = 16.0
FOX_HD = 64
FOX_HEADS = SELF_W // FOX_HD
N_EXPERTS = 16
N_GROUPS = 4
EXPERTS_PER_GROUP = N_EXPERTS // N_GROUPS
D_EXPERT = D_MODEL // 2
DN_ALPHA = (2.0 * DEPTH) ** 0.25
LN_EPS = 1e-5
RMS_EPS = 1e-6

LANE = 128
GLA_DK_PAD = LANE
GLA_KPAD = GLA_HEADS * GLA_DK_PAD
GLA_VBLOCKS = SELF_W // LANE
GLA_HEAD_BLOCKS = ((0, 1), (1, 2), (3, 4), (4, 5))
FOX_PAIRS = FOX_HEADS // 2
LOG2E = math.log2(math.e)
NEG_BIG = -1e30
VMEM_LIMIT_BYTES = 48 * 1024 * 1024
TOKEN_TILE = 512
MOE_TILE = 256

NT_DIMS = (((1,), (1,)), ((), ()))
TN_DIMS = (((0,), (0,)), ((), ()))


def _cparams(*sem):
    return pltpu.CompilerParams(dimension_semantics=sem, vmem_limit_bytes=VMEM_LIMIT_BYTES)


def _dot(a, b):
    return jnp.dot(a, b, preferred_element_type=F32)


def _dot_nt(a, b):
    return lax.dot_general(a, b, NT_DIMS, preferred_element_type=F32)


def _dot_tn(a, b):
    return lax.dot_general(a, b, TN_DIMS, preferred_element_type=F32)


def _log_sigmoid(z):
    return jnp.minimum(z, 0.0) - jnp.log(1.0 + jnp.exp(-jnp.abs(z)))


def _sigmoid(z):
    return 1.0 / (1.0 + jnp.exp(-z))


def _split2(x):
    hi = x.astype(BF16)
    lo = (x - hi.astype(F32)).astype(BF16)
    return hi, lo


def _split3(x):
    hi = x.astype(BF16)
    r1 = x - hi.astype(F32)
    mid = r1.astype(BF16)
    lo = (r1 - mid.astype(F32)).astype(BF16)
    return hi, mid, lo


def _layer_norm(x, g, b):
    mu = jnp.mean(x, axis=-1, keepdims=True)
    xc = x - mu
    var = jnp.mean(xc * xc, axis=-1, keepdims=True)
    return xc * lax.rsqrt(var + LN_EPS) * g + b


def _proj_a_kernel(x_ref, w_ref, q_ref, k_ref, v_ref, r_ref, mq_ref, g_ref):
    xb = x_ref[...].astype(BF16)
    c = 0
    for ref in (q_ref, k_ref, v_ref, r_ref, mq_ref, g_ref):
        n = ref.shape[1]
        ref[...] = _dot(xb, w_ref[:, c:c + n]).astype(ref.dtype)
        c += n


def _proj_a(x2d, w, tm):
    m = x2d.shape[0]
    widths = (GLA_KPAD, GLA_KPAD, SELF_W, SELF_W, MEM_W, LANE)
    assert w.shape == (D_MODEL, sum(widths))
    return pl.pallas_call(
        _proj_a_kernel,
        grid=(m // tm,),
        in_specs=[pl.BlockSpec((tm, D_MODEL), lambda i: (i, 0)),
                  pl.BlockSpec(w.shape, lambda i: (0, 0))],
        out_specs=[pl.BlockSpec((tm, n), lambda i: (i, 0)) for n in widths],
        out_shape=[jax.ShapeDtypeStruct((m, n), BF16) for n in widths],
        compiler_params=_cparams("parallel"),
        name="proj_a",
    )(x2d, w)


def _gla_kernel(q_ref, k_ref, v_ref, r_ref, g_ref, wgu_ref, bg_ref, gn_ref, o_ref, st_ref, la_ref):
    tg = q_ref.shape[0]

    @pl.when(pl.program_id(1) == 0)
    def _():
        st_ref[...] = jnp.zeros_like(st_ref)

    z = _dot(g_ref[...], wgu_ref[...]) + bg_ref[...]
    la_ref[...] = _log_sigmoid(z) * (1.0 / GLA_TAU)

    row = lax.broadcasted_iota(jnp.int32, (CHUNK, CHUNK), 0)
    col = lax.broadcasted_iota(jnp.int32, (CHUNK, CHUNK), 1)
    causal = col <= row
    tri = jnp.where(causal, 1.0, 0.0).astype(BF16)
    lo_half = lax.broadcasted_iota(jnp.int32, (CHUNK, LANE), 1) < (LANE // 2)
    scale = GLA_DK ** -0.5

    def chunk(c, carry):
        rows = pl.ds(c * CHUNK, CHUNK)
        la_hi, la_lo = _split2(la_ref[rows, :])
        b = _dot(tri, la_hi) + _dot(tri, la_lo)
        bl = b[CHUNK - 1:CHUNK, :]
        qc = q_ref[rows, :].astype(F32) * scale
        kc = k_ref[rows, :].astype(F32)
        qd = (qc * jnp.exp(b)).astype(BF16)
        ki = (kc * jnp.exp(-b)).astype(BF16)
        ke = (kc * jnp.exp(bl - b)).astype(BF16)
        dec = jnp.exp(bl)
        unit = []
        for h in range(GLA_HEADS):
            sl = slice(GLA_DK_PAD * h, GLA_DK_PAD * (h + 1))
            qh, kih, keh = qd[:, sl], ki[:, sl], ke[:, sl]
            att = jnp.where(causal, _dot_nt(qh, kih), 0.0).astype(BF16)
            for t, blk in enumerate(GLA_HEAD_BLOCKS[h]):
                u = 2 * h + t
                vb = v_ref[rows, LANE * blk:LANE * (blk + 1)]
                st = st_ref[u]
                unit.append(_dot(att, vb) + _dot_nt(qh, st.astype(BF16)))
                st_ref[u] = st * dec[:, sl] + _dot_tn(vb, keh)
        o_blk = [unit[0], jnp.where(lo_half, unit[1], unit[2]), unit[3],
                 unit[4], jnp.where(lo_half, unit[5], unit[6]), unit[7]]
        sq = [o * o for o in o_blk]
        full = [jnp.sum(s, axis=-1, keepdims=True) for s in sq]
        lo1 = jnp.sum(jnp.where(lo_half, sq[1], 0.0), axis=-1, keepdims=True)
        hi1 = jnp.sum(jnp.where(lo_half, 0.0, sq[1]), axis=-1, keepdims=True)
        lo4 = jnp.sum(jnp.where(lo_half, sq[4], 0.0), axis=-1, keepdims=True)
        hi4 = jnp.sum(jnp.where(lo_half, 0.0, sq[4]), axis=-1, keepdims=True)
        ss = [full[0] + lo1, hi1 + full[2], full[3] + lo4, hi4 + full[5]]
        inv = [lax.rsqrt(s * (1.0 / GLA_DV) + RMS_EPS) for s in ss]
        inv_blk = [inv[0], jnp.where(lo_half, inv[0], inv[1]), inv[1],
                   inv[2], jnp.where(lo_half, inv[2], inv[3]), inv[3]]
        for blk in range(GLA_VBLOCKS):
            cs = slice(LANE * blk, LANE * (blk + 1))
            rg = r_ref[rows, cs].astype(F32)
            y = o_blk[blk] * inv_blk[blk] * gn_ref[:, cs]
            o_ref[rows, cs] = (y * (rg * _sigmoid(rg))).astype(o_ref.dtype)
        return carry

    for c in range(tg // CHUNK):
        chunk(c, 0)


def _gla(q, k, v, r, g, wgu, bg, gn, batch, seq, tg):
    m = batch * seq
    nt = seq // tg
    row_map = lambda b, i: (b * nt + i, 0)
    const = lambda b, i: (0, 0)
    return pl.pallas_call(
        _gla_kernel,
        grid=(batch, nt),
        in_specs=[pl.BlockSpec((tg, GLA_KPAD), row_map),
                  pl.BlockSpec((tg, GLA_KPAD), row_map),
                  pl.BlockSpec((tg, SELF_W), row_map),
                  pl.BlockSpec((tg, SELF_W), row_map),
                  pl.BlockSpec((tg, LANE), row_map),
                  pl.BlockSpec(wgu.shape, const),
                  pl.BlockSpec(bg.shape, const),
                  pl.BlockSpec(gn.shape, const)],
        out_specs=pl.BlockSpec((tg, SELF_W), row_map),
        out_shape=jax.ShapeDtypeStruct((m, SELF_W), BF16),
        scratch_shapes=[pltpu.VMEM((2 * GLA_HEADS, LANE, GLA_DK_PAD), F32),
                        pltpu.VMEM((tg, GLA_KPAD), F32)],
        compiler_params=_cparams("parallel", "arbitrary"),
        name="gla",
    )(q, k, v, r, g, wgu, bg, gn)


def _mem_kv_kernel(m_ref, w_ref, o_ref):
    o_ref[...] = _dot(m_ref[...].astype(BF16), w_ref[...]).astype(o_ref.dtype)


def _mem_kv(mem2d, w):
    m, n = mem2d.shape[0], w.shape[1]
    tm = N_MEM
    return pl.pallas_call(
        _mem_kv_kernel,
        grid=(m // tm,),
        in_specs=[pl.BlockSpec((tm, D_MODEL), lambda i: (i, 0)),
                  pl.BlockSpec(w.shape, lambda i: (0, 0))],
        out_specs=pl.BlockSpec((tm, n), lambda i: (i, 0)),
        out_shape=jax.ShapeDtypeStruct((m, n), BF16),
        compiler_params=_cparams("parallel"),
        name="mem_kv",
    )(mem2d, w)


def _top2_sum(a, b, c, d):
    p, q = jnp.maximum(a, b), jnp.minimum(a, b)
    r, s = jnp.maximum(c, d), jnp.minimum(c, d)
    return jnp.maximum(p, r) + jnp.maximum(jnp.minimum(p, r), jnp.maximum(q, s))


PAIRS = tuple((i, j) for i in range(EXPERTS_PER_GROUP) for j in range(i + 1, EXPERTS_PER_GROUP))
N_CLASSES = N_GROUPS * len(PAIRS)
CLS_PAD = 32
SLOT = D_MODEL // LANE


def _to_slots(ref, x):
    rows = x.shape[0]
    for j in range(SLOT):
        ref[pl.ds(j, rows, stride=SLOT), :] = x[:, LANE * j:LANE * (j + 1)]


def _from_slots(ref, rows):
    return jnp.concatenate([ref[pl.ds(j, rows, stride=SLOT), :] for j in range(SLOT)], axis=1)


def _slot(ref, r):
    return ref.at[pl.ds(pl.multiple_of(r * SLOT, SLOT), SLOT)]


CLASS_EA = np.array([EXPERTS_PER_GROUP * (c // len(PAIRS)) + PAIRS[c % len(PAIRS)][0] for c in range(N_CLASSES)], np.int32)
CLASS_EB = np.array([EXPERTS_PER_GROUP * (c // len(PAIRS)) + PAIRS[c % len(PAIRS)][1] for c in range(N_CLASSES)], np.int32)


def _router(sel):
    score = [_top2_sum(*sel[EXPERTS_PER_GROUP * g:EXPERTS_PER_GROUP * (g + 1)]) for g in range(N_GROUPS)]
    picked = []
    for g in range(N_GROUPS):
        ok = None
        for i in range(N_GROUPS):
            if i == g:
                continue
            c = (score[g] > score[i]) if i < g else (score[g] >= score[i])
            ok = c if ok is None else (ok & c)
        picked.append(ok)
    chosen = []
    for g in range(N_GROUPS):
        for j in range(EXPERTS_PER_GROUP):
            ej = EXPERTS_PER_GROUP * g + j
            rank = None
            for i in range(EXPERTS_PER_GROUP):
                if i == j:
                    continue
                ei = EXPERTS_PER_GROUP * g + i
                beats = (sel[ei] >= sel[ej]) if i < j else (sel[ei] > sel[ej])
                beats = jnp.where(beats, 1.0, 0.0)
                rank = beats if rank is None else rank + beats
            chosen.append(picked[g] & (rank < 1.5))
    return [chosen[CLASS_EA[c]] & chosen[CLASS_EB[c]] for c in range(N_CLASSES)]


def _out_kernel(o_ref, mq_ref, mk_ref, mv_ref, x_ref, wo_ref, lng_ref, lnb_ref,
                wrh_ref, wr2_ref, br_ref, x1_ref, idx_ref, cnt_ref, carry_ref):
    tm = o_ref.shape[0]

    @pl.when(pl.program_id(0) == 0)
    def _():
        carry_ref[...] = jnp.zeros_like(carry_ref)

    mq = mq_ref[...]
    mk = mk_ref[...]
    mv = mv_ref[...]
    lane = lax.broadcasted_iota(jnp.int32, (N_MEM, MEM_W), 1)
    m_out = jnp.zeros((tm, MEM_W), F32)
    for h in range(MEM_HEADS):
        in_head = (lane >= MEM_HD * h) & (lane < MEM_HD * (h + 1))
        s = _dot_nt(mq, jnp.where(in_head, mk, jnp.zeros_like(mk))) * (MEM_HD ** -0.5)
        p = jnp.exp(s - jnp.max(s, axis=-1, keepdims=True))
        l = jnp.sum(p, axis=-1, keepdims=True)
        m_out = m_out + _dot(p.astype(BF16), jnp.where(in_head, mv, jnp.zeros_like(mv))) / l
    y = _dot(o_ref[...], wo_ref[:SELF_W, :]) + _dot(m_out.astype(BF16), wo_ref[SELF_W:, :])
    x1 = _layer_norm(DN_ALPHA * x_ref[...] + y, lng_ref[...], lnb_ref[...])
    _to_slots(x1_ref, x1)

    xh, xl = _split2(x1)
    hh_lh = _dot_nt(wr2_ref[...], xh)
    logits = hh_lh[:N_EXPERTS, :] + hh_lh[N_EXPERTS:, :] + _dot_nt(wrh_ref[...], xl)
    sel = _sigmoid(logits) + br_ref[...]
    onehot = _router([sel[e:e + 1, :] for e in range(N_EXPERTS)])

    csub = lax.broadcasted_iota(jnp.int32, (CLS_PAD, tm), 0)
    oh = jnp.zeros((CLS_PAD, tm), F32)
    for c in range(N_CLASSES):
        oh = jnp.where((csub == c) & onehot[c], 1.0, oh)
    row = lax.broadcasted_iota(jnp.int32, (tm, tm), 0)
    col = lax.broadcasted_iota(jnp.int32, (tm, tm), 1)
    tri_u = jnp.where(row <= col, 1.0, 0.0).astype(BF16)
    incl = _dot(oh.astype(BF16), tri_u)
    carry = carry_ref[...]
    rank = jnp.sum(oh * (carry + incl - 1.0), axis=0, keepdims=True)
    cls = jnp.sum(oh * csub.astype(F32), axis=0, keepdims=True)
    idx_ref[0, 0:1, :] = cls.astype(jnp.int32)
    idx_ref[0, 1:2, :] = rank.astype(jnp.int32)
    carry = carry + jnp.sum(oh, axis=1, keepdims=True)
    carry_ref[...] = carry
    cnt_ref[...] = carry[:, :LANE]


def _out_block(o, mq, mkv, layer, x2d, wo, lng, lnb, wrh, wr2, br, seq, tm):
    m = x2d.shape[0]
    nt = seq // tm
    row_map = lambda i: (i, 0)
    const = lambda i: (0, 0)
    return pl.pallas_call(
        _out_kernel,
        grid=(m // tm,),
        in_specs=[pl.BlockSpec((tm, SELF_W), row_map),
                  pl.BlockSpec((tm, MEM_W), row_map),
                  pl.BlockSpec((N_MEM, MEM_W), lambda i: (i // nt, 2 * layer)),
                  pl.BlockSpec((N_MEM, MEM_W), lambda i: (i // nt, 2 * layer + 1)),
                  pl.BlockSpec((tm, D_MODEL), row_map),
                  pl.BlockSpec(wo.shape, const),
                  pl.BlockSpec(lng.shape, const),
                  pl.BlockSpec(lnb.shape, const),
                  pl.BlockSpec(wrh.shape, const),
                  pl.BlockSpec(wr2.shape, const),
                  pl.BlockSpec(br.shape, const)],
        out_specs=[pl.BlockSpec((tm * SLOT, LANE), row_map),
                   pl.BlockSpec((1, 2, tm), lambda i: (i, 0, 0)),
                   pl.BlockSpec((CLS_PAD, LANE), const)],
        out_shape=[jax.ShapeDtypeStruct((m * SLOT, LANE), F32),
                   jax.ShapeDtypeStruct((m // tm, 2, tm), jnp.int32),
                   jax.ShapeDtypeStruct((CLS_PAD, LANE), F32)],
        scratch_shapes=[pltpu.VMEM((CLS_PAD, tm), F32)],
        compiler_params=_cparams("arbitrary"),
        name="out_ln_router",
    )(o, mq, mkv, mkv, x2d, wo, lng, lnb, wrh, wr2, br)


def _moe_plan(counts, tmx, n_tiles):
    cnt = counts[:N_CLASSES, 0].astype(jnp.int32)
    nt = (cnt + (tmx - 1)) // tmx
    tend = jnp.cumsum(nt)
    offs = (tend - nt) * tmx
    n_used = tend[-1:]
    i = jnp.arange(n_tiles + 1, dtype=jnp.int32)
    ieff = jnp.minimum(i, n_used - 1)
    cls = jnp.sum((ieff[:, None] >= tend[None, :]).astype(jnp.int32), axis=1)
    fresh = jnp.concatenate([jnp.ones((1,), jnp.int32), (cls[1:] != cls[:-1]).astype(jnp.int32)])
    last_tile_row = jnp.where(nt > 0, (tend - 1) * tmx, -1)
    spare = n_used + jnp.arange(N_CLASSES, dtype=jnp.int32)
    spare_row = jnp.where(spare < n_tiles, spare * tmx, -1)
    pad = lambda a: jnp.pad(a, (0, CLS_PAD - N_CLASSES))
    return dict(offs=pad(offs), zero_rows=jnp.concatenate([last_tile_row, spare_row]), row_block=ieff,
                ea=jnp.asarray(CLASS_EA)[cls], eb=jnp.asarray(CLASS_EB)[cls], fresh=fresh, n_used=n_used)


ROW_DMA_UNROLL = 16


def _tile_dest(offs_ref, idx_ref, dest_v, dest_s, sem):
    cls = idx_ref[0, 0:1, :]
    dest = idx_ref[0, 1:2, :]
    for c in range(N_CLASSES):
        dest = dest + jnp.where(cls == c, offs_ref[c], 0)
    dest_v[...] = dest
    cp = pltpu.make_async_copy(dest_v, dest_s, sem)
    cp.start()
    cp.wait()


def _issue_row_copies(n_rows, make_copy):
    def group(g, carry):
        base = pl.multiple_of(g * ROW_DMA_UNROLL, ROW_DMA_UNROLL)
        for k in range(ROW_DMA_UNROLL):
            make_copy(base + k).start(priority=k % 2)
        return carry
    lax.fori_loop(0, n_rows // ROW_DMA_UNROLL, group, 0)


def _dispatch_kernel(offs_ref, zrow_ref, idx_ref, x_ref, xs_ref, dest_v, dest_s, zero_ref, sem_i, sem_z, sem):
    tm = x_ref.shape[0] // SLOT
    zrows = zero_ref.shape[0]

    @pl.when(pl.program_id(0) == 0)
    def _():
        zero_ref[...] = jnp.zeros_like(zero_ref)

        def zcopy(c):
            start = pl.multiple_of(zrow_ref[c] * SLOT, zrows)
            return pltpu.make_async_copy(zero_ref, xs_ref.at[pl.ds(start, zrows)], sem_z)

        for c in range(2 * N_CLASSES):
            @pl.when(zrow_ref[c] >= 0)
            def _(c=c):
                zcopy(c).start()
        for c in range(2 * N_CLASSES):
            @pl.when(zrow_ref[c] >= 0)
            def _(c=c):
                zcopy(c).wait()

    _tile_dest(offs_ref, idx_ref, dest_v, dest_s, sem_i)
    _issue_row_copies(tm, lambda r: pltpu.make_async_copy(_slot(x_ref, r), _slot(xs_ref, dest_s[0, r]), sem))
    pltpu.make_async_copy(x_ref, xs_ref.at[pl.ds(0, tm * SLOT)], sem).wait()


def _dispatch(plan, idx, x1t, tm, tmx, n_rows):
    m = x1t.shape[0] // SLOT
    return pl.pallas_call(
        _dispatch_kernel,
        grid_spec=pltpu.PrefetchScalarGridSpec(
            num_scalar_prefetch=2,
            grid=(m // tm,),
            in_specs=[pl.BlockSpec((1, 2, tm), lambda i, *_: (i, 0, 0)),
                      pl.BlockSpec((tm * SLOT, LANE), lambda i, *_: (i, 0))],
            out_specs=pl.BlockSpec(memory_space=pl.ANY),
            scratch_shapes=[pltpu.VMEM((1, tm), jnp.int32),
                            pltpu.SMEM((1, tm), jnp.int32),
                            pltpu.VMEM((tmx * SLOT, LANE), F32),
                            pltpu.SemaphoreType.DMA, pltpu.SemaphoreType.DMA, pltpu.SemaphoreType.DMA]),
        out_shape=jax.ShapeDtypeStruct((n_rows * SLOT, LANE), F32),
        compiler_params=_cparams("arbitrary"),
        name="moe_dispatch",
    )(plan["offs"], plan["zero_rows"], idx, x1t)


def _moe_ffn_kernel(rb_ref, ea_ref, eb_ref, fresh_ref, nu_ref,
                    xs_ref, wga_ref, wua_ref, wgb_ref, wub_ref, wda_ref, wdb_ref, wr_ref, lng_ref, lnb_ref,
                    ys_ref, sg_ref, su_ref, sd_ref, h_ref, x_ref, g_ref):
    i = pl.program_id(0)
    t = i - 1
    n_used = nu_ref[0]
    tmx = ys_ref.shape[0] // SLOT
    slot_a = i & 1
    slot_b = 1 - slot_a

    @pl.when((i < n_used) & (fresh_ref[i] == 1))
    def _():
        for s, (g, u) in enumerate(((wga_ref, wua_ref), (wgb_ref, wub_ref))):
            sg_ref[s] = g[0, 0].astype(BF16)
            su_ref[s] = u[0, 0].astype(BF16)

    @pl.when((t >= 0) & (t < n_used) & (fresh_ref[jnp.maximum(t, 0)] == 1))
    def _():
        for s, d in enumerate((wda_ref, wdb_ref)):
            sd_ref[s] = d[0, 0].astype(BF16)

    def load_tile():
        x = _from_slots(xs_ref, tmx)
        x_ref[slot_a] = x
        aff = _sigmoid(_dot(x.astype(BF16), wr_ref[...]))
        lane = lax.broadcasted_iota(jnp.int32, aff.shape, 1)
        g = [jnp.sum(jnp.where(lane == e_ref[i], aff, 0.0), axis=-1, keepdims=True) for e_ref in (ea_ref, eb_ref)]
        for s in range(2):
            g_ref[slot_a, s] = jnp.broadcast_to(g[s] / (g[0] + g[1]), (tmx, LANE))

    def gate_up():
        xb = x_ref[slot_a].astype(BF16)
        for s in range(2):
            hg = _dot(xb, sg_ref[s])
            hu = _dot(xb, su_ref[s])
            h_ref[slot_a, s] = (hg * _sigmoid(hg) * hu).astype(BF16)

    def down_norm():
        f = None
        for s in range(2):
            fs = g_ref[slot_b, s][:, :1] * _dot(h_ref[slot_b, s], sd_ref[s])
            f = fs if f is None else f + fs
        _to_slots(ys_ref, _layer_norm(DN_ALPHA * x_ref[slot_b] + f, lng_ref[...], lnb_ref[...]))

    @pl.when(i == 0)
    def _():
        load_tile()
        gate_up()

    @pl.when((i >= 1) & (i < n_used))
    def _():
        load_tile()
        down_norm()
        gate_up()

    @pl.when((i >= n_used) & (t < n_used))
    def _():
        down_norm()

    @pl.when(t >= n_used)
    def _():
        ys_ref[...] = jnp.zeros_like(ys_ref)


def _moe_ffn(plan, xs, layer, wg, wu, wd, wr, lng, lnb, tmx, n_tiles):
    prev = lambda i: jnp.maximum(i - 1, 0)
    rows = lambda i, rb, *_: (rb[i], 0)
    wa = lambda i, rb, ea, eb, *_: (layer, ea[i], 0, 0)
    wb = lambda i, rb, ea, eb, *_: (layer, eb[i], 0, 0)
    wa_prev = lambda i, rb, ea, eb, *_: (layer, ea[prev(i)], 0, 0)
    wb_prev = lambda i, rb, ea, eb, *_: (layer, eb[prev(i)], 0, 0)
    const = lambda i, *_: (0, 0)
    gu, dn = (1, 1, D_MODEL, D_EXPERT), (1, 1, D_EXPERT, D_MODEL)
    return pl.pallas_call(
        _moe_ffn_kernel,
        grid_spec=pltpu.PrefetchScalarGridSpec(
            num_scalar_prefetch=5,
            grid=(n_tiles + 1,),
            in_specs=[pl.BlockSpec((tmx * SLOT, LANE), rows),
                      pl.BlockSpec(gu, wa), pl.BlockSpec(gu, wa), pl.BlockSpec(gu, wb), pl.BlockSpec(gu, wb),
                      pl.BlockSpec(dn, wa_prev), pl.BlockSpec(dn, wb_prev),
                      pl.BlockSpec(wr.shape, const),
                      pl.BlockSpec(lng.shape, const), pl.BlockSpec(lnb.shape, const)],
            out_specs=pl.BlockSpec((tmx * SLOT, LANE), lambda i, *_: (prev(i), 0)),
            scratch_shapes=[pltpu.VMEM((2, D_MODEL, D_EXPERT), BF16),
                            pltpu.VMEM((2, D_MODEL, D_EXPERT), BF16),
                            pltpu.VMEM((2, D_EXPERT, D_MODEL), BF16),
                            pltpu.VMEM((2, 2, tmx, D_EXPERT), BF16),
                            pltpu.VMEM((2, tmx, D_MODEL), F32),
                            pltpu.VMEM((2, 2, tmx, LANE), F32)]),
        out_shape=jax.ShapeDtypeStruct(xs.shape, F32),
        compiler_params=_cparams("arbitrary"),
        name="moe_ffn",
    )(plan["row_block"], plan["ea"], plan["eb"], plan["fresh"], plan["n_used"],
      xs, wg, wu, wg, wu, wd, wd, wr, lng, lnb)


def _unpermute_kernel(offs_ref, idx_ref, ys_ref, o_ref, dest_v, dest_s, buf_ref, sem_i, sem):
    tm = o_ref.shape[0]
    _tile_dest(offs_ref, idx_ref, dest_v, dest_s, sem_i)
    _issue_row_copies(tm, lambda r: pltpu.make_async_copy(_slot(ys_ref, dest_s[0, r]), _slot(buf_ref, r), sem))
    pltpu.make_async_copy(ys_ref.at[pl.ds(0, tm * SLOT)], buf_ref, sem).wait()
    o_ref[...] = _from_slots(buf_ref, tm)


def _unpermute(plan, idx, ys, m, tm):
    return pl.pallas_call(
        _unpermute_kernel,
        grid_spec=pltpu.PrefetchScalarGridSpec(
            num_scalar_prefetch=1,
            grid=(m // tm,),
            in_specs=[pl.BlockSpec((1, 2, tm), lambda i, *_: (i, 0, 0)),
                      pl.BlockSpec(memory_space=pl.ANY)],
            out_specs=pl.BlockSpec((tm, D_MODEL), lambda i, *_: (i, 0)),
            scratch_shapes=[pltpu.VMEM((1, tm), jnp.int32),
                            pltpu.SMEM((1, tm), jnp.int32),
                            pltpu.VMEM((tm * SLOT, LANE), F32),
                            pltpu.SemaphoreType.DMA, pltpu.SemaphoreType.DMA]),
        out_shape=jax.ShapeDtypeStruct((m, D_MODEL), F32),
        compiler_params=_cparams("arbitrary"),
        name="moe_unpermute",
    )(plan["offs"], idx, ys)


FOX_AUG = 6


def _fox_placement():
    pl_q = np.zeros((3 * LANE, SELF_W), np.float32)
    pl_k = np.zeros((3 * LANE, SELF_W), np.float32)
    ones_q = np.zeros((1, SELF_W), np.float32)
    ones_k = np.zeros((1, SELF_W), np.float32)
    for h in range(FOX_HEADS):
        base = LANE * (h // 2) + FOX_AUG * (h % 2)
        for piece in range(3):
            pl_q[piece * LANE + h, base + piece] = 1.0
            pl_k[piece * LANE + h, base + 3 + piece] = -1.0
            ones_q[0, base + 3 + piece] = 1.0
            ones_k[0, base + piece] = 1.0
    return (jnp.asarray(pl_q, BF16), jnp.asarray(pl_k, BF16), jnp.asarray(ones_q), jnp.asarray(ones_k))


def _proj_b_kernel(x_ref, wq_ref, wog_ref, wmq_ref, wk_ref, wv_ref, wf_ref, bf_ref, plq_ref, plk_ref,
                   oq_ref, ok_ref, q_ref, k_ref, v_ref, og_ref, mq_ref, aq_ref, ak_ref, carry_ref):
    tm = x_ref.shape[0]

    @pl.when(pl.program_id(1) == 0)
    def _():
        carry_ref[...] = jnp.zeros_like(carry_ref)

    xb = x_ref[...].astype(BF16)
    for w_ref, o_ref in ((wq_ref, q_ref), (wk_ref, k_ref), (wv_ref, v_ref), (wog_ref, og_ref), (wmq_ref, mq_ref)):
        o_ref[...] = _dot(xb, w_ref[...]).astype(o_ref.dtype)

    log_f = _log_sigmoid(_dot(xb, wf_ref[...]) + bf_ref[...])
    row = lax.broadcasted_iota(jnp.int32, (tm, tm), 0)
    col = lax.broadcasted_iota(jnp.int32, (tm, tm), 1)
    tri = jnp.where(col <= row, 1.0, 0.0).astype(BF16)
    f_hi, f_mid, f_lo = _split3(log_f)
    cum = carry_ref[...] + (_dot(tri, f_hi) + _dot(tri, f_mid) + _dot(tri, f_lo))
    carry_ref[...] = cum[tm - 1:tm, :]
    c3 = jnp.concatenate(_split3(cum * LOG2E), axis=1)
    aq_ref[...] = (_dot(c3, plq_ref[...]) + oq_ref[...]).astype(aq_ref.dtype)
    ak_ref[...] = (_dot(c3, plk_ref[...]) + ok_ref[...]).astype(ak_ref.dtype)


def _proj_b(x2d, wq, wog, wmq, wk, wv, wf, bf, batch, seq, tm):
    m = batch * seq
    nt = seq // tm
    row_map = lambda b, i: (b * nt + i, 0)
    const = lambda b, i: (0, 0)
    consts = (wq, wog, wmq, wk, wv, wf, bf) + _fox_placement()
    widths = (SELF_W, SELF_W, SELF_W, SELF_W, MEM_W, SELF_W, SELF_W)
    return pl.pallas_call(
        _proj_b_kernel,
        grid=(batch, nt),
        in_specs=[pl.BlockSpec((tm, D_MODEL), row_map)] + [pl.BlockSpec(a.shape, const) for a in consts],
        out_specs=[pl.BlockSpec((tm, n), row_map) for n in widths],
        out_shape=[jax.ShapeDtypeStruct((m, n), BF16) for n in widths],
        scratch_shapes=[pltpu.VMEM((1, LANE), F32)],
        compiler_params=_cparams("parallel", "arbitrary"),
        name="proj_b",
    )(x2d, *consts)


def _pair_rms_norm(t, gain):
    lo_half = lax.broadcasted_iota(jnp.int32, t.shape, 1) < FOX_HD
    sq = t * t
    ss_lo = jnp.sum(jnp.where(lo_half, sq, 0.0), axis=-1, keepdims=True)
    ss_hi = jnp.sum(jnp.where(lo_half, 0.0, sq), axis=-1, keepdims=True)
    ss = jnp.where(lo_half, ss_lo, ss_hi)
    return t * lax.rsqrt(ss * (1.0 / FOX_HD) + RMS_EPS) * gain


SUB = 8
FOX_VROWS = FOX_HD + 16
FOX_QTILES = 8


def _sublane_all(x, op):
    shift = SUB // 2
    while shift:
        x = op(x, pltpu.roll(x, shift, 0))
        shift //= 2
    return x


def _fox_kernel(q_ref, aq_ref, k_ref, ak_ref, v_ref, og_ref, qg_ref, kg_ref, o_ref,
                ka_ref, vt_ref, qa_ref, m_ref, acc_ref, sa_ref, sb_ref):
    tq = sa_ref.shape[1]
    seq = k_ref.shape[1]
    w = pl.program_id(2)

    @pl.when(w == 0)
    def _():
        def fill(c, carry):
            rows = pl.ds(pl.multiple_of(c * tq, tq), tq)
            kn = _pair_rms_norm(k_ref[0, rows, :].astype(F32), kg_ref[...])
            ka_ref[rows, :] = jnp.concatenate([kn.astype(BF16), ak_ref[0, rows, :]], axis=1)
            vt = v_ref[0, rows, :].astype(F32).T.astype(BF16)
            for hh in range(2):
                vt_ref[hh, :FOX_HD, rows] = vt[FOX_HD * hh:FOX_HD * (hh + 1), :]
                vt_ref[hh, FOX_HD:, rows] = jnp.ones((FOX_VROWS - FOX_HD, tq), BF16)
            return carry
        lax.fori_loop(0, seq // tq, fill, 0)

    lane = lax.broadcasted_iota(jnp.int32, (tq, LANE), 1)
    krow = lax.broadcasted_iota(jnp.int32, (tq, tq), 0)
    qcol = lax.broadcasted_iota(jnp.int32, (tq, tq), 1)
    causal = krow <= qcol
    nb = tq // SUB

    def prep(i):
        rows = slice(i * tq, (i + 1) * tq)
        qn = _pair_rms_norm(q_ref[rows, :].astype(F32), qg_ref[...]) * ((FOX_HD ** -0.5) * LOG2E)
        aq = aq_ref[rows, :]
        for hh in range(2):
            feat = (lane >= FOX_HD * hh) & (lane < FOX_HD * (hh + 1))
            bias = (lane >= FOX_AUG * hh) & (lane < FOX_AUG * (hh + 1))
            qa_ref[i % 2, hh] = jnp.concatenate([jnp.where(feat, qn, 0.0).astype(BF16),
                                                 jnp.where(bias, aq, jnp.zeros_like(aq))], axis=1)
        m_ref[i % 2] = jnp.full(m_ref.shape[1:], NEG_BIG, F32)
        acc_ref[i % 2] = jnp.zeros(acc_ref.shape[1:], F32)

    def scores(i, j, s_ref):
        kt = ka_ref[pl.ds(pl.multiple_of(j * tq, tq), tq), :]
        for hh in range(2):
            s_ref[hh] = _dot_nt(kt, qa_ref[i % 2, hh])

    def consume(i, j, s_ref, masked):
        cols = pl.ds(pl.multiple_of(j * tq, tq), tq)
        hq = tq // 2
        for hh in range(2):
            for q0 in (0, hq):
                qs = slice(q0, q0 + hq)
                s = s_ref[hh, :, qs]
                if masked:
                    s = jnp.where(causal[:, qs], s, NEG_BIG)
                s3 = s.reshape(nb, SUB, hq)
                m_old = m_ref[i % 2, hh, :, qs]
                m_new = jnp.maximum(m_old, _sublane_all(jnp.max(s3, axis=0), jnp.maximum))
                alpha = jnp.exp2(m_old - m_new)
                p = jnp.exp2(s3 - m_new[None]).reshape(tq, hq).astype(BF16)
                pv = _dot(vt_ref[hh, :, cols], p)
                acc3 = acc_ref[i % 2, hh, :, qs].reshape(FOX_VROWS // SUB, SUB, hq) * alpha[None]
                acc_ref[i % 2, hh, :, qs] = acc3.reshape(FOX_VROWS, hq) + pv
                m_ref[i % 2, hh, :, qs] = m_new

    def finish(i):
        rows = slice(i * tq, (i + 1) * tq)
        o_t = []
        for hh in range(2):
            l = acc_ref[i % 2, hh, FOX_HD:FOX_HD + SUB, :]
            o_t.append((acc_ref[i % 2, hh, :FOX_HD, :].reshape(FOX_HD // SUB, SUB, tq) / l[None]).reshape(FOX_HD, tq))
        o = jnp.concatenate(o_t, axis=0).T
        og = og_ref[rows, :].astype(F32)
        o_ref[rows, :] = (o * _sigmoid(og)).astype(o_ref.dtype)

    cur, other = sa_ref, sb_ref
    prep(0)
    scores(0, 0, cur)
    for i in range(FOX_QTILES):
        qi = FOX_QTILES * w + i

        def pair(t, carry, i=i, cur=cur, other=other):
            j = 2 * t
            scores(i, j + 1, other)
            consume(i, j, cur, False)
            scores(i, j + 2, cur)
            consume(i, j + 1, other, False)
            return carry

        lax.fori_loop(0, qi // 2, pair, 0)
        last = i == FOX_QTILES - 1
        if i % 2 == 0:
            if not last:
                prep(i + 1)
                scores(i + 1, 0, other)
            consume(i, qi, cur, True)
            cur, other = other, cur
        else:
            scores(i, qi, other)
            consume(i, qi - 1, cur, False)
            if not last:
                prep(i + 1)
                scores(i + 1, 0, cur)
            consume(i, qi, other, True)
        finish(i)


def _fox(q, aq, k, ak, v, og, qg2, kg2, batch, seq, tq):
    tg = FOX_QTILES * tq
    ng = seq // tg
    m = batch * seq
    k3, ak3, v3 = (a.reshape(batch, seq, SELF_W) for a in (k, ak, v))
    tile_map = lambda b, p, i: (b * ng + i, p)
    seq_map = lambda b, p, i: (b, 0, p)
    const = lambda b, p, i: (0, 0)
    return pl.pallas_call(
        _fox_kernel,
        grid=(batch, FOX_PAIRS, ng),
        in_specs=[pl.BlockSpec((tg, LANE), tile_map),
                  pl.BlockSpec((tg, LANE), tile_map),
                  pl.BlockSpec((1, seq, LANE), seq_map),
                  pl.BlockSpec((1, seq, LANE), seq_map),
                  pl.BlockSpec((1, seq, LANE), seq_map),
                  pl.BlockSpec((tg, LANE), tile_map),
                  pl.BlockSpec((1, LANE), const),
                  pl.BlockSpec((1, LANE), const)],
        out_specs=pl.BlockSpec((tg, LANE), tile_map),
        out_shape=jax.ShapeDtypeStruct((m, SELF_W), BF16),
        scratch_shapes=[pltpu.VMEM((seq, 2 * LANE), BF16),
                        pltpu.VMEM((2, FOX_VROWS, seq), BF16),
                        pltpu.VMEM((2, 2, tq, 2 * LANE), BF16),
                        pltpu.VMEM((2, 2, SUB, tq), F32),
                        pltpu.VMEM((2, 2, FOX_VROWS, tq), F32),
                        pltpu.VMEM((2, tq, tq), F32),
                        pltpu.VMEM((2, tq, tq), F32)],
        compiler_params=_cparams("parallel", "parallel", "arbitrary"),
        name="fox",
    )(q, aq, k3, ak3, v3, og, qg2, kg2)


def _pad_heads(w, n_heads, d, d_pad):
    lead = w.shape[:-1]
    w = w.reshape(lead + (n_heads, d))
    w = jnp.pad(w, [(0, 0)] * len(lead) + [(0, 0), (0, d_pad - d)])
    return w.reshape(lead + (n_heads * d_pad,))


def _pad_last(w, n):
    return jnp.pad(w, [(0, 0)] * (w.ndim - 1) + [(0, n - w.shape[-1])])


def kernel(x, mem, w_in_a, w_gate_up_a, b_gate_a, gla_norm_g, w_in_b, q_norm_g, w_kv_shared, b_forget, k_norm_g, w_mem_kv, w_out, ln_mix_g, ln_mix_b, ln_ffn_g, ln_ffn_b, w_router, b_router, w_exp_gate, w_exp_up, w_exp_down):
    batch, seq, d = x.shape
    m = batch * seq
    tm, tmx = TOKEN_TILE, MOE_TILE
    assert d == D_MODEL and seq % (FOX_QTILES * tm) == 0 and m % tmx == 0
    x2d = x.reshape(m, d)

    wa = w_in_a[0]
    s0, s1, s2, s3, s4 = (GLA_KDIM, 2 * GLA_KDIM, 2 * GLA_KDIM + SELF_W,
                          2 * GLA_KDIM + SELF_W + GLA_GATE_RANK, 2 * GLA_KDIM + 2 * SELF_W + GLA_GATE_RANK)
    w_a = jnp.concatenate([
        _pad_heads(wa[:, :s0], GLA_HEADS, GLA_DK, GLA_DK_PAD),
        _pad_heads(wa[:, s0:s1], GLA_HEADS, GLA_DK, GLA_DK_PAD),
        wa[:, s1:s2], wa[:, s3:s4], wa[:, s4:], _pad_last(wa[:, s2:s3], LANE)], axis=1).astype(BF16)
    wgu = _pad_heads(w_gate_up_a[0], GLA_HEADS, GLA_DK, GLA_DK_PAD)
    wgu = jnp.pad(wgu, ((0, LANE - GLA_GATE_RANK), (0, 0))).astype(BF16)
    bg = _pad_heads(b_gate_a[0], GLA_HEADS, GLA_DK, GLA_DK_PAD).reshape(1, GLA_KPAD)
    gn = jnp.tile(gla_norm_g[0], GLA_HEADS).reshape(1, SELF_W)

    wb = w_in_b[0]
    wq, wog, wmq = (wb[:, :SELF_W].astype(BF16), wb[:, SELF_W:2 * SELF_W].astype(BF16),
                    wb[:, 2 * SELF_W:].astype(BF16))
    wk, wv = w_kv_shared[:, :SELF_W].astype(BF16), w_kv_shared[:, SELF_W:2 * SELF_W].astype(BF16)
    wf = _pad_last(w_kv_shared[:, 2 * SELF_W:], LANE).astype(BF16)
    bf = _pad_last(b_forget, LANE).reshape(1, LANE)
    qg2 = jnp.tile(q_norm_g[0], 2).reshape(1, LANE)
    kg2 = jnp.tile(k_norm_g, 2).reshape(1, LANE)

    w_mkv = jnp.concatenate([w_mem_kv[l] for l in range(DEPTH)], axis=1).astype(BF16)
    wo = w_out.astype(BF16)
    wr_t = w_router.T
    wrh = wr_t.astype(BF16)
    wr2 = jnp.concatenate([wrh, (wr_t - wrh.astype(F32)).astype(BF16)], axis=0)
    br = b_router.reshape(N_EXPERTS, 1)
    row = lambda a: a.reshape(1, d)

    mkv = _mem_kv(mem.reshape(batch * N_MEM, d), w_mkv)

    n_tiles = m // tmx + N_CLASSES
    n_rows = n_tiles * tmx

    wr_pad = _pad_last(w_router, LANE).astype(BF16)

    def tail(o, mq, xin, layer):
        x1t, idx, counts = _out_block(o, mq, mkv, layer, xin, wo[layer], row(ln_mix_g[layer]), row(ln_mix_b[layer]),
                                      wrh, wr2, br, seq, tm)
        plan = _moe_plan(counts, tmx, n_tiles)
        xs = _dispatch(plan, idx, x1t, tm, tmx, n_rows)
        ys = _moe_ffn(plan, xs, layer, w_exp_gate, w_exp_up, w_exp_down, wr_pad,
                      row(ln_ffn_g[layer]), row(ln_ffn_b[layer]), tmx, n_tiles)
        return _unpermute(plan, idx, ys, m, tm)

    q, k, v, r, mq, g = _proj_a(x2d, w_a, tm)
    o = _gla(q, k, v, r, g, wgu, bg, gn, batch, seq, tm)
    xa = tail(o, mq, x2d, 0)

    qb, kb, vb, og, mqb, aq, ak = _proj_b(xa, wq, wog, wmq, wk, wv, wf, bf, batch, seq, tm)
    ob = _fox(qb, aq, kb, ak, vb, og, qg2, kg2, batch, seq, tm)
    xb = tail(ob, mqb, xa, 1)
    return xb.reshape(batch, seq, d)
```
